```python
import math
import jax, jax.numpy as jnp
from jax import lax
import numpy as np


D_MODEL = 1024
BATCH = 16
SEQ = 2048
DEPTH = 1

MEM_LEN = 256
DA_HEADS = 8
DA_DK = D_MODEL // (2 * DA_HEADS)
DA_DV = 2 * DA_DK
Q_BLOCK = 128
RET_HEADS = 4
RET_DV = D_MODEL // RET_HEADS
RET_DK = RET_DV // 2
RET_CHUNK = 128
MEM_HEADS = 4
MEM_DH = D_MODEL // MEM_HEADS
N_BRANCH = 3
BRANCH_W = D_MODEL
N_GROUPS = 4
EXPERTS_PER_GROUP = 4
N_EXPERTS = N_GROUPS * EXPERTS_PER_GROUP
EXPERT_TOPK = 2
D_FF_EXPERT = D_MODEL // 2
DEEPNORM_ALPHA = (2.0 * DEPTH) ** 0.25
DEEPNORM_BETA = (8.0 * DEPTH) ** -0.25
EPS = 1e-5

SPLIT_SIZES = (DA_HEADS * 2 * DA_DK, DA_HEADS * 2 * DA_DK, DA_HEADS * DA_DV,
               RET_HEADS * RET_DK, RET_HEADS * RET_DK, RET_HEADS * RET_DV, RET_HEADS * RET_DV,
               MEM_HEADS * MEM_DH, N_BRANCH * D_MODEL)
SPLIT_POINTS = tuple(int(v) for v in np.cumsum(SPLIT_SIZES)[:-1])
IN_WIDTH = int(sum(SPLIT_SIZES))
VALUE_SLOTS = (2, 5)

kernel_name = 'hybrid_diffattn_retention_memxattn_hmoe_deepnorm'


def layer_norm(x, g, b):
    xf = x.astype(jnp.float32)
    mu = jnp.mean(xf, axis=-1, keepdims=True)
    var = jnp.mean(jnp.square(xf - mu), axis=-1, keepdims=True)
    return ((xf - mu) * lax.rsqrt(var + EPS) * g + b).astype(x.dtype)


def diff_attention(q, k, v, lam_vecs, norm_g, lambda_init):
    B, S, H = q.shape[0], q.shape[1], q.shape[2]
    lv = lam_vecs.astype(jnp.float32)
    lam = jnp.exp(jnp.sum(lv[0] * lv[1])) - jnp.exp(jnp.sum(lv[2] * lv[3])) + lambda_init
    slopes = 2.0 ** (-8.0 * (jnp.arange(H, dtype=jnp.float32) + 1.0) / H)
    kf = k.astype(jnp.float32)
    vf = v.astype(jnp.float32)
    kpos = jnp.arange(S)
    nblk = S // Q_BLOCK
    qb = q.astype(jnp.float32).reshape(B, nblk, Q_BLOCK, H, 2, DA_DK).transpose(1, 0, 2, 3, 4, 5)
    scale = DA_DK ** -0.5

    def block(args):
        q_blk, i = args
        qpos = i * Q_BLOCK + jnp.arange(Q_BLOCK)
        dist = (qpos[:, None] - kpos[None, :]).astype(jnp.float32)
        s = jnp.einsum('bqhcd,bkhcd->bhcqk', q_blk, kf) * scale
        s = s - slopes[None, :, None, None, None] * dist[None, None, None]
        s = jnp.where(dist[None, None, None] >= 0, s, -jnp.inf)
        p = jax.nn.softmax(s, axis=-1)
        a = p[:, :, 0] - lam * p[:, :, 1]
        return jnp.einsum('bhqk,bkhe->bqhe', a, vf)

    o = lax.map(block, (qb, jnp.arange(nblk)))
    o = o.transpose(1, 0, 2, 3, 4).reshape(B, S, H, DA_DV)
    o = o * lax.rsqrt(jnp.mean(jnp.square(o), axis=-1, keepdims=True) + EPS) * norm_g
    return (o * (1.0 - lambda_init)).reshape(B, S, H * DA_DV)


def retention(q, k, v, gn_g, gn_b):
    B, S, H, dk = q.shape
    dv = v.shape[-1]
    C = RET_CHUNK
    n = S // C
    log_g = jnp.log(1.0 - 2.0 ** (-5.0 - jnp.arange(H, dtype=jnp.float32)))
    idx = jnp.arange(C, dtype=jnp.float32)
    rel = idx[:, None] - idx[None, :]
    intra = jnp.where(rel[None] >= 0, jnp.exp(rel[None] * log_g[:, None, None]), 0.0)
    q_decay = jnp.exp((idx[:, None] + 1.0) * log_g[None, :])
    k_decay = jnp.exp((C - 1.0 - idx[:, None]) * log_g[None, :])
    chunk_decay = jnp.exp(C * log_g)
    qf = q.astype(jnp.float32)
    kf = k.astype(jnp.float32) * dk ** -0.5
    vf = v.astype(jnp.float32)
    qc = qf.reshape(B, n, C, H, dk).transpose(1, 0, 2, 3, 4)
    kc = kf.reshape(B, n, C, H, dk).transpose(1, 0, 2, 3, 4)
    vc = vf.reshape(B, n, C, H, dv).transpose(1, 0, 2, 3, 4)

    def step(state, inp):
        qi, ki, vi = inp
        s = jnp.einsum('bqhd,bkhd->bhqk', qi, ki) * intra[None]
        o = (jnp.einsum('bhqk,bkhe->bqhe', s, vi)
             + jnp.einsum('bqhd,bhde->bqhe', qi, state) * q_decay[None, :, :, None])
        new_state = (state * chunk_decay[None, :, None, None]
                     + jnp.einsum('bkhd,bkhe->bhde', ki * k_decay[None, :, :, None], vi))
        return new_state, o

    state0 = jnp.zeros((B, H, dk, dv), jnp.float32)
    _, o = lax.scan(step, state0, (qc, kc, vc))
    o = o.transpose(1, 0, 2, 3, 4).reshape(B, S, H, dv)
    mu = jnp.mean(o, axis=-1, keepdims=True)
    var = jnp.mean(jnp.square(o - mu), axis=-1, keepdims=True)
    o = ((o - mu) * lax.rsqrt(var + EPS)).reshape(B, S, H * dv)
    return o * gn_g + gn_b


def memory_attention(q, k, v):
    B, S = q.shape[0], q.shape[1]
    s = jnp.einsum('bshd,bmhd->bhsm', q.astype(jnp.float32), k.astype(jnp.float32)) * MEM_DH ** -0.5
    p = jax.nn.softmax(s, axis=-1)
    o = jnp.einsum('bhsm,bmhd->bshd', p, v.astype(jnp.float32))
    return o.reshape(B, S, MEM_HEADS * MEM_DH)


def hier_moe(x, w_rg, b_rg, w_re, b_re, w1, w3, w2):
    B, S, D = x.shape
    t = x.reshape(B * S, D)
    p_grp = jax.nn.softmax((t @ w_rg + b_rg).astype(jnp.float32), axis=-1)
    p_top, g_idx = lax.top_k(p_grp, 1)
    e_logits = (t @ w_re + b_re).astype(jnp.float32).reshape(B * S, N_GROUPS, EXPERTS_PER_GROUP)
    sel = jnp.take_along_axis(e_logits, g_idx[:, :, None], axis=1)[:, 0]
    p_exp = jax.nn.softmax(sel, axis=-1)
    pe_top, e_idx = lax.top_k(p_exp, EXPERT_TOPK)
    pe_top = pe_top / jnp.sum(pe_top, axis=-1, keepdims=True)
    w_tok = p_top * pe_top
    flat_idx = g_idx * EXPERTS_PER_GROUP + e_idx
    combine = jnp.sum(jax.nn.one_hot(flat_idx, N_EXPERTS, dtype=jnp.float32) * w_tok[..., None], axis=1)
    out = jnp.zeros((B * S, D), jnp.float32)
    for e in range(N_EXPERTS):
        h = jax.nn.silu(t @ w1[e]) * (t @ w3[e])
        out = out + combine[:, e:e + 1] * (h @ w2[e])
    return out.reshape(B, S, D).astype(x.dtype)


def setup_inputs(seed: int = 0) -> dict:
    key = jax.random.key(seed)
    ks = jax.random.split(key, 24)
    D = D_MODEL

    def nrm(k, shape, scale):
        return jax.random.normal(k, shape, jnp.float32) * scale

    col_scale = np.concatenate([np.full(sz, DEEPNORM_BETA if i in VALUE_SLOTS else 1.0, np.float32)
                                for i, sz in enumerate(SPLIT_SIZES)])
    mem_scale = np.concatenate([np.ones(D, np.float32), np.full(D, DEEPNORM_BETA, np.float32)])
    return {
        'x': nrm(ks[0], (BATCH, SEQ, D), 1.0),
        'mem': nrm(ks[1], (BATCH, MEM_LEN, D), 1.0),
        'w_in': nrm(ks[2], (DEPTH, D, IN_WIDTH), D ** -0.5) * jnp.asarray(col_scale),
        'b_gate': nrm(ks[3], (DEPTH, N_BRANCH * D), 0.02),
        'w_mem_kv': nrm(ks[4], (DEPTH, D, 2 * D), D ** -0.5) * jnp.asarray(mem_scale),
        'da_lambda': nrm(ks[5], (DEPTH, 4, DA_DK), 0.1),
        'da_norm_g': 1.0 + nrm(ks[6], (DEPTH, DA_DV), 0.02),
        'ret_gn_g': 1.0 + nrm(ks[7], (DEPTH, RET_HEADS * RET_DV), 0.02),
        'ret_gn_b': nrm(ks[8], (DEPTH, RET_HEADS * RET_DV), 0.02),
        'w_branch': nrm(ks[9], (DEPTH, N_BRANCH, BRANCH_W, D), BRANCH_W ** -0.5),
        'w_o': nrm(ks[10], (DEPTH, D, D), D ** -0.5 * DEEPNORM_BETA),
        'ln1_g': 1.0 + nrm(ks[11], (DEPTH, D), 0.02),
        'ln1_b': nrm(ks[12], (DEPTH, D), 0.02),
        'w_rg': nrm(ks[13], (DEPTH, D, N_GROUPS), D ** -0.5),
        'b_rg': nrm(ks[14], (DEPTH, N_GROUPS), 0.01),
        'w_re': nrm(ks[15], (DEPTH, D, N_EXPERTS), D ** -0.5),
        'b_re': nrm(ks[16], (DEPTH, N_EXPERTS), 0.01),
        'w1': nrm(ks[17], (DEPTH, N_EXPERTS, D, D_FF_EXPERT), D ** -0.5),
        'w3': nrm(ks[18], (DEPTH, N_EXPERTS, D, D_FF_EXPERT), D ** -0.5),
        'w2': nrm(ks[19], (DEPTH, N_EXPERTS, D_FF_EXPERT, D), D_FF_EXPERT ** -0.5 * DEEPNORM_BETA),
        'ln2_g': 1.0 + nrm(ks[20], (DEPTH, D), 0.02),
        'ln2_b': nrm(ks[21], (DEPTH, D), 0.02),
    }


def reference(x, mem, w_in, b_gate, w_mem_kv, da_lambda, da_norm_g, ret_gn_g, ret_gn_b,
              w_branch, w_o, ln1_g, ln1_b, w_rg, b_rg, w_re, b_re, w1, w3, w2, ln2_g, ln2_b):
    B, S, D = x.shape
    M = mem.shape[1]
    for l in range(DEPTH):
        lambda_init = 0.8 - 0.6 * math.exp(-0.3 * l)
        proj = x @ w_in[l]
        da_q, da_k, da_v, r_q, r_k, r_v, r_g, m_q, gate_logits = jnp.split(proj, SPLIT_POINTS, axis=-1)

        y_da = diff_attention(da_q.reshape(B, S, DA_HEADS, 2, DA_DK),
                              da_k.reshape(B, S, DA_HEADS, 2, DA_DK),
                              da_v.reshape(B, S, DA_HEADS, DA_DV),
                              da_lambda[l], da_norm_g[l], lambda_init).astype(x.dtype)

        y_ret = retention(r_q.reshape(B, S, RET_HEADS, RET_DK),
                          r_k.reshape(B, S, RET_HEADS, RET_DK),
                          r_v.reshape(B, S, RET_HEADS, RET_DV),
                          ret_gn_g[l], ret_gn_b[l])
        y_ret = (jax.nn.silu(r_g.astype(jnp.float32)) * y_ret).astype(x.dtype)

        m_k, m_v = jnp.split(mem @ w_mem_kv[l], 2, axis=-1)
        y_mem = memory_attention(m_q.reshape(B, S, MEM_HEADS, MEM_DH),
                                 m_k.reshape(B, M, MEM_HEADS, MEM_DH),
                                 m_v.reshape(B, M, MEM_HEADS, MEM_DH)).astype(x.dtype)

        gates = jax.nn.sigmoid(gate_logits + b_gate[l]).reshape(B, S, N_BRANCH, D)
        branches = jnp.stack([y_da, y_ret, y_mem], axis=2)
        branch_d = jnp.einsum('bsnc,ncd->bsnd', branches, w_branch[l])
        merged = jnp.sum(gates * branch_d, axis=2)
        x = layer_norm(DEEPNORM_ALPHA * x + merged @ w_o[l], ln1_g[l], ln1_b[l])

        y_moe = hier_moe(x, w_rg[l], b_rg[l], w_re[l], b_re[l], w1[l], w3[l], w2[l])
        x = layer_norm(DEEPNORM_ALPHA * x + y_moe, ln2_g[l], ln2_b[l])
    return x
```

```python
import functools
import math

import jax
import jax.numpy as jnp
from jax import lax
from jax.experimental import pallas as pl
from jax.experimental.pallas import tpu as pltpu

BF16 = jnp.bfloat16
F32 = jnp.float32

DA_HEADS = 8
RET_HEADS = 4
RET_CHUNK = 128
MEM_HEADS = 4
N_BRANCH = 3
N_GROUPS = 4
EXPERTS_PER_GROUP = 4
N_EXPERTS = N_GROUPS * EXPERTS_PER_GROUP
EPS = 1e-5
LANES = 128
VMEM_LIMIT = 56 * 1024 * 1024

_NT = (((1,), (1,)), ((), ()))


def _params(n_axes):
    return pltpu.CompilerParams(dimension_semantics=("arbitrary",) * n_axes,
                                vmem_limit_bytes=VMEM_LIMIT)


def _layer_norm(z, g, b):
    mu = jnp.mean(z, axis=-1, keepdims=True)
    zc = z - mu
    var = jnp.mean(zc * zc, axis=-1, keepdims=True)
    return zc * lax.rsqrt(var + EPS) * g + b


def _matmul_kernel(x_ref, w_ref, o_ref):
    o_ref[...] = jnp.dot(x_ref[...].astype(BF16), w_ref[...],
                         preferred_element_type=F32).astype(o_ref.dtype)


def _matmul(x, w, *, tm, tn, name):
    m, k = x.shape
    n = w.shape[1]
    tm, tn = min(tm, m), min(tn, n)
    return pl.pallas_call(
        _matmul_kernel,
        grid=(m // tm, n // tn),
        in_specs=[pl.BlockSpec((tm, k), lambda i, j: (i, 0)),
                  pl.BlockSpec((k, tn), lambda i, j: (0, j))],
        out_specs=pl.BlockSpec((tm, tn), lambda i, j: (i, j)),
        out_shape=jax.ShapeDtypeStruct((m, n), BF16),
        compiler_params=_params(2),
        name=name,
    )(x, w)


def _da_kernel(slope_ref, lam_ref, g_ref, q_ref, k_ref, v_ref, o_ref, *, tq, dk, lambda_init):
    h = pl.program_id(1)
    qi = pl.program_id(2)
    slope = slope_ref[h]
    lane = lax.broadcasted_iota(jnp.int32, (tq, 2 * dk), 1)
    qf = q_ref[...].astype(F32) * (dk ** -0.5)
    q_comp = (jnp.where(lane < dk, qf, 0.0).astype(BF16),
              jnp.where(lane >= dk, qf, 0.0).astype(BF16))
    row = lax.broadcasted_iota(jnp.int32, (tq, tq), 0)
    col = lax.broadcasted_iota(jnp.int32, (tq, tq), 1)
    bias = slope * (col - row).astype(F32)

    def block(j, carry, masked):
        start = pl.multiple_of(j * tq, tq)
        k = k_ref[pl.ds(start, tq), :]
        v = v_ref[pl.ds(start, tq), :]
        off = slope * ((j - qi) * tq).astype(F32)
        new = []
        for c in range(2):
            m, l, acc = carry[c]
            s = lax.dot_general(q_comp[c], k, _NT, preferred_element_type=F32) + bias
            if masked:
                s = jnp.where(col <= row, s, -jnp.inf)
            m_new = jnp.maximum(m, jnp.max(s, axis=1, keepdims=True) + off)
            p = jnp.exp(s - (m_new - off))
            alpha = jnp.exp(m - m_new)
            l_new = alpha * l + jnp.sum(p, axis=1, keepdims=True)
            acc_new = alpha * acc + jnp.dot(p.astype(BF16), v, preferred_element_type=F32)
            new.append((m_new, l_new, acc_new))
        return tuple(new)

    init_c = (jnp.full((tq, 1), -jnp.inf, F32), jnp.zeros((tq, 1), F32),
              jnp.zeros((tq, 2 * dk), F32))
    carry = lax.fori_loop(0, qi, lambda j, c: block(j, c, False), (init_c, init_c))
    (_, l0, a0), (_, l1, a1) = block(qi, carry, True)

    lv = lam_ref[...]
    lam = (jnp.exp(jnp.sum(lv[0:1] * lv[1:2], axis=1, keepdims=True))
           - jnp.exp(jnp.sum(lv[2:3] * lv[3:4], axis=1, keepdims=True)) + lambda_init)
    o = a0 / l0 - lam * (a1 / l1)
    o = o * lax.rsqrt(jnp.mean(o * o, axis=1, keepdims=True) + EPS) * g_ref[...]
    o_ref[...] = (o * (1.0 - lambda_init)).astype(o_ref.dtype)


def _diff_attention(proj, lam_vecs, norm_g, *, batch, seq, d, lambda_init, tq):
    dk = d // (2 * DA_HEADS)
    w = 2 * dk
    tq = min(tq, seq)
    nq = seq // tq
    hpb = d // w
    slopes = 2.0 ** (-8.0 * (jnp.arange(DA_HEADS, dtype=F32) + 1.0) / DA_HEADS)
    kern = functools.partial(_da_kernel, tq=tq, dk=dk, lambda_init=lambda_init)
    return pl.pallas_call(
        kern,
        grid=(batch, DA_HEADS, nq),
        in_specs=[pl.BlockSpec(memory_space=pltpu.SMEM),
                  pl.BlockSpec((4, dk), lambda b, h, i: (0, 0)),
                  pl.BlockSpec((1, w), lambda b, h, i: (0, 0)),
                  pl.BlockSpec((tq, w), lambda b, h, i: (b * nq + i, h)),
                  pl.BlockSpec((seq, w), lambda b, h, i: (b, hpb + h)),
                  pl.BlockSpec((seq, w), lambda b, h, i: (b, 2 * hpb + h))],
        out_specs=pl.BlockSpec((tq, w), lambda b, h, i: (b * nq + i, h)),
        out_shape=jax.ShapeDtypeStruct((batch * seq, d), BF16),
        compiler_params=_params(3),
        name="diff_attention",
    )(slopes, lam_vecs, norm_g.reshape(1, w), proj, proj, proj)


def _ret_kernel(cd_ref, intra_ref, qd_ref, kd_ref, gng_ref, gnb_ref,
                q_ref, k_ref, v_ref, g_ref, o_ref, *, n_chunks, dk, dv):
    h = pl.program_id(1)
    cd = cd_ref[h]
    intra = intra_ref[...]
    qd = qd_ref[...]
    kd = kd_ref[...]
    gng = gng_ref[...]
    gnb = gnb_ref[...]
    c_len = RET_CHUNK

    def chunk(c, state):
        start = pl.multiple_of(c * c_len, c_len)
        q = q_ref[pl.ds(start, c_len), :]
        k = k_ref[pl.ds(start, c_len), :]
        v = v_ref[pl.ds(start, c_len), :]
        s = lax.dot_general(q, k, _NT, preferred_element_type=F32) * intra
        o = (jnp.dot(s.astype(BF16), v, preferred_element_type=F32)
             + jnp.dot(q, state.astype(BF16), preferred_element_type=F32) * qd)
        kt = (k.astype(F32) * kd).T.astype(BF16)
        new_state = state * cd + jnp.dot(kt, v, preferred_element_type=F32)
        mu = jnp.mean(o, axis=1, keepdims=True)
        oc = o - mu
        var = jnp.mean(oc * oc, axis=1, keepdims=True)
        y = oc * lax.rsqrt(var + EPS) * gng + gnb
        gate = g_ref[pl.ds(start, c_len), :].astype(F32)
        y = gate * (1.0 / (1.0 + jnp.exp(-gate))) * y
        o_ref[pl.ds(start, c_len), :] = y.astype(o_ref.dtype)
        return new_state

    lax.fori_loop(0, n_chunks, chunk, jnp.zeros((dk, dv), F32))


def _retention(proj, gn_g, gn_b, *, batch, seq, d):
    dv = d // RET_HEADS
    dk = dv // 2
    c_len = RET_CHUNK
    n_chunks = seq // c_len
    log_g = jnp.log(1.0 - 2.0 ** (-5.0 - jnp.arange(RET_HEADS, dtype=F32)))
    idx = jnp.arange(c_len, dtype=F32)
    rel = idx[:, None] - idx[None, :]
    intra = jnp.where(rel[None] >= 0, jnp.exp(rel[None] * log_g[:, None, None]), 0.0) * dk ** -0.5
    q_decay = jnp.exp((idx[None, :] + 1.0) * log_g[:, None])[:, :, None]
    k_decay = (jnp.exp((c_len - 1.0 - idx[None, :]) * log_g[:, None]) * dk ** -0.5)[:, :, None]
    chunk_decay = jnp.exp(c_len * log_g)
    q_blk = 3 * d // dk
    k_blk = q_blk + RET_HEADS
    v_blk = (3 * d + d) // dv
    g_blk = v_blk + RET_HEADS
    kern = functools.partial(_ret_kernel, n_chunks=n_chunks, dk=dk, dv=dv)
    return pl.pallas_call(
        kern,
        grid=(batch, RET_HEADS),
        in_specs=[pl.BlockSpec(memory_space=pltpu.SMEM),
                  pl.BlockSpec((None, c_len, c_len), lambda b, h: (h, 0, 0)),
                  pl.BlockSpec((None, c_len, 1), lambda b, h: (h, 0, 0)),
                  pl.BlockSpec((None, c_len, 1), lambda b, h: (h, 0, 0)),
                  pl.BlockSpec((1, dv), lambda b, h: (0, h)),
                  pl.BlockSpec((1, dv), lambda b, h: (0, h)),
                  pl.BlockSpec((seq, dk), lambda b, h: (b, q_blk + h)),
                  pl.BlockSpec((seq, dk), lambda b, h: (b, k_blk + h)),
                  pl.BlockSpec((seq, dv), lambda b, h: (b, v_blk + h)),
                  pl.BlockSpec((seq, dv), lambda b, h: (b, g_blk + h))],
        out_specs=pl.BlockSpec((seq, dv), lambda b, h: (b, h)),
        out_shape=jax.ShapeDtypeStruct((batch * seq, d), BF16),
        compiler_params=_params(2),
        name="retention",
    )(chunk_decay, intra, q_decay, k_decay, gn_g.reshape(1, d), gn_b.reshape(1, d),
      proj, proj, proj, proj)


def _mem_kernel(q_ref, kv_ref, o_ref, *, d, dh):
    for h in range(MEM_HEADS):
        q = q_ref[:, h * dh:(h + 1) * dh]
        k = kv_ref[:, h * dh:(h + 1) * dh]
        v = kv_ref[:, d + h * dh:d + (h + 1) * dh]
        s = lax.dot_general(q, k, _NT, preferred_element_type=F32) * dh ** -0.5
        p = jnp.exp(s - jnp.max(s, axis=1, keepdims=True))
        l = jnp.sum(p, axis=1, keepdims=True)
        o = jnp.dot(p.astype(BF16), v, preferred_element_type=F32) / l
        o_ref[:, h * dh:(h + 1) * dh] = o.astype(o_ref.dtype)


def _memory_attention(proj, mem_kv, *, batch, seq, mem_len, d, tq):
    dh = d // MEM_HEADS
    tq = min(tq, seq)
    nq = seq // tq
    q_blk = 6
    kern = functools.partial(_mem_kernel, d=d, dh=dh)
    return pl.pallas_call(
        kern,
        grid=(batch, nq),
        in_specs=[pl.BlockSpec((tq, d), lambda b, i: (b * nq + i, q_blk)),
                  pl.BlockSpec((mem_len, 2 * d), lambda b, i: (b, 0))],
        out_specs=pl.BlockSpec((tq, d), lambda b, i: (b * nq + i, 0)),
        out_shape=jax.ShapeDtypeStruct((batch * seq, d), BF16),
        compiler_params=_params(2),
        name="memory_attention",
    )(proj, mem_kv)


def _merge_kernel(x_ref, yda_ref, yret_ref, ymem_ref, g0_ref, g1_ref, g2_ref, bg_ref,
                  wb_ref, wo_ref, lng_ref, lnb_ref, wr_ref, br_ref, x1_ref, lg_ref, *, alpha):
    ys = (yda_ref, yret_ref, ymem_ref)
    gs = (g0_ref, g1_ref, g2_ref)
    merged = None
    for n in range(N_BRANCH):
        gate = gs[n][...].astype(F32) + bg_ref[n:n + 1, :]
        gate = 1.0 / (1.0 + jnp.exp(-gate))
        term = gate * jnp.dot(ys[n][...], wb_ref[n], preferred_element_type=F32)
        merged = term if merged is None else merged + term
    out = jnp.dot(merged.astype(BF16), wo_ref[...], preferred_element_type=F32)
    x1 = _layer_norm(alpha * x_ref[...] + out, lng_ref[...], lnb_ref[...])
    x1_ref[...] = x1
    lg_ref[...] = jnp.dot(x1, wr_ref[...], preferred_element_type=F32,
                          precision=lax.Precision.HIGHEST) + br_ref[...]


def _merge(x2, y_da, y_ret, y_mem, proj, b_gate, wb, wo, ln_g, ln_b, wr, br, *, alpha, tm):
    t, d = x2.shape
    tm = min(tm, t)
    gate_blk = 7
    row = lambda i: (i, 0)
    const2 = lambda i: (0, 0)
    kern = functools.partial(_merge_kernel, alpha=alpha)
    return pl.pallas_call(
        kern,
        grid=(t // tm,),
        in_specs=[pl.BlockSpec((tm, d), row), pl.BlockSpec((tm, d), row),
                  pl.BlockSpec((tm, d), row), pl.BlockSpec((tm, d), row),
                  pl.BlockSpec((tm, d), lambda i: (i, gate_blk)),
                  pl.BlockSpec((tm, d), lambda i: (i, gate_blk + 1)),
                  pl.BlockSpec((tm, d), lambda i: (i, gate_blk + 2)),
                  pl.BlockSpec((N_BRANCH, d), const2),
                  pl.BlockSpec((N_BRANCH, d, d), lambda i: (0, 0, 0)),
                  pl.BlockSpec((d, d), const2),
                  pl.BlockSpec((1, d), const2), pl.BlockSpec((1, d), const2),
                  pl.BlockSpec((d, LANES), const2), pl.BlockSpec((1, LANES), const2)],
        out_specs=[pl.BlockSpec((tm, d), row), pl.BlockSpec((tm, LANES), row)],
        out_shape=[jax.ShapeDtypeStruct((t, d), F32), jax.ShapeDtypeStruct((t, LANES), F32)],
        compiler_params=_params(1),
        name="merge_ln1_router",
    )(x2, y_da, y_ret, y_mem, proj, proj, proj, b_gate, wb, wo, ln_g, ln_b, wr, br)


def _route_kernel(lg_ref, comb_ref):
    lg = lg_ref[...]
    shape = lg.shape
    col = lax.broadcasted_iota(jnp.int32, shape, 1).astype(F32)
    big = float(LANES)
    ninf = -jnp.inf

    def first_argmax(vals, vmax):
        return jnp.min(jnp.where(vals == vmax, col, big), axis=1, keepdims=True)

    gl = jnp.where(col < N_GROUPS, lg, ninf)
    gmax = jnp.max(gl, axis=1, keepdims=True)
    p_top = 1.0 / jnp.sum(jnp.exp(gl - gmax), axis=1, keepdims=True)
    g_idx = first_argmax(gl, gmax)
    lo = N_GROUPS + g_idx * EXPERTS_PER_GROUP
    el = jnp.where((col >= lo) & (col < lo + EXPERTS_PER_GROUP), lg, ninf)
    e1 = jnp.max(el, axis=1, keepdims=True)
    i1 = first_argmax(el, e1)
    el2 = jnp.where(col == i1, ninf, el)
    e2 = jnp.max(el2, axis=1, keepdims=True)
    i2 = first_argmax(el2, e2)
    r = jnp.exp(e2 - e1)
    w1 = p_top / (1.0 + r)
    w2 = p_top * r / (1.0 + r)
    comb_ref[...] = jnp.where(col == i1, w1, 0.0) + jnp.where(col == i2, w2, 0.0)


def _route(logits, *, tm):
    t = logits.shape[0]
    tm = min(tm, t)
    return pl.pallas_call(
        _route_kernel,
        grid=(t // tm,),
        in_specs=[pl.BlockSpec((tm, LANES), lambda i: (i, 0))],
        out_specs=pl.BlockSpec((tm, LANES), lambda i: (i, 0)),
        out_shape=jax.ShapeDtypeStruct((t, LANES), F32),
        compiler_params=_params(1),
        name="routing",
    )(logits)


def _moe_kernel(x_ref, comb_ref, w1_ref, w3_ref, w2_ref, lng_ref, lnb_ref, o_ref, acc_ref, *, alpha):
    e = pl.program_id(1)

    @pl.when(e == 0)
    def _():
        acc_ref[...] = jnp.zeros_like(acc_ref)

    comb = comb_ref[...]
    col = lax.broadcasted_iota(jnp.int32, comb.shape, 1)
    c_e = jnp.sum(jnp.where(col == N_GROUPS + e, comb, 0.0), axis=1, keepdims=True)
    xb = x_ref[...].astype(BF16)
    a = jnp.dot(xb, w1_ref[...], preferred_element_type=F32)
    b = jnp.dot(xb, w3_ref[...], preferred_element_type=F32)
    hid = a * (1.0 / (1.0 + jnp.exp(-a))) * b
    acc_ref[...] += c_e * jnp.dot(hid.astype(BF16), w2_ref[...], preferred_element_type=F32)

    @pl.when(e == N_EXPERTS - 1)
    def _():
        o_ref[...] = _layer_norm(alpha * x_ref[...] + acc_ref[...], lng_ref[...], lnb_ref[...])


def _moe(x1, comb, w1, w3, w2, ln_g, ln_b, *, alpha, tm):
    t, d = x1.shape
    f = w1.shape[-1]
    tm = min(tm, t)
    kern = functools.partial(_moe_kernel, alpha=alpha)
    return pl.pallas_call(
        kern,
        grid=(t // tm, N_EXPERTS),
        in_specs=[pl.BlockSpec((tm, d), lambda i, e: (i, 0)),
                  pl.BlockSpec((tm, LANES), lambda i, e: (i, 0)),
                  pl.BlockSpec((None, d, f), lambda i, e: (e, 0, 0)),
                  pl.BlockSpec((None, d, f), lambda i, e: (e, 0, 0)),
                  pl.BlockSpec((None, f, d), lambda i, e: (e, 0, 0)),
                  pl.BlockSpec((1, d), lambda i, e: (0, 0)),
                  pl.BlockSpec((1, d), lambda i, e: (0, 0))],
        out_specs=pl.BlockSpec((tm, d), lambda i, e: (i, 0)),
        out_shape=jax.ShapeDtypeStruct((t, d), F32),
        scratch_shapes=[pltpu.VMEM((tm, d), F32)],
        compiler_params=_params(2),
        name="experts_ln2",
    )(x1, comb, w1, w3, w2, ln_g, ln_b)


def kernel(x, mem, w_in, b_gate, w_mem_kv, da_lambda, da_norm_g, ret_gn_g, ret_gn_b, w_branch, w_o,
           ln1_g, ln1_b, w_rg, b_rg, w_re, b_re, w1, w3, w2, ln2_g, ln2_b):
    batch, seq, d = x.shape
    mem_len = mem.shape[1]
    depth = w_in.shape[0]
    alpha = (2.0 * depth) ** 0.25
    t = batch * seq
    x2 = x.reshape(t, d)
    for l in range(depth):
        lambda_init = 0.8 - 0.6 * math.exp(-0.3 * l)
        proj = _matmul(x2, w_in[l].astype(BF16), tm=1024, tn=1024, name="in_proj")
        mem_kv = _matmul(mem.reshape(batch * mem_len, d), w_mem_kv[l].astype(BF16),
                         tm=1024, tn=1024, name="mem_kv_proj")
        y_da = _diff_attention(proj, da_lambda[l], da_norm_g[l], batch=batch, seq=seq, d=d,
                               lambda_init=lambda_init, tq=256)
        y_ret = _retention(proj, ret_gn_g[l], ret_gn_b[l], batch=batch, seq=seq, d=d)
        y_mem = _memory_attention(proj, mem_kv, batch=batch, seq=seq, mem_len=mem_len, d=d, tq=512)
        n_route = N_GROUPS + N_EXPERTS
        wr = jnp.pad(jnp.concatenate([w_rg[l], w_re[l]], axis=1), ((0, 0), (0, LANES - n_route)))
        br = jnp.pad(jnp.concatenate([b_rg[l], b_re[l]]), (0, LANES - n_route)).reshape(1, LANES)
        x1, logits = _merge(x2, y_da, y_ret, y_mem, proj, b_gate[l].reshape(N_BRANCH, d),
                            w_branch[l].astype(BF16), w_o[l].astype(BF16),
                            ln1_g[l].reshape(1, d), ln1_b[l].reshape(1, d), wr, br,
                            alpha=alpha, tm=256)
        comb = _route(logits, tm=1024)
        x2 = _moe(x1, comb, w1[l].astype(BF16), w3[l].astype(BF16), w2[l].astype(BF16),
                  ln2_g[l].reshape(1, d), ln2_b[l].reshape(1, d), alpha=alpha, tm=1024)
    return x2.reshape(batch, seq, d)
```

```python
import functools
import math

import jax
import jax.numpy as jnp
import numpy as np
from jax import lax
from jax.experimental import pallas as pl
from jax.experimental.pallas import tpu as pltpu

BF16 = jnp.bfloat16
F32 = jnp.float32

DA_HEADS = 8
RET_HEADS = 4
RET_CHUNK = 128
MEM_HEADS = 4
N_BRANCH = 3
N_GROUPS = 4
EXPERTS_PER_GROUP = 4
N_EXPERTS = N_GROUPS * EXPERTS_PER_GROUP
EPS = 1e-5
LANES = 128
VMEM_LIMIT = 56 * 1024 * 1024

_NT = (((1,), (1,)), ((), ()))


def _params(n_axes):
    return pltpu.CompilerParams(dimension_semantics=("arbitrary",) * n_axes,
                                vmem_limit_bytes=VMEM_LIMIT)


def _layer_norm(z, g, b):
    mu = jnp.mean(z, axis=-1, keepdims=True)
    zc = z - mu
    var = jnp.mean(zc * zc, axis=-1, keepdims=True)
    return zc * lax.rsqrt(var + EPS) * g + b


def _matmul_kernel(x_ref, w_ref, o_ref):
    o_ref[...] = jnp.dot(x_ref[...].astype(BF16), w_ref[...],
                         preferred_element_type=F32).astype(o_ref.dtype)


def _matmul(x, w, *, tm, tn, name):
    m, k = x.shape
    n = w.shape[1]
    tm, tn = min(tm, m), min(tn, n)
    return pl.pallas_call(
        _matmul_kernel,
        grid=(m // tm, n // tn),
        in_specs=[pl.BlockSpec((tm, k), lambda i, j: (i, 0)),
                  pl.BlockSpec((k, tn), lambda i, j: (0, j))],
        out_specs=pl.BlockSpec((tm, tn), lambda i, j: (i, j)),
        out_shape=jax.ShapeDtypeStruct((m, n), BF16),
        compiler_params=_params(2),
        name=name,
    )(x, w)


DA_ONES_ROWS = 16
DA_MASKED = -1e30
DA_CHUNK_ROWS = 32
LOG2_E = 1.4426950408889634


def _da_schedule(n_blk):
    full = [(qi, p) for qi in range(n_blk) for p in range(qi // 2)]
    order = full + [(qi, qi // 2) for qi in range(n_blk)]
    order.append(order[-1])
    table = np.asarray(order, np.int32)
    return len(full), table[:, 0], table[:, 1]


def _da_kernel(sq_ref, sp_ref, slope_ref, lam_ref, g_ref, q_ref, k_ref, v_ref, o_ref,
               ka0_ref, ka1_ref, qa0_ref, qa1_ref, vt_ref, s_ref, acc0_ref, acc1_ref, m0_ref, m1_ref,
               *, tile, dk, n_blk, n_full, lambda_init):
    h = pl.program_id(1)
    a_log2 = slope_ref[h] * LOG2_E
    w = 2 * dk
    ka_refs = (ka0_ref, ka1_ref)
    qa_refs = (qa0_ref, qa1_ref)
    acc_refs = (acc0_ref, acc1_ref)
    m_refs = (m0_ref, m1_ref)
    lane = lax.broadcasted_iota(jnp.int32, (tile, w), 1)
    pos = lax.broadcasted_iota(jnp.int32, (tile, w), 0).astype(F32)
    pieces = []
    rest = jnp.full((tile, w), a_log2, F32)
    for _ in range(3):
        piece = rest.astype(BF16).astype(F32)
        pieces.append(piece)
        rest = rest - piece
    in_comp, k_bias, q_bias = [], [], []
    for c in range(2):
        free = dk * (1 - c)
        in_comp.append((lane >= c * dk) & (lane < (c + 1) * dk))
        kb = jnp.where((lane >= free) & (lane < free + 3), pos, 0.0)
        qb = jnp.where((lane >= free + 3) & (lane < free + 6), -pos, 0.0)
        for i in range(3):
            kb = jnp.where(lane == free + 3 + i, pieces[i], kb)
            qb = jnp.where(lane == free + i, pieces[i], qb)
        k_bias.append(kb)
        q_bias.append(qb)

    def prepare(pp, _):
        for jj in range(2):
            blk = 2 * pp + jj
            rows = pl.ds(pl.multiple_of(blk * tile, tile), tile)
            kf = k_ref[rows, :].astype(F32)
            qf = q_ref[rows, :].astype(F32) * (dk ** -0.5 * LOG2_E)
            for c in range(2):
                ka_refs[c][rows, :] = jnp.where(in_comp[c], kf, k_bias[c]).astype(BF16)
                qa_refs[c][rows, :] = jnp.where(in_comp[c], qf, q_bias[c]).astype(BF16)
                acc_refs[c][blk] = jnp.zeros((w + DA_ONES_ROWS, tile), F32)
                m_refs[c][blk] = jnp.full((1, tile), -jnp.inf, F32)
            vt_ref[pp, 0:w, jj * tile:(jj + 1) * tile] = v_ref[rows, :].astype(F32).T.astype(BF16)
        vt_ref[pp, w:w + DA_ONES_ROWS, :] = jnp.ones((DA_ONES_ROWS, 2 * tile), BF16)
        return 0

    lax.fori_loop(0, n_blk // 2, prepare, 0)

    lv = lam_ref[...]
    lam = (jnp.exp(jnp.sum(lv[0:1] * lv[1:2], axis=1, keepdims=True))
           - jnp.exp(jnp.sum(lv[2:3] * lv[3:4], axis=1, keepdims=True)) + lambda_init)
    chunk = DA_CHUNK_ROWS
    n_chunks = tile // chunk
    rel0 = (lax.broadcasted_iota(jnp.int32, (chunk, tile), 0)
            - lax.broadcasted_iota(jnp.int32, (chunk, tile), 1))

    def issue_scores(t, slot, n_kv):
        qi, p = sq_ref[t], sp_ref[t]
        q_rows = pl.ds(pl.multiple_of(qi * tile, tile), tile)
        for jj in range(n_kv):
            k_rows = pl.ds(pl.multiple_of((2 * p + jj) * tile, tile), tile)
            for c in range(2):
                s_ref[slot, 2 * jj + c] = lax.dot_general(
                    ka_refs[c][k_rows, :], qa_refs[c][q_rows, :], _NT, preferred_element_type=F32)

    def consume(t, slot, diag):
        n_kv = 1 if diag == 0 else 2
        qi, p = sq_ref[t], sp_ref[t]
        offs = [a_log2 * ((2 * p + jj - qi) * tile).astype(F32) for jj in range(n_kv)]
        accs = []
        for c in range(2):
            tiles = [s_ref.at[slot, 2 * jj + c] for jj in range(n_kv)]
            m_old = m_refs[c][qi]
            m_new = m_old
            for jj in range(n_kv):
                col_max = None
                for r in range(n_chunks):
                    rows = slice(r * chunk, (r + 1) * chunk)
                    x = tiles[jj][rows, :]
                    if jj == diag:
                        x = jnp.where(rel0 <= -r * chunk, x, DA_MASKED)
                        tiles[jj][rows, :] = x
                    col_max = x if col_max is None else jnp.maximum(col_max, x)
                m_new = jnp.maximum(m_new, jnp.max(col_max, axis=0, keepdims=True) + offs[jj])
            pts = []
            for jj in range(n_kv):
                shift = m_new - offs[jj]
                for r in range(n_chunks):
                    x = tiles[jj][r * chunk:(r + 1) * chunk, :]
                    pts.append(jnp.exp2(x - shift).astype(BF16))
            vt = vt_ref[p] if n_kv == 2 else vt_ref[p, :, 0:tile]
            pv = jnp.dot(vt, jnp.concatenate(pts, axis=0), preferred_element_type=F32)
            acc = jnp.exp2(m_old - m_new) * acc_refs[c][qi] + pv
            if diag is None:
                acc_refs[c][qi] = acc
                m_refs[c][qi] = m_new
            else:
                accs.append(acc)
        if diag is not None:
            a0, a1 = accs
            ot = a0[0:w] / a0[w:w + 1] - lam * (a1[0:w] / a1[w:w + 1])
            o = ot.T
            o = o * lax.rsqrt(jnp.mean(o * o, axis=1, keepdims=True) + EPS) * g_ref[...]
            q_rows = pl.ds(pl.multiple_of(qi * tile, tile), tile)
            o_ref[q_rows, :] = (o * (1.0 - lambda_init)).astype(o_ref.dtype)

    def two_entries(i, base, last):
        t = base + 2 * i
        issue_scores(t + 1, 1, 2)
        consume(t, 0, 0 if last else None)
        issue_scores(t + 2, 0, 1 if last else 2)
        consume(t + 1, 1, 1 if last else None)
        return 0

    issue_scores(0, 0, 2 if n_full else 1)
    lax.fori_loop(0, n_full // 2, lambda i, _: two_entries(i, 0, False), 0)
    lax.fori_loop(0, n_blk // 2, lambda i, _: two_entries(i, n_full, True), 0)


def _diff_attention(proj, lam_vecs, norm_g, *, batch, seq, d, lambda_init, tile):
    dk = d // (2 * DA_HEADS)
    w = 2 * dk
    tile = min(tile, seq)
    assert tile <= 256 and seq % (2 * tile) == 0
    n_blk = seq // tile
    hpb = d // w
    slopes = 2.0 ** (-8.0 * (jnp.arange(DA_HEADS, dtype=F32) + 1.0) / DA_HEADS)
    n_full, sched_q, sched_p = _da_schedule(n_blk)
    kern = functools.partial(_da_kernel, tile=tile, dk=dk, n_blk=n_blk, n_full=n_full,
                             lambda_init=lambda_init)
    smem = pl.BlockSpec(memory_space=pltpu.SMEM)
    return pl.pallas_call(
        kern,
        grid=(batch, DA_HEADS),
        in_specs=[smem, smem, smem,
                  pl.BlockSpec((4, dk), lambda b, h: (0, 0)),
                  pl.BlockSpec((1, w), lambda b, h: (0, 0)),
                  pl.BlockSpec((seq, w), lambda b, h: (b, h)),
                  pl.BlockSpec((seq, w), lambda b, h: (b, hpb + h)),
                  pl.BlockSpec((seq, w), lambda b, h: (b, 2 * hpb + h))],
        out_specs=pl.BlockSpec((seq, w), lambda b, h: (b, h)),
        out_shape=jax.ShapeDtypeStruct((batch * seq, d), BF16),
        scratch_shapes=[pltpu.VMEM((seq, w), BF16), pltpu.VMEM((seq, w), BF16),
                        pltpu.VMEM((seq, w), BF16), pltpu.VMEM((seq, w), BF16),
                        pltpu.VMEM((n_blk // 2, w + DA_ONES_ROWS, 2 * tile), BF16),
                        pltpu.VMEM((2, 4, tile, tile), F32),
                        pltpu.VMEM((n_blk, w + DA_ONES_ROWS, tile), F32),
                        pltpu.VMEM((n_blk, w + DA_ONES_ROWS, tile), F32),
                        pltpu.VMEM((n_blk, 1, tile), F32),
                        pltpu.VMEM((n_blk, 1, tile), F32)],
        compiler_params=_params(2),
        name="diff_attention",
    )(jnp.asarray(sched_q), jnp.asarray(sched_p), slopes, lam_vecs, norm_g.reshape(1, w),
      proj, proj, proj)


def _ret_kernel(cd_ref, intra_ref, qd_ref, kd_ref, gng_ref, gnb_ref,
                q_ref, k_ref, v_ref, g_ref, o_ref, *, n_chunks, dk, dv):
    h = pl.program_id(1)
    cd = cd_ref[h]
    intra = intra_ref[...]
    qd = qd_ref[...]
    kd = kd_ref[...]
    gng = gng_ref[...]
    gnb = gnb_ref[...]
    c_len = RET_CHUNK

    def chunk(c, state):
        start = pl.multiple_of(c * c_len, c_len)
        q = q_ref[pl.ds(start, c_len), :]
        k = k_ref[pl.ds(start, c_len), :]
        v = v_ref[pl.ds(start, c_len), :]
        s = lax.dot_general(q, k, _NT, preferred_element_type=F32) * intra
        o = (jnp.dot(s.astype(BF16), v, preferred_element_type=F32)
             + jnp.dot(q, state.astype(BF16), preferred_element_type=F32) * qd)
        kt = (k.astype(F32) * kd).T.astype(BF16)
        new_state = state * cd + jnp.dot(kt, v, preferred_element_type=F32)
        mu = jnp.mean(o, axis=1, keepdims=True)
        oc = o - mu
        var = jnp.mean(oc * oc, axis=1, keepdims=True)
        y = oc * lax.rsqrt(var + EPS) * gng + gnb
        gate = g_ref[pl.ds(start, c_len), :].astype(F32)
        y = gate * (1.0 / (1.0 + jnp.exp(-gate))) * y
        o_ref[pl.ds(start, c_len), :] = y.astype(o_ref.dtype)
        return new_state

    lax.fori_loop(0, n_chunks, chunk, jnp.zeros((dk, dv), F32))


def _retention(proj, gn_g, gn_b, *, batch, seq, d):
    dv = d // RET_HEADS
    dk = dv // 2
    c_len = RET_CHUNK
    n_chunks = seq // c_len
    log_g = jnp.log(1.0 - 2.0 ** (-5.0 - jnp.arange(RET_HEADS, dtype=F32)))
    idx = jnp.arange(c_len, dtype=F32)
    rel = idx[:, None] - idx[None, :]
    intra = jnp.where(rel[None] >= 0, jnp.exp(rel[None] * log_g[:, None, None]), 0.0) * dk ** -0.5
    q_decay = jnp.exp((idx[None, :] + 1.0) * log_g[:, None])[:, :, None]
    k_decay = (jnp.exp((c_len - 1.0 - idx[None, :]) * log_g[:, None]) * dk ** -0.5)[:, :, None]
    chunk_decay = jnp.exp(c_len * log_g)
    q_blk = 3 * d // dk
    k_blk = q_blk + RET_HEADS
    v_blk = (3 * d + d) // dv
    g_blk = v_blk + RET_HEADS
    kern = functools.partial(_ret_kernel, n_chunks=n_chunks, dk=dk, dv=dv)
    return pl.pallas_call(
        kern,
        grid=(batch, RET_HEADS),
        in_specs=[pl.BlockSpec(memory_space=pltpu.SMEM),
                  pl.BlockSpec((None, c_len, c_len), lambda b, h: (h, 0, 0)),
                  pl.BlockSpec((None, c_len, 1), lambda b, h: (h, 0, 0)),
                  pl.BlockSpec((None, c_len, 1), lambda b, h: (h, 0, 0)),
                  pl.BlockSpec((1, dv), lambda b, h: (0, h)),
                  pl.BlockSpec((1, dv), lambda b, h: (0, h)),
                  pl.BlockSpec((seq, dk), lambda b, h: (b, q_blk + h)),
                  pl.BlockSpec((seq, dk), lambda b, h: (b, k_blk + h)),
                  pl.BlockSpec((seq, dv), lambda b, h: (b, v_blk + h)),
                  pl.BlockSpec((seq, dv), lambda b, h: (b, g_blk + h))],
        out_specs=pl.BlockSpec((seq, dv), lambda b, h: (b, h)),
        out_shape=jax.ShapeDtypeStruct((batch * seq, d), BF16),
        compiler_params=_params(2),
        name="retention",
    )(chunk_decay, intra, q_decay, k_decay, gn_g.reshape(1, d), gn_b.reshape(1, d),
      proj, proj, proj, proj)


def _mem_kernel(q_ref, kv_ref, o_ref, *, d, dh):
    for h in range(MEM_HEADS):
        q = q_ref[:, h * dh:(h + 1) * dh]
        k = kv_ref[:, h * dh:(h + 1) * dh]
        v = kv_ref[:, d + h * dh:d + (h + 1) * dh]
        s = lax.dot_general(q, k, _NT, preferred_element_type=F32) * dh ** -0.5
        p = jnp.exp(s - jnp.max(s, axis=1, keepdims=True))
        l = jnp.sum(p, axis=1, keepdims=True)
        o = jnp.dot(p.astype(BF16), v, preferred_element_type=F32) / l
        o_ref[:, h * dh:(h + 1) * dh] = o.astype(o_ref.dtype)


def _memory_attention(proj, mem_kv, *, batch, seq, mem_len, d, tq):
    dh = d // MEM_HEADS
    tq = min(tq, seq)
    nq = seq // tq
    q_blk = 6
    kern = functools.partial(_mem_kernel, d=d, dh=dh)
    return pl.pallas_call(
        kern,
        grid=(batch, nq),
        in_specs=[pl.BlockSpec((tq, d), lambda b, i: (b * nq + i, q_blk)),
                  pl.BlockSpec((mem_len, 2 * d), lambda b, i: (b, 0))],
        out_specs=pl.BlockSpec((tq, d), lambda b, i: (b * nq + i, 0)),
        out_shape=jax.ShapeDtypeStruct((batch * seq, d), BF16),
        compiler_params=_params(2),
        name="memory_attention",
    )(proj, mem_kv)


def _merge_kernel(x_ref, yda_ref, yret_ref, ymem_ref, g0_ref, g1_ref, g2_ref, bg_ref,
                  wb_ref, wo_ref, lng_ref, lnb_ref, wr_ref, br_ref, x1_ref, lg_ref, *, alpha):
    ys = (yda_ref, yret_ref, ymem_ref)
    gs = (g0_ref, g1_ref, g2_ref)
    merged = None
    for n in range(N_BRANCH):
        gate = gs[n][...].astype(F32) + bg_ref[n:n + 1, :]
        gate = 1.0 / (1.0 + jnp.exp(-gate))
        term = gate * jnp.dot(ys[n][...], wb_ref[n], preferred_element_type=F32)
        merged = term if merged is None else merged + term
    out = jnp.dot(merged.astype(BF16), wo_ref[...], preferred_element_type=F32)
    x1 = _layer_norm(alpha * x_ref[...] + out, lng_ref[...], lnb_ref[...])
    x1_ref[...] = x1
    lg_ref[...] = jnp.dot(x1, wr_ref[...], preferred_element_type=F32,
                          precision=lax.Precision.HIGHEST) + br_ref[...]


def _merge(x2, y_da, y_ret, y_mem, proj, b_gate, wb, wo, ln_g, ln_b, wr, br, *, alpha, tm):
    t, d = x2.shape
    tm = min(tm, t)
    gate_blk = 7
    row = lambda i: (i, 0)
    const2 = lambda i: (0, 0)
    kern = functools.partial(_merge_kernel, alpha=alpha)
    return pl.pallas_call(
        kern,
        grid=(t // tm,),
        in_specs=[pl.BlockSpec((tm, d), row), pl.BlockSpec((tm, d), row),
                  pl.BlockSpec((tm, d), row), pl.BlockSpec((tm, d), row),
                  pl.BlockSpec((tm, d), lambda i: (i, gate_blk)),
                  pl.BlockSpec((tm, d), lambda i: (i, gate_blk + 1)),
                  pl.BlockSpec((tm, d), lambda i: (i, gate_blk + 2)),
                  pl.BlockSpec((N_BRANCH, d), const2),
                  pl.BlockSpec((N_BRANCH, d, d), lambda i: (0, 0, 0)),
                  pl.BlockSpec((d, d), const2),
                  pl.BlockSpec((1, d), const2), pl.BlockSpec((1, d), const2),
                  pl.BlockSpec((d, LANES), const2), pl.BlockSpec((1, LANES), const2)],
        out_specs=[pl.BlockSpec((tm, d), row), pl.BlockSpec((tm, LANES), row)],
        out_shape=[jax.ShapeDtypeStruct((t, d), F32), jax.ShapeDtypeStruct((t, LANES), F32)],
        compiler_params=_params(1),
        name="merge_ln1_router",
    )(x2, y_da, y_ret, y_mem, proj, proj, proj, b_gate, wb, wo, ln_g, ln_b, wr, br)


def _route_kernel(lg_ref, comb_ref):
    lg = lg_ref[...]
    shape = lg.shape
    col = lax.broadcasted_iota(jnp.int32, shape, 1).astype(F32)
    big = float(LANES)
    ninf = -jnp.inf

    def first_argmax(vals, vmax):
        return jnp.min(jnp.where(vals == vmax, col, big), axis=1, keepdims=True)

    gl = jnp.where(col < N_GROUPS, lg, ninf)
    gmax = jnp.max(gl, axis=1, keepdims=True)
    p_top = 1.0 / jnp.sum(jnp.exp(gl - gmax), axis=1, keepdims=True)
    g_idx = first_argmax(gl, gmax)
    lo = N_GROUPS + g_idx * EXPERTS_PER_GROUP
    el = jnp.where((col >= lo) & (col < lo + EXPERTS_PER_GROUP), lg, ninf)
    e1 = jnp.max(el, axis=1, keepdims=True)
    i1 = first_argmax(el, e1)
    el2 = jnp.where(col == i1, ninf, el)
    e2 = jnp.max(el2, axis=1, keepdims=True)
    i2 = first_argmax(el2, e2)
    r = jnp.exp(e2 - e1)
    w1 = p_top / (1.0 + r)
    w2 = p_top * r / (1.0 + r)
    comb_ref[...] = jnp.where(col == i1, w1, 0.0) + jnp.where(col == i2, w2, 0.0)


def _route(logits, *, tm):
    t = logits.shape[0]
    tm = min(tm, t)
    return pl.pallas_call(
        _route_kernel,
        grid=(t // tm,),
        in_specs=[pl.BlockSpec((tm, LANES), lambda i: (i, 0))],
        out_specs=pl.BlockSpec((tm, LANES), lambda i: (i, 0)),
        out_shape=jax.ShapeDtypeStruct((t, LANES), F32),
        compiler_params=_params(1),
        name="routing",
    )(logits)


def _moe_kernel(x_ref, comb_ref, w1_ref, w3_ref, w2_ref, lng_ref, lnb_ref, o_ref, acc_ref, *, alpha):
    e = pl.program_id(1)

    @pl.when(e == 0)
    def _():
        acc_ref[...] = jnp.zeros_like(acc_ref)

    comb = comb_ref[...]
    col = lax.broadcasted_iota(jnp.int32, comb.shape, 1)
    c_e = jnp.sum(jnp.where(col == N_GROUPS + e, comb, 0.0), axis=1, keepdims=True)
    xb = x_ref[...].astype(BF16)
    a = jnp.dot(xb, w1_ref[...], preferred_element_type=F32)
    b = jnp.dot(xb, w3_ref[...], preferred_element_type=F32)
    hid = a * (1.0 / (1.0 + jnp.exp(-a))) * b
    acc_ref[...] += c_e * jnp.dot(hid.astype(BF16), w2_ref[...], preferred_element_type=F32)

    @pl.when(e == N_EXPERTS - 1)
    def _():
        o_ref[...] = _layer_norm(alpha * x_ref[...] + acc_ref[...], lng_ref[...], lnb_ref[...])


def _moe(x1, comb, w1, w3, w2, ln_g, ln_b, *, alpha, tm):
    t, d = x1.shape
    f = w1.shape[-1]
    tm = min(tm, t)
    kern = functools.partial(_moe_kernel, alpha=alpha)
    return pl.pallas_call(
        kern,
        grid=(t // tm, N_EXPERTS),
        in_specs=[pl.BlockSpec((tm, d), lambda i, e: (i, 0)),
                  pl.BlockSpec((tm, LANES), lambda i, e: (i, 0)),
                  pl.BlockSpec((None, d, f), lambda i, e: (e, 0, 0)),
                  pl.BlockSpec((None, d, f), lambda i, e: (e, 0, 0)),
                  pl.BlockSpec((None, f, d), lambda i, e: (e, 0, 0)),
                  pl.BlockSpec((1, d), lambda i, e: (0, 0)),
                  pl.BlockSpec((1, d), lambda i, e: (0, 0))],
        out_specs=pl.BlockSpec((tm, d), lambda i, e: (i, 0)),
        out_shape=jax.ShapeDtypeStruct((t, d), F32),
        scratch_shapes=[pltpu.VMEM((tm, d), F32)],
        compiler_params=_params(2),
        name="experts_ln2",
    )(x1, comb, w1, w3, w2, ln_g, ln_b)


def kernel(x, mem, w_in, b_gate, w_mem_kv, da_lambda, da_norm_g, ret_gn_g, ret_gn_b, w_branch, w_o,
           ln1_g, ln1_b, w_rg, b_rg, w_re, b_re, w1, w3, w2, ln2_g, ln2_b):
    batch, seq, d = x.shape
    mem_len = mem.shape[1]
    depth = w_in.shape[0]
    alpha = (2.0 * depth) ** 0.25
    t = batch * seq
    x2 = x.reshape(t, d)
    for l in range(depth):
        lambda_init = 0.8 - 0.6 * math.exp(-0.3 * l)
        proj = _matmul(x2, w_in[l].astype(BF16), tm=1024, tn=1024, name="in_proj")
        mem_kv = _matmul(mem.reshape(batch * mem_len, d), w_mem_kv[l].astype(BF16),
                         tm=1024, tn=1024, name="mem_kv_proj")
        y_da = _diff_attention(proj, da_lambda[l], da_norm_g[l], batch=batch, seq=seq, d=d,
                               lambda_init=lambda_init, tile=256)
        y_ret = _retention(proj, ret_gn_g[l], ret_gn_b[l], batch=batch, seq=seq, d=d)
        y_mem = _memory_attention(proj, mem_kv, batch=batch, seq=seq, mem_len=mem_len, d=d, tq=512)
        n_route = N_GROUPS + N_EXPERTS
        wr = jnp.pad(jnp.concatenate([w_rg[l], w_re[l]], axis=1), ((0, 0), (0, LANES - n_route)))
        br = jnp.pad(jnp.concatenate([b_rg[l], b_re[l]]), (0, LANES - n_route)).reshape(1, LANES)
        x1, logits = _merge(x2, y_da, y_ret, y_mem, proj, b_gate[l].reshape(N_BRANCH, d),
                            w_branch[l].astype(BF16), w_o[l].astype(BF16),
                            ln1_g[l].reshape(1, d), ln1_b[l].reshape(1, d), wr, br,
                            alpha=alpha, tm=256)
        comb = _route(logits, tm=1024)
        x2 = _moe(x1, comb, w1[l].astype(BF16), w3[l].astype(BF16), w2[l].astype(BF16),
                  ln2_g[l].reshape(1, d), ln2_b[l].reshape(1, d), alpha=alpha, tm=1024)
    return x2.reshape(batch, seq, d)
```

```python
import functools
import math

import jax
import jax.numpy as jnp
import numpy as np
from jax import lax
from jax.experimental import pallas as pl
from jax.experimental.pallas import tpu as pltpu

BF16 = jnp.bfloat16
F32 = jnp.float32

DA_HEADS = 8
RET_HEADS = 4
RET_CHUNK = 128
MEM_HEADS = 4
N_BRANCH = 3
N_GROUPS = 4
EXPERTS_PER_GROUP = 4
N_EXPERTS = N_GROUPS * EXPERTS_PER_GROUP
EPS = 1e-5
LANES = 128
VMEM_LIMIT = 56 * 1024 * 1024

_NT = (((1,), (1,)), ((), ()))


def _params(n_axes):
    return pltpu.CompilerParams(dimension_semantics=("arbitrary",) * n_axes,
                                vmem_limit_bytes=VMEM_LIMIT)


def _layer_norm(z, g, b):
    mu = jnp.mean(z, axis=-1, keepdims=True)
    zc = z - mu
    var = jnp.mean(zc * zc, axis=-1, keepdims=True)
    return zc * lax.rsqrt(var + EPS) * g + b


def _matmul_kernel(x_ref, w_ref, o_ref):
    o_ref[...] = jnp.dot(x_ref[...].astype(BF16), w_ref[...],
                         preferred_element_type=F32).astype(o_ref.dtype)


def _matmul(x, w, *, tm, tn, name):
    m, k = x.shape
    n = w.shape[1]
    tm, tn = min(tm, m), min(tn, n)
    return pl.pallas_call(
        _matmul_kernel,
        grid=(m // tm, n // tn),
        in_specs=[pl.BlockSpec((tm, k), lambda i, j: (i, 0)),
                  pl.BlockSpec((k, tn), lambda i, j: (0, j))],
        out_specs=pl.BlockSpec((tm, tn), lambda i, j: (i, j)),
        out_shape=jax.ShapeDtypeStruct((m, n), BF16),
        compiler_params=_params(2),
        name=name,
    )(x, w)


DA_ONES_ROWS = 16
DA_MASKED = -1e30
DA_CHUNK_ROWS = 32
LOG2_E = 1.4426950408889634


def _da_schedule(n_blk):
    full = [(qi, p) for qi in range(n_blk) for p in range(qi // 2)]
    order = full + [(qi, qi // 2) for qi in range(n_blk)]
    order.append(order[-1])
    table = np.asarray(order, np.int32)
    return len(full), table[:, 0], table[:, 1]


def _da_kernel(sq_ref, sp_ref, slope_ref, lam_ref, g_ref, q_ref, k_ref, v_ref, o_ref,
               ka0_ref, ka1_ref, qa0_ref, qa1_ref, vt_ref, s_ref, acc0_ref, acc1_ref, m0_ref, m1_ref,
               *, tile, dk, n_blk, n_full, lambda_init):
    h = pl.program_id(1)
    a_log2 = slope_ref[h] * LOG2_E
    w = 2 * dk
    ka_refs = (ka0_ref, ka1_ref)
    qa_refs = (qa0_ref, qa1_ref)
    acc_refs = (acc0_ref, acc1_ref)
    m_refs = (m0_ref, m1_ref)
    lane = lax.broadcasted_iota(jnp.int32, (tile, w), 1)
    pos = lax.broadcasted_iota(jnp.int32, (tile, w), 0).astype(F32)
    pieces = []
    rest = jnp.full((tile, w), a_log2, F32)
    for _ in range(3):
        piece = rest.astype(BF16).astype(F32)
        pieces.append(piece)
        rest = rest - piece
    in_comp, k_bias, q_bias = [], [], []
    for c in range(2):
        free = dk * (1 - c)
        in_comp.append((lane >= c * dk) & (lane < (c + 1) * dk))
        kb = jnp.where((lane >= free) & (lane < free + 3), pos, 0.0)
        qb = jnp.where((lane >= free + 3) & (lane < free + 6), -pos, 0.0)
        for i in range(3):
            kb = jnp.where(lane == free + 3 + i, pieces[i], kb)
            qb = jnp.where(lane == free + i, pieces[i], qb)
        k_bias.append(kb)
        q_bias.append(qb)

    def prepare(pp, _):
        for jj in range(2):
            blk = 2 * pp + jj
            rows = pl.ds(pl.multiple_of(blk * tile, tile), tile)
            kf = k_ref[rows, :].astype(F32)
            qf = q_ref[rows, :].astype(F32) * (dk ** -0.5 * LOG2_E)
            for c in range(2):
                ka_refs[c][rows, :] = jnp.where(in_comp[c], kf, k_bias[c]).astype(BF16)
                qa_refs[c][rows, :] = jnp.where(in_comp[c], qf, q_bias[c]).astype(BF16)
                acc_refs[c][blk] = jnp.zeros((w + DA_ONES_ROWS, tile), F32)
                m_refs[c][blk] = jnp.full((1, tile), -jnp.inf, F32)
            vt_ref[pp, 0:w, jj * tile:(jj + 1) * tile] = v_ref[rows, :].astype(F32).T.astype(BF16)
        vt_ref[pp, w:w + DA_ONES_ROWS, :] = jnp.ones((DA_ONES_ROWS, 2 * tile), BF16)
        return 0

    lax.fori_loop(0, n_blk // 2, prepare, 0)

    lv = lam_ref[...]
    lam = (jnp.exp(jnp.sum(lv[0:1] * lv[1:2], axis=1, keepdims=True))
           - jnp.exp(jnp.sum(lv[2:3] * lv[3:4], axis=1, keepdims=True)) + lambda_init)
    chunk = DA_CHUNK_ROWS
    n_chunks = tile // chunk
    rel0 = (lax.broadcasted_iota(jnp.int32, (chunk, tile), 0)
            - lax.broadcasted_iota(jnp.int32, (chunk, tile), 1))

    def issue_scores(t, slot, n_kv):
        qi, p = sq_ref[t], sp_ref[t]
        q_rows = pl.ds(pl.multiple_of(qi * tile, tile), tile)
        for jj in range(n_kv):
            k_rows = pl.ds(pl.multiple_of((2 * p + jj) * tile, tile), tile)
            for c in range(2):
                s_ref[slot, 2 * jj + c] = lax.dot_general(
                    ka_refs[c][k_rows, :], qa_refs[c][q_rows, :], _NT, preferred_element_type=F32)

    def consume(t, slot, diag):
        n_kv = 1 if diag == 0 else 2
        qi, p = sq_ref[t], sp_ref[t]
        offs = [a_log2 * ((2 * p + jj - qi) * tile).astype(F32) for jj in range(n_kv)]
        accs = []
        for c in range(2):
            tiles = [s_ref.at[slot, 2 * jj + c] for jj in range(n_kv)]
            m_old = m_refs[c][qi]
            m_new = m_old
            for jj in range(n_kv):
                col_max = None
                for r in range(n_chunks):
                    rows = slice(r * chunk, (r + 1) * chunk)
                    x = tiles[jj][rows, :]
                    if jj == diag:
                        x = jnp.where(rel0 <= -r * chunk, x, DA_MASKED)
                        tiles[jj][rows, :] = x
                    col_max = x if col_max is None else jnp.maximum(col_max, x)
                m_new = jnp.maximum(m_new, jnp.max(col_max, axis=0, keepdims=True) + offs[jj])
            pts = []
            for jj in range(n_kv):
                shift = m_new - offs[jj]
                for r in range(n_chunks):
                    x = tiles[jj][r * chunk:(r + 1) * chunk, :]
                    pts.append(jnp.exp2(x - shift).astype(BF16))
            vt = vt_ref[p] if n_kv == 2 else vt_ref[p, :, 0:tile]
            pv = jnp.dot(vt, jnp.concatenate(pts, axis=0), preferred_element_type=F32)
            acc = jnp.exp2(m_old - m_new) * acc_refs[c][qi] + pv
            if diag is None:
                acc_refs[c][qi] = acc
                m_refs[c][qi] = m_new
            else:
                accs.append(acc)
        if diag is not None:
            a0, a1 = accs
            ot = a0[0:w] / a0[w:w + 1] - lam * (a1[0:w] / a1[w:w + 1])
            o = ot.T
            o = o * lax.rsqrt(jnp.mean(o * o, axis=1, keepdims=True) + EPS) * g_ref[...]
            q_rows = pl.ds(pl.multiple_of(qi * tile, tile), tile)
            o_ref[q_rows, :] = (o * (1.0 - lambda_init)).astype(o_ref.dtype)

    def two_entries(i, base, last):
        t = base + 2 * i
        issue_scores(t + 1, 1, 2)
        consume(t, 0, 0 if last else None)
        issue_scores(t + 2, 0, 1 if last else 2)
        consume(t + 1, 1, 1 if last else None)
        return 0

    issue_scores(0, 0, 2 if n_full else 1)
    lax.fori_loop(0, n_full // 2, lambda i, _: two_entries(i, 0, False), 0)
    lax.fori_loop(0, n_blk // 2, lambda i, _: two_entries(i, n_full, True), 0)


def _diff_attention(proj, lam_vecs, norm_g, *, batch, seq, d, lambda_init, tile):
    dk = d // (2 * DA_HEADS)
    w = 2 * dk
    tile = min(tile, seq)
    assert tile <= 256 and seq % (2 * tile) == 0
    n_blk = seq // tile
    hpb = d // w
    slopes = 2.0 ** (-8.0 * (jnp.arange(DA_HEADS, dtype=F32) + 1.0) / DA_HEADS)
    n_full, sched_q, sched_p = _da_schedule(n_blk)
    kern = functools.partial(_da_kernel, tile=tile, dk=dk, n_blk=n_blk, n_full=n_full,
                             lambda_init=lambda_init)
    smem = pl.BlockSpec(memory_space=pltpu.SMEM)
    return pl.pallas_call(
        kern,
        grid=(batch, DA_HEADS),
        in_specs=[smem, smem, smem,
                  pl.BlockSpec((4, dk), lambda b, h: (0, 0)),
                  pl.BlockSpec((1, w), lambda b, h: (0, 0)),
                  pl.BlockSpec((seq, w), lambda b, h: (b, h)),
                  pl.BlockSpec((seq, w), lambda b, h: (b, hpb + h)),
                  pl.BlockSpec((seq, w), lambda b, h: (b, 2 * hpb + h))],
        out_specs=pl.BlockSpec((seq, w), lambda b, h: (b, h)),
        out_shape=jax.ShapeDtypeStruct((batch * seq, d), BF16),
        scratch_shapes=[pltpu.VMEM((seq, w), BF16), pltpu.VMEM((seq, w), BF16),
                        pltpu.VMEM((seq, w), BF16), pltpu.VMEM((seq, w), BF16),
                        pltpu.VMEM((n_blk // 2, w + DA_ONES_ROWS, 2 * tile), BF16),
                        pltpu.VMEM((2, 4, tile, tile), F32),
                        pltpu.VMEM((n_blk, w + DA_ONES_ROWS, tile), F32),
                        pltpu.VMEM((n_blk, w + DA_ONES_ROWS, tile), F32),
                        pltpu.VMEM((n_blk, 1, tile), F32),
                        pltpu.VMEM((n_blk, 1, tile), F32)],
        compiler_params=_params(2),
        name="diff_attention",
    )(jnp.asarray(sched_q), jnp.asarray(sched_p), slopes, lam_vecs, norm_g.reshape(1, w),
      proj, proj, proj)


def _ret_kernel(cd_ref, intra_ref, qd_ref, kd_ref, gng_ref, gnb_ref,
                q_ref, k_ref, v_ref, g_ref, o_ref, *, n_chunks, dk, dv):
    h = pl.program_id(1)
    cd = cd_ref[h]
    intra = intra_ref[...]
    qd = qd_ref[...]
    kd = kd_ref[...]
    gng = gng_ref[...]
    gnb = gnb_ref[...]
    c_len = RET_CHUNK

    def chunk(c, state):
        start = pl.multiple_of(c * c_len, c_len)
        q = q_ref[pl.ds(start, c_len), :]
        k = k_ref[pl.ds(start, c_len), :]
        v = v_ref[pl.ds(start, c_len), :]
        s = lax.dot_general(q, k, _NT, preferred_element_type=F32) * intra
        o = (jnp.dot(s.astype(BF16), v, preferred_element_type=F32)
             + jnp.dot(q, state.astype(BF16), preferred_element_type=F32) * qd)
        kt = (k.astype(F32) * kd).T.astype(BF16)
        new_state = state * cd + jnp.dot(kt, v, preferred_element_type=F32)
        mu = jnp.mean(o, axis=1, keepdims=True)
        oc = o - mu
        var = jnp.mean(oc * oc, axis=1, keepdims=True)
        y = oc * lax.rsqrt(var + EPS) * gng + gnb
        gate = g_ref[pl.ds(start, c_len), :].astype(F32)
        y = gate * (1.0 / (1.0 + jnp.exp(-gate))) * y
        o_ref[pl.ds(start, c_len), :] = y.astype(o_ref.dtype)
        return new_state

    lax.fori_loop(0, n_chunks, chunk, jnp.zeros((dk, dv), F32))


def _retention(proj, gn_g, gn_b, *, batch, seq, d):
    dv = d // RET_HEADS
    dk = dv // 2
    c_len = RET_CHUNK
    n_chunks = seq // c_len
    log_g = jnp.log(1.0 - 2.0 ** (-5.0 - jnp.arange(RET_HEADS, dtype=F32)))
    idx = jnp.arange(c_len, dtype=F32)
    rel = idx[:, None] - idx[None, :]
    intra = jnp.where(rel[None] >= 0, jnp.exp(rel[None] * log_g[:, None, None]), 0.0) * dk ** -0.5
    q_decay = jnp.exp((idx[None, :] + 1.0) * log_g[:, None])[:, :, None]
    k_decay = (jnp.exp((c_len - 1.0 - idx[None, :]) * log_g[:, None]) * dk ** -0.5)[:, :, None]
    chunk_decay = jnp.exp(c_len * log_g)
    q_blk = 3 * d // dk
    k_blk = q_blk + RET_HEADS
    v_blk = (3 * d + d) // dv
    g_blk = v_blk + RET_HEADS
    kern = functools.partial(_ret_kernel, n_chunks=n_chunks, dk=dk, dv=dv)
    return pl.pallas_call(
        kern,
        grid=(batch, RET_HEADS),
        in_specs=[pl.BlockSpec(memory_space=pltpu.SMEM),
                  pl.BlockSpec((None, c_len, c_len), lambda b, h: (h, 0, 0)),
                  pl.BlockSpec((None, c_len, 1), lambda b, h: (h, 0, 0)),
                  pl.BlockSpec((None, c_len, 1), lambda b, h: (h, 0, 0)),
                  pl.BlockSpec((1, dv), lambda b, h: (0, h)),
                  pl.BlockSpec((1, dv), lambda b, h: (0, h)),
                  pl.BlockSpec((seq, dk), lambda b, h: (b, q_blk + h)),
                  pl.BlockSpec((seq, dk), lambda b, h: (b, k_blk + h)),
                  pl.BlockSpec((seq, dv), lambda b, h: (b, v_blk + h)),
                  pl.BlockSpec((seq, dv), lambda b, h: (b, g_blk + h))],
        out_specs=pl.BlockSpec((seq, dv), lambda b, h: (b, h)),
        out_shape=jax.ShapeDtypeStruct((batch * seq, d), BF16),
        compiler_params=_params(2),
        name="retention",
    )(chunk_decay, intra, q_decay, k_decay, gn_g.reshape(1, d), gn_b.reshape(1, d),
      proj, proj, proj, proj)


def _mem_kernel(q_ref, kv_ref, o_ref, *, d, dh):
    for h in range(MEM_HEADS):
        q = q_ref[:, h * dh:(h + 1) * dh]
        k = kv_ref[:, h * dh:(h + 1) * dh]
        v = kv_ref[:, d + h * dh:d + (h + 1) * dh]
        s = lax.dot_general(q, k, _NT, preferred_element_type=F32) * dh ** -0.5
        p = jnp.exp(s - jnp.max(s, axis=1, keepdims=True))
        l = jnp.sum(p, axis=1, keepdims=True)
        o = jnp.dot(p.astype(BF16), v, preferred_element_type=F32) / l
        o_ref[:, h * dh:(h + 1) * dh] = o.astype(o_ref.dtype)


def _memory_attention(proj, mem_kv, *, batch, seq, mem_len, d, tq):
    dh = d // MEM_HEADS
    tq = min(tq, seq)
    nq = seq // tq
    q_blk = 6
    kern = functools.partial(_mem_kernel, d=d, dh=dh)
    return pl.pallas_call(
        kern,
        grid=(batch, nq),
        in_specs=[pl.BlockSpec((tq, d), lambda b, i: (b * nq + i, q_blk)),
                  pl.BlockSpec((mem_len, 2 * d), lambda b, i: (b, 0))],
        out_specs=pl.BlockSpec((tq, d), lambda b, i: (b * nq + i, 0)),
        out_shape=jax.ShapeDtypeStruct((batch * seq, d), BF16),
        compiler_params=_params(2),
        name="memory_attention",
    )(proj, mem_kv)


def _merge_kernel(x_ref, yda_ref, yret_ref, ymem_ref, g0_ref, g1_ref, g2_ref, bg_ref,
                  wb_ref, wo_ref, lng_ref, lnb_ref, wr_ref, br_ref, x1_ref, lg_ref, *, alpha):
    ys = (yda_ref, yret_ref, ymem_ref)
    gs = (g0_ref, g1_ref, g2_ref)
    merged = None
    for n in range(N_BRANCH):
        gate = gs[n][...].astype(F32) + bg_ref[n:n + 1, :]
        gate = 1.0 / (1.0 + jnp.exp(-gate))
        term = gate * jnp.dot(ys[n][...], wb_ref[n], preferred_element_type=F32)
        merged = term if merged is None else merged + term
    out = jnp.dot(merged.astype(BF16), wo_ref[...], preferred_element_type=F32)
    x1 = _layer_norm(alpha * x_ref[...] + out, lng_ref[...], lnb_ref[...])
    x1_ref[...] = x1
    lg_ref[...] = jnp.dot(x1, wr_ref[...], preferred_element_type=F32,
                          precision=lax.Precision.HIGHEST) + br_ref[...]


def _merge(x2, y_da, y_ret, y_mem, proj, b_gate, wb, wo, ln_g, ln_b, wr, br, *, alpha, tm):
    t, d = x2.shape
    tm = min(tm, t)
    gate_blk = 7
    row = lambda i: (i, 0)
    const2 = lambda i: (0, 0)
    kern = functools.partial(_merge_kernel, alpha=alpha)
    return pl.pallas_call(
        kern,
        grid=(t // tm,),
        in_specs=[pl.BlockSpec((tm, d), row), pl.BlockSpec((tm, d), row),
                  pl.BlockSpec((tm, d), row), pl.BlockSpec((tm, d), row),
                  pl.BlockSpec((tm, d), lambda i: (i, gate_blk)),
                  pl.BlockSpec((tm, d), lambda i: (i, gate_blk + 1)),
                  pl.BlockSpec((tm, d), lambda i: (i, gate_blk + 2)),
                  pl.BlockSpec((N_BRANCH, d), const2),
                  pl.BlockSpec((N_BRANCH, d, d), lambda i: (0, 0, 0)),
                  pl.BlockSpec((d, d), const2),
                  pl.BlockSpec((1, d), const2), pl.BlockSpec((1, d), const2),
                  pl.BlockSpec((d, LANES), const2), pl.BlockSpec((1, LANES), const2)],
        out_specs=[pl.BlockSpec((tm, d), row), pl.BlockSpec((tm, LANES), row)],
        out_shape=[jax.ShapeDtypeStruct((t, d), F32), jax.ShapeDtypeStruct((t, LANES), F32)],
        compiler_params=_params(1),
        name="merge_ln1_router",
    )(x2, y_da, y_ret, y_mem, proj, proj, proj, b_gate, wb, wo, ln_g, ln_b, wr, br)


PAIRS = ((0, 1), (0, 2), (0, 3), (1, 2), (1, 3), (2, 3))
N_CLASSES = N_GROUPS * len(PAIRS)
ROUTE_CLS, ROUTE_RANK, ROUTE_WA, ROUTE_WB = 0, 1, 2, 3


def _route_kernel(lg_ref, tril_ref, route_ref, cnt_ref, carry_ref):
    i = pl.program_id(0)

    @pl.when(i == 0)
    def _():
        carry_ref[...] = jnp.zeros_like(carry_ref)

    lg = lg_ref[...]
    tm = lg.shape[0]
    col = lax.broadcasted_iota(jnp.int32, lg.shape, 1).astype(F32)
    big = float(LANES)
    ninf = -jnp.inf

    def first_argmax(vals, vmax):
        return jnp.min(jnp.where(vals == vmax, col, big), axis=1, keepdims=True)

    gl = jnp.where(col < N_GROUPS, lg, ninf)
    gmax = jnp.max(gl, axis=1, keepdims=True)
    p_top = 1.0 / jnp.sum(jnp.exp(gl - gmax), axis=1, keepdims=True)
    g_idx = first_argmax(gl, gmax)
    lo = N_GROUPS + g_idx * EXPERTS_PER_GROUP
    el = jnp.where((col >= lo) & (col < lo + EXPERTS_PER_GROUP), lg, ninf)
    e1 = jnp.max(el, axis=1, keepdims=True)
    i1 = first_argmax(el, e1)
    el2 = jnp.where(col == i1, ninf, el)
    e2 = jnp.max(el2, axis=1, keepdims=True)
    i2 = first_argmax(el2, e2)
    r = jnp.exp(e2 - e1)
    w1 = p_top / (1.0 + r)
    w2 = p_top * r / (1.0 + r)

    a_loc = jnp.minimum(i1, i2) - lo
    b_loc = jnp.maximum(i1, i2) - lo
    pair = jnp.where(a_loc == 0.0, b_loc - 1.0, jnp.where(a_loc == 1.0, b_loc + 1.0, 5.0))
    cls = g_idx * float(len(PAIRS)) + pair
    top_is_a = i1 < i2
    w_a = jnp.where(top_is_a, w1, w2)
    w_b = jnp.where(top_is_a, w2, w1)

    onehot = jnp.where(col == cls, 1.0, 0.0)
    prefix = carry_ref[...] + jnp.dot(tril_ref[...], onehot.astype(BF16), preferred_element_type=F32)
    rank = jnp.sum(onehot * prefix, axis=1, keepdims=True) - 1.0
    carry_ref[...] = prefix[tm - 1:tm, :]
    cnt_ref[...] = prefix[tm - 1:tm, :]
    route_ref[...] = jnp.where(col == ROUTE_CLS, cls,
                               jnp.where(col == ROUTE_RANK, rank,
                                         jnp.where(col == ROUTE_WA, w_a,
                                                   jnp.where(col == ROUTE_WB, w_b, 0.0))))


def _route(logits, *, tm):
    t = logits.shape[0]
    tm = min(tm, t)
    tril = jnp.tril(jnp.ones((tm, tm), BF16))
    return pl.pallas_call(
        _route_kernel,
        grid=(t // tm,),
        in_specs=[pl.BlockSpec((tm, LANES), lambda i: (i, 0)),
                  pl.BlockSpec((tm, tm), lambda i: (0, 0))],
        out_specs=[pl.BlockSpec((tm, LANES), lambda i: (i, 0)),
                   pl.BlockSpec((1, LANES), lambda i: (0, 0))],
        out_shape=[jax.ShapeDtypeStruct((t, LANES), F32), jax.ShapeDtypeStruct((1, LANES), F32)],
        scratch_shapes=[pltpu.VMEM((1, LANES), F32)],
        compiler_params=_params(1),
        name="routing",
    )(logits, tril)


def _dispatch_plan(route, counts, *, tm):
    t = route.shape[0]
    n_tiles = t // tm + N_CLASSES
    cnt = counts[0, :N_CLASSES].astype(jnp.int32)
    padded = (cnt + tm - 1) // tm * tm
    ends = jnp.cumsum(padded)
    starts = ends - padded
    cls = route[:, ROUTE_CLS].astype(jnp.int32)
    pos = starts[cls] + route[:, ROUTE_RANK].astype(jnp.int32)
    n_used = ends[-1] // tm
    tile = jnp.arange(n_tiles, dtype=jnp.int32)
    used = (tile < n_used).astype(jnp.int32)
    first_row = jnp.minimum(tile, n_used - 1) * tm
    tile_cls = jnp.sum((ends[None, :] <= first_row[:, None]).astype(jnp.int32), axis=1)
    tile_cls = jnp.minimum(tile_cls, N_CLASSES - 1)
    pairs = jnp.asarray(PAIRS, jnp.int32)
    group = tile_cls // len(PAIRS)
    e_a = group * EXPERTS_PER_GROUP + pairs[tile_cls % len(PAIRS), 0]
    e_b = group * EXPERTS_PER_GROUP + pairs[tile_cls % len(PAIRS), 1]
    return pos, n_tiles, used, e_a, e_b


def _invert_kernel(pos_ref, src_ref, *, n_tokens, n_rows):
    def clear(r, _):
        src_ref[r] = 0
        return 0

    def put(t, _):
        src_ref[pos_ref[t]] = t
        return 0

    lax.fori_loop(0, n_rows, clear, 0, unroll=8)
    lax.fori_loop(0, n_tokens, put, 0, unroll=8)


def _invert(pos, *, n_rows):
    n_tokens = pos.shape[0]
    smem = pl.BlockSpec(memory_space=pltpu.SMEM)
    return pl.pallas_call(
        functools.partial(_invert_kernel, n_tokens=n_tokens, n_rows=n_rows),
        in_specs=[smem], out_specs=smem,
        out_shape=jax.ShapeDtypeStruct((n_rows,), jnp.int32),
        name="invert_permutation",
    )(pos)


def _gather_copy(tab_ref, buf_ref, sem, row, r):
    return pltpu.make_async_copy(tab_ref.at[pl.ds(row, 1), :], buf_ref.at[pl.ds(r, 1), :], sem)


def _gather_start(idx_ref, tab_ref, buf_ref, sem, tile, tm):
    def issue(r, _):
        _gather_copy(tab_ref, buf_ref, sem, idx_ref[tile * tm + r], r).start()
        return 0

    lax.fori_loop(0, tm, issue, 0, unroll=8)


def _gather_wait(tab_ref, buf_ref, sem, tm):
    pltpu.make_async_copy(tab_ref.at[pl.ds(0, tm), :], buf_ref, sem).wait()


def _gather_pipeline(idx_ref, tab_ref, buf_ref, sem_ref, tm, n_tiles):
    i = pl.program_id(0)
    slot = i & 1

    @pl.when(i == 0)
    def _():
        _gather_start(idx_ref, tab_ref, buf_ref.at[0], sem_ref.at[0], 0, tm)

    @pl.when(i + 1 < n_tiles)
    def _():
        _gather_start(idx_ref, tab_ref, buf_ref.at[1 - slot], sem_ref.at[1 - slot], i + 1, tm)

    _gather_wait(tab_ref, buf_ref.at[slot], sem_ref.at[slot], tm)
    return slot


def _dispatch_kernel(src_ref, x_ref, o_ref, buf_ref, sem_ref, *, tm, n_tiles):
    slot = _gather_pipeline(src_ref, x_ref, buf_ref, sem_ref, tm, n_tiles)
    o_ref[...] = buf_ref[slot].astype(o_ref.dtype)


def _dispatch(x1, src, *, tm, n_tiles):
    d = x1.shape[1]
    return pl.pallas_call(
        functools.partial(_dispatch_kernel, tm=tm, n_tiles=n_tiles),
        grid_spec=pltpu.PrefetchScalarGridSpec(
            num_scalar_prefetch=1, grid=(n_tiles,),
            in_specs=[pl.BlockSpec(memory_space=pl.ANY)],
            out_specs=pl.BlockSpec((tm, d), lambda i, src: (i, 0)),
            scratch_shapes=[pltpu.VMEM((2, tm, d), F32), pltpu.SemaphoreType.DMA((2,))]),
        out_shape=jax.ShapeDtypeStruct((n_tiles * tm, d), BF16),
        compiler_params=_params(1),
        name="moe_dispatch",
    )(src, x1)


def _ffn_kernel(ea_ref, eb_ref, used_ref, xs_ref, ws_ref, w1a_ref, w3a_ref, w2a_ref,
                w1b_ref, w3b_ref, w2b_ref, o_ref):
    i = pl.program_id(0)

    @pl.when(used_ref[i] != 0)
    def _():
        x = xs_ref[...]
        ws = ws_ref[...]

        def expert(w1_ref, w3_ref, w2_ref):
            a = jnp.dot(x, w1_ref[...], preferred_element_type=F32)
            b = jnp.dot(x, w3_ref[...], preferred_element_type=F32)
            hid = a * (1.0 / (1.0 + jnp.exp(-a))) * b
            return jnp.dot(hid.astype(BF16), w2_ref[...], preferred_element_type=F32)

        o_ref[...] = (ws[:, 0:1] * expert(w1a_ref, w3a_ref, w2a_ref)
                      + ws[:, 1:2] * expert(w1b_ref, w3b_ref, w2b_ref))

    @pl.when(used_ref[i] == 0)
    def _():
        o_ref[...] = jnp.zeros_like(o_ref)


def _ffn(xs, ws, e_a, e_b, used, w1, w3, w2, *, tm, n_tiles):
    d = xs.shape[1]
    f = w1.shape[-1]
    up_a = pl.BlockSpec((None, d, f), lambda i, ea, eb, used: (ea[i], 0, 0))
    up_b = pl.BlockSpec((None, d, f), lambda i, ea, eb, used: (eb[i], 0, 0))
    down_a = pl.BlockSpec((None, f, d), lambda i, ea, eb, used: (ea[i], 0, 0))
    down_b = pl.BlockSpec((None, f, d), lambda i, ea, eb, used: (eb[i], 0, 0))
    return pl.pallas_call(
        _ffn_kernel,
        grid_spec=pltpu.PrefetchScalarGridSpec(
            num_scalar_prefetch=3, grid=(n_tiles,),
            in_specs=[pl.BlockSpec((tm, d), lambda i, ea, eb, used: (i, 0)),
                      pl.BlockSpec((tm, 2), lambda i, ea, eb, used: (i, 0)),
                      up_a, up_a, down_a, up_b, up_b, down_b],
            out_specs=pl.BlockSpec((tm, d), lambda i, ea, eb, used: (i, 0))),
        out_shape=jax.ShapeDtypeStruct((n_tiles * tm, d), F32),
        compiler_params=_params(1),
        name="experts",
    )(e_a, e_b, used, xs, ws, w1, w3, w2, w1, w3, w2)


def _combine_kernel(pos_ref, x_ref, lng_ref, lnb_ref, ys_ref, o_ref, buf_ref, sem_ref,
                    *, tm, n_tiles, alpha):
    slot = _gather_pipeline(pos_ref, ys_ref, buf_ref, sem_ref, tm, n_tiles)
    o_ref[...] = _layer_norm(alpha * x_ref[...] + buf_ref[slot], lng_ref[...], lnb_ref[...])


def _combine(x1, ys, pos, ln_g, ln_b, *, alpha, tm):
    t, d = x1.shape
    n_tiles = t // tm
    return pl.pallas_call(
        functools.partial(_combine_kernel, tm=tm, n_tiles=n_tiles, alpha=alpha),
        grid_spec=pltpu.PrefetchScalarGridSpec(
            num_scalar_prefetch=1, grid=(n_tiles,),
            in_specs=[pl.BlockSpec((tm, d), lambda i, pos: (i, 0)),
                      pl.BlockSpec((1, d), lambda i, pos: (0, 0)),
                      pl.BlockSpec((1, d), lambda i, pos: (0, 0)),
                      pl.BlockSpec(memory_space=pl.ANY)],
            out_specs=pl.BlockSpec((tm, d), lambda i, pos: (i, 0)),
            scratch_shapes=[pltpu.VMEM((2, tm, d), F32), pltpu.SemaphoreType.DMA((2,))]),
        out_shape=jax.ShapeDtypeStruct((t, d), F32),
        compiler_params=_params(1),
        name="moe_combine_ln2",
    )(pos, x1, ln_g, ln_b, ys)


def _moe(x1, logits, w1, w3, w2, ln_g, ln_b, *, alpha, tm):
    tm = min(tm, x1.shape[0])
    route, counts = _route(logits, tm=tm)
    pos, n_tiles, used, e_a, e_b = _dispatch_plan(route, counts, tm=tm)
    src = _invert(pos, n_rows=n_tiles * tm)
    ws = route[src, ROUTE_WA:ROUTE_WB + 1]
    xs = _dispatch(x1, src, tm=tm, n_tiles=n_tiles)
    ys = _ffn(xs, ws, e_a, e_b, used, w1, w3, w2, tm=tm, n_tiles=n_tiles)
    return _combine(x1, ys, pos, ln_g, ln_b, alpha=alpha, tm=tm)


def kernel(x, mem, w_in, b_gate, w_mem_kv, da_lambda, da_norm_g, ret_gn_g, ret_gn_b, w_branch, w_o,
           ln1_g, ln1_b, w_rg, b_rg, w_re, b_re, w1, w3, w2, ln2_g, ln2_b):
    batch, seq, d = x.shape
    mem_len = mem.shape[1]
    depth = w_in.shape[0]
    alpha = (2.0 * depth) ** 0.25
    t = batch * seq
    x2 = x.reshape(t, d)
    for l in range(depth):
        lambda_init = 0.8 - 0.6 * math.exp(-0.3 * l)
        proj = _matmul(x2, w_in[l].astype(BF16), tm=1024, tn=1024, name="in_proj")
        mem_kv = _matmul(mem.reshape(batch * mem_len, d), w_mem_kv[l].astype(BF16),
                         tm=1024, tn=1024, name="mem_kv_proj")
        y_da = _diff_attention(proj, da_lambda[l], da_norm_g[l], batch=batch, seq=seq, d=d,
                               lambda_init=lambda_init, tile=256)
        y_ret = _retention(proj, ret_gn_g[l], ret_gn_b[l], batch=batch, seq=seq, d=d)
        y_mem = _memory_attention(proj, mem_kv, batch=batch, seq=seq, mem_len=mem_len, d=d, tq=512)
        n_route = N_GROUPS + N_EXPERTS
        wr = jnp.pad(jnp.concatenate([w_rg[l], w_re[l]], axis=1), ((0, 0), (0, LANES - n_route)))
        br = jnp.pad(jnp.concatenate([b_rg[l], b_re[l]]), (0, LANES - n_route)).reshape(1, LANES)
        x1, logits = _merge(x2, y_da, y_ret, y_mem, proj, b_gate[l].reshape(N_BRANCH, d),
                            w_branch[l].astype(BF16), w_o[l].astype(BF16),
                            ln1_g[l].reshape(1, d), ln1_b[l].reshape(1, d), wr, br,
                            alpha=alpha, tm=256)
        x2 = _moe(x1, logits, w1[l].astype(BF16), w3[l].astype(BF16), w2[l].astype(BF16),
                  ln2_g[l].reshape(1, d), ln2_b[l].reshape(1, d), alpha=alpha, tm=256)
    return x2.reshape(batch, seq, d)
```

```python
import functools
import math

import jax
import jax.numpy as jnp
import numpy as np
from jax import lax
from jax.experimental import pallas as pl
from jax.experimental.pallas import tpu as pltpu

BF16 = jnp.bfloat16
F32 = jnp.float32

DA_HEADS = 8
RET_HEADS = 4
RET_CHUNK = 128
MEM_HEADS = 4
N_BRANCH = 3
N_GROUPS = 4
EXPERTS_PER_GROUP = 4
N_EXPERTS = N_GROUPS * EXPERTS_PER_GROUP
EPS = 1e-5
LANES = 128
VMEM_LIMIT = 56 * 1024 * 1024

_NT = (((1,), (1,)), ((), ()))


def _params(n_axes):
    return pltpu.CompilerParams(dimension_semantics=("arbitrary",) * n_axes,
                                vmem_limit_bytes=VMEM_LIMIT)


def _layer_norm(z, g, b):
    mu = jnp.mean(z, axis=-1, keepdims=True)
    zc = z - mu
    var = jnp.mean(zc * zc, axis=-1, keepdims=True)
    return zc * lax.rsqrt(var + EPS) * g + b


def _matmul_kernel(x_ref, w_ref, o_ref):
    o_ref[...] = jnp.dot(x_ref[...].astype(BF16), w_ref[...],
                         preferred_element_type=F32).astype(o_ref.dtype)


def _matmul(x, w, *, tm, tn, name):
    m, k = x.shape
    n = w.shape[1]
    tm, tn = min(tm, m), min(tn, n)
    return pl.pallas_call(
        _matmul_kernel,
        grid=(m // tm, n // tn),
        in_specs=[pl.BlockSpec((tm, k), lambda i, j: (i, 0)),
                  pl.BlockSpec((k, tn), lambda i, j: (0, j))],
        out_specs=pl.BlockSpec((tm, tn), lambda i, j: (i, j)),
        out_shape=jax.ShapeDtypeStruct((m, n), BF16),
        compiler_params=_params(2),
        name=name,
    )(x, w)


DA_ONES_ROWS = 16
DA_MASKED = -1e30
DA_CHUNK_ROWS = 32
LOG2_E = 1.4426950408889634


def _da_schedule(n_blk):
    full = [(qi, p) for qi in range(n_blk) for p in range(qi // 2)]
    order = full + [(qi, qi // 2) for qi in range(n_blk)]
    order.append(order[-1])
    table = np.asarray(order, np.int32)
    return len(full), table[:, 0], table[:, 1]


def _da_kernel(sq_ref, sp_ref, slope_ref, lam_ref, g_ref, q_ref, k_ref, v_ref, o_ref,
               ka0_ref, ka1_ref, qa0_ref, qa1_ref, vt_ref, s_ref, acc0_ref, acc1_ref, m0_ref, m1_ref,
               *, tile, dk, n_blk, n_full, lambda_init):
    h = pl.program_id(1)
    a_log2 = slope_ref[h] * LOG2_E
    w = 2 * dk
    ka_refs = (ka0_ref, ka1_ref)
    qa_refs = (qa0_ref, qa1_ref)
    acc_refs = (acc0_ref, acc1_ref)
    m_refs = (m0_ref, m1_ref)
    lane = lax.broadcasted_iota(jnp.int32, (tile, w), 1)
    pos = lax.broadcasted_iota(jnp.int32, (tile, w), 0).astype(F32)
    pieces = []
    rest = jnp.full((tile, w), a_log2, F32)
    for _ in range(3):
        piece = rest.astype(BF16).astype(F32)
        pieces.append(piece)
        rest = rest - piece
    in_comp, k_bias, q_bias = [], [], []
    for c in range(2):
        free = dk * (1 - c)
        in_comp.append((lane >= c * dk) & (lane < (c + 1) * dk))
        kb = jnp.where((lane >= free) & (lane < free + 3), pos, 0.0)
        qb = jnp.where((lane >= free + 3) & (lane < free + 6), -pos, 0.0)
        for i in range(3):
            kb = jnp.where(lane == free + 3 + i, pieces[i], kb)
            qb = jnp.where(lane == free + i, pieces[i], qb)
        k_bias.append(kb)
        q_bias.append(qb)

    def prepare(pp, _):
        for jj in range(2):
            blk = 2 * pp + jj
            rows = pl.ds(pl.multiple_of(blk * tile, tile), tile)
            kf = k_ref[rows, :].astype(F32)
            qf = q_ref[rows, :].astype(F32) * (dk ** -0.5 * LOG2_E)
            for c in range(2):
                ka_refs[c][rows, :] = jnp.where(in_comp[c], kf, k_bias[c]).astype(BF16)
                qa_refs[c][rows, :] = jnp.where(in_comp[c], qf, q_bias[c]).astype(BF16)
                acc_refs[c][blk] = jnp.zeros((w + DA_ONES_ROWS, tile), F32)
                m_refs[c][blk] = jnp.full((1, tile), -jnp.inf, F32)
            vt_ref[pp, 0:w, jj * tile:(jj + 1) * tile] = v_ref[rows, :].astype(F32).T.astype(BF16)
        vt_ref[pp, w:w + DA_ONES_ROWS, :] = jnp.ones((DA_ONES_ROWS, 2 * tile), BF16)
        return 0

    lax.fori_loop(0, n_blk // 2, prepare, 0)

    lv = lam_ref[...]
    lam = (jnp.exp(jnp.sum(lv[0:1] * lv[1:2], axis=1, keepdims=True))
           - jnp.exp(jnp.sum(lv[2:3] * lv[3:4], axis=1, keepdims=True)) + lambda_init)
    chunk = DA_CHUNK_ROWS
    n_chunks = tile // chunk
    rel0 = (lax.broadcasted_iota(jnp.int32, (chunk, tile), 0)
            - lax.broadcasted_iota(jnp.int32, (chunk, tile), 1))

    def issue_scores(t, slot, n_kv):
        qi, p = sq_ref[t], sp_ref[t]
        q_rows = pl.ds(pl.multiple_of(qi * tile, tile), tile)
        for jj in range(n_kv):
            k_rows = pl.ds(pl.multiple_of((2 * p + jj) * tile, tile), tile)
            for c in range(2):
                s_ref[slot, 2 * jj + c] = lax.dot_general(
                    ka_refs[c][k_rows, :], qa_refs[c][q_rows, :], _NT, preferred_element_type=F32)

    def consume(t, slot, diag):
        n_kv = 1 if diag == 0 else 2
        qi, p = sq_ref[t], sp_ref[t]
        offs = [a_log2 * ((2 * p + jj - qi) * tile).astype(F32) for jj in range(n_kv)]
        accs = []
        for c in range(2):
            tiles = [s_ref.at[slot, 2 * jj + c] for jj in range(n_kv)]
            m_old = m_refs[c][qi]
            m_new = m_old
            for jj in range(n_kv):
                col_max = None
                for r in range(n_chunks):
                    rows = slice(r * chunk, (r + 1) * chunk)
                    x = tiles[jj][rows, :]
                    if jj == diag:
                        x = jnp.where(rel0 <= -r * chunk, x, DA_MASKED)
                        tiles[jj][rows, :] = x
                    col_max = x if col_max is None else jnp.maximum(col_max, x)
                m_new = jnp.maximum(m_new, jnp.max(col_max, axis=0, keepdims=True) + offs[jj])
            pts = []
            for jj in range(n_kv):
                shift = m_new - offs[jj]
                for r in range(n_chunks):
                    x = tiles[jj][r * chunk:(r + 1) * chunk, :]
                    pts.append(jnp.exp2(x - shift).astype(BF16))
            vt = vt_ref[p] if n_kv == 2 else vt_ref[p, :, 0:tile]
            pv = jnp.dot(vt, jnp.concatenate(pts, axis=0), preferred_element_type=F32)
            acc = jnp.exp2(m_old - m_new) * acc_refs[c][qi] + pv
            if diag is None:
                acc_refs[c][qi] = acc
                m_refs[c][qi] = m_new
            else:
                accs.append(acc)
        if diag is not None:
            a0, a1 = accs
            ot = a0[0:w] / a0[w:w + 1] - lam * (a1[0:w] / a1[w:w + 1])
            o = ot.T
            o = o * lax.rsqrt(jnp.mean(o * o, axis=1, keepdims=True) + EPS) * g_ref[...]
            q_rows = pl.ds(pl.multiple_of(qi * tile, tile), tile)
            o_ref[q_rows, :] = (o * (1.0 - lambda_init)).astype(o_ref.dtype)

    def two_entries(i, base, last):
        t = base + 2 * i
        issue_scores(t + 1, 1, 2)
        consume(t, 0, 0 if last else None)
        issue_scores(t + 2, 0, 1 if last else 2)
        consume(t + 1, 1, 1 if last else None)
        return 0

    issue_scores(0, 0, 2 if n_full else 1)
    lax.fori_loop(0, n_full // 2, lambda i, _: two_entries(i, 0, False), 0)
    lax.fori_loop(0, n_blk // 2, lambda i, _: two_entries(i, n_full, True), 0)


def _diff_attention(proj, lam_vecs, norm_g, *, batch, seq, d, lambda_init, tile):
    dk = d // (2 * DA_HEADS)
    w = 2 * dk
    tile = min(tile, seq)
    assert tile <= 256 and seq % (2 * tile) == 0
    n_blk = seq // tile
    hpb = d // w
    slopes = 2.0 ** (-8.0 * (jnp.arange(DA_HEADS, dtype=F32) + 1.0) / DA_HEADS)
    n_full, sched_q, sched_p = _da_schedule(n_blk)
    kern = functools.partial(_da_kernel, tile=tile, dk=dk, n_blk=n_blk, n_full=n_full,
                             lambda_init=lambda_init)
    smem = pl.BlockSpec(memory_space=pltpu.SMEM)
    return pl.pallas_call(
        kern,
        grid=(batch, DA_HEADS),
        in_specs=[smem, smem, smem,
                  pl.BlockSpec((4, dk), lambda b, h: (0, 0)),
                  pl.BlockSpec((1, w), lambda b, h: (0, 0)),
                  pl.BlockSpec((seq, w), lambda b, h: (b, h)),
                  pl.BlockSpec((seq, w), lambda b, h: (b, hpb + h)),
                  pl.BlockSpec((seq, w), lambda b, h: (b, 2 * hpb + h))],
        out_specs=pl.BlockSpec((seq, w), lambda b, h: (b, h)),
        out_shape=jax.ShapeDtypeStruct((batch * seq, d), BF16),
        scratch_shapes=[pltpu.VMEM((seq, w), BF16), pltpu.VMEM((seq, w), BF16),
                        pltpu.VMEM((seq, w), BF16), pltpu.VMEM((seq, w), BF16),
                        pltpu.VMEM((n_blk // 2, w + DA_ONES_ROWS, 2 * tile), BF16),
                        pltpu.VMEM((2, 4, tile, tile), F32),
                        pltpu.VMEM((n_blk, w + DA_ONES_ROWS, tile), F32),
                        pltpu.VMEM((n_blk, w + DA_ONES_ROWS, tile), F32),
                        pltpu.VMEM((n_blk, 1, tile), F32),
                        pltpu.VMEM((n_blk, 1, tile), F32)],
        compiler_params=_params(2),
        name="diff_attention",
    )(jnp.asarray(sched_q), jnp.asarray(sched_p), slopes, lam_vecs, norm_g.reshape(1, w),
      proj, proj, proj)


def _ret_kernel(cd_ref, intra_ref, qd_ref, kd_ref, gng_ref, gnb_ref,
                q_ref, k_ref, v_ref, g_ref, o_ref, state_ref, *, n_chunks, dk, dv):
    c_len = RET_CHUNK
    heads = range(RET_HEADS)
    state_ref[...] = jnp.zeros_like(state_ref)

    def chunk(c, _):
        rows = pl.ds(pl.multiple_of(c * c_len, c_len), c_len)
        q = [q_ref[rows, h * dk:(h + 1) * dk] for h in heads]
        k = [k_ref[rows, h * dk:(h + 1) * dk] for h in heads]
        v = [v_ref[rows, h * dv:(h + 1) * dv] for h in heads]
        state = [state_ref[h] for h in heads]
        s = [lax.dot_general(q[h], k[h], _NT, preferred_element_type=F32) for h in heads]
        cross = [jnp.dot(q[h], state[h].astype(BF16), preferred_element_type=F32) for h in heads]
        kv = [jnp.dot((k[h].astype(F32) * kd_ref[h]).T.astype(BF16), v[h],
                      preferred_element_type=F32) for h in heads]
        o = [jnp.dot((s[h] * intra_ref[h]).astype(BF16), v[h], preferred_element_type=F32)
             + cross[h] * qd_ref[h] for h in heads]
        for h in heads:
            state_ref[h] = state[h] * cd_ref[h] + kv[h]
            cols = slice(h * dv, (h + 1) * dv)
            mu = jnp.mean(o[h], axis=1, keepdims=True)
            oc = o[h] - mu
            var = jnp.mean(oc * oc, axis=1, keepdims=True)
            y = oc * lax.rsqrt(var + EPS) * gng_ref[:, cols] + gnb_ref[:, cols]
            gate = g_ref[rows, cols].astype(F32)
            y = gate * (1.0 / (1.0 + jnp.exp(-gate))) * y
            o_ref[rows, cols] = y.astype(o_ref.dtype)
        return 0

    lax.fori_loop(0, n_chunks, chunk, 0)


def _retention(proj, gn_g, gn_b, *, batch, seq, d):
    dv = d // RET_HEADS
    dk = dv // 2
    c_len = RET_CHUNK
    n_chunks = seq // c_len
    log_g = jnp.log(1.0 - 2.0 ** (-5.0 - jnp.arange(RET_HEADS, dtype=F32)))
    idx = jnp.arange(c_len, dtype=F32)
    rel = idx[:, None] - idx[None, :]
    intra = jnp.where(rel[None] >= 0, jnp.exp(rel[None] * log_g[:, None, None]), 0.0) * dk ** -0.5
    q_decay = jnp.exp((idx[None, :] + 1.0) * log_g[:, None])[:, :, None]
    k_decay = (jnp.exp((c_len - 1.0 - idx[None, :]) * log_g[:, None]) * dk ** -0.5)[:, :, None]
    chunk_decay = jnp.exp(c_len * log_g)
    hk = RET_HEADS * dk
    q_blk = 3 * d // hk
    k_blk = q_blk + 1
    v_blk = (3 * d + 2 * hk) // d
    g_blk = v_blk + 1
    kern = functools.partial(_ret_kernel, n_chunks=n_chunks, dk=dk, dv=dv)
    const3 = lambda b: (0, 0, 0)
    return pl.pallas_call(
        kern,
        grid=(batch,),
        in_specs=[pl.BlockSpec(memory_space=pltpu.SMEM),
                  pl.BlockSpec((RET_HEADS, c_len, c_len), const3),
                  pl.BlockSpec((RET_HEADS, c_len, 1), const3),
                  pl.BlockSpec((RET_HEADS, c_len, 1), const3),
                  pl.BlockSpec((1, d), lambda b: (0, 0)),
                  pl.BlockSpec((1, d), lambda b: (0, 0)),
                  pl.BlockSpec((seq, hk), lambda b: (b, q_blk)),
                  pl.BlockSpec((seq, hk), lambda b: (b, k_blk)),
                  pl.BlockSpec((seq, d), lambda b: (b, v_blk)),
                  pl.BlockSpec((seq, d), lambda b: (b, g_blk))],
        out_specs=pl.BlockSpec((seq, d), lambda b: (b, 0)),
        out_shape=jax.ShapeDtypeStruct((batch * seq, d), BF16),
        scratch_shapes=[pltpu.VMEM((RET_HEADS, dk, dv), F32)],
        compiler_params=_params(1),
        name="retention",
    )(chunk_decay, intra, q_decay, k_decay, gn_g.reshape(1, d), gn_b.reshape(1, d),
      proj, proj, proj, proj)


def _mem_kernel(q_ref, kv_ref, o_ref, *, d, dh):
    for h in range(MEM_HEADS):
        q = q_ref[:, h * dh:(h + 1) * dh]
        k = kv_ref[:, h * dh:(h + 1) * dh]
        v = kv_ref[:, d + h * dh:d + (h + 1) * dh]
        s = lax.dot_general(q, k, _NT, preferred_element_type=F32) * dh ** -0.5
        p = jnp.exp(s - jnp.max(s, axis=1, keepdims=True))
        l = jnp.sum(p, axis=1, keepdims=True)
        o = jnp.dot(p.astype(BF16), v, preferred_element_type=F32) / l
        o_ref[:, h * dh:(h + 1) * dh] = o.astype(o_ref.dtype)


def _memory_attention(proj, mem_kv, *, batch, seq, mem_len, d, tq):
    dh = d // MEM_HEADS
    tq = min(tq, seq)
    nq = seq // tq
    q_blk = 6
    kern = functools.partial(_mem_kernel, d=d, dh=dh)
    return pl.pallas_call(
        kern,
        grid=(batch, nq),
        in_specs=[pl.BlockSpec((tq, d), lambda b, i: (b * nq + i, q_blk)),
                  pl.BlockSpec((mem_len, 2 * d), lambda b, i: (b, 0))],
        out_specs=pl.BlockSpec((tq, d), lambda b, i: (b * nq + i, 0)),
        out_shape=jax.ShapeDtypeStruct((batch * seq, d), BF16),
        compiler_params=_params(2),
        name="memory_attention",
    )(proj, mem_kv)


def _merge_kernel(x_ref, yda_ref, yret_ref, ymem_ref, g0_ref, g1_ref, g2_ref, bg_ref,
                  wb_ref, wo_ref, lng_ref, lnb_ref, wrh_ref, wrl_ref, br_ref, x1_ref, lg_ref,
                  *, alpha):
    ys = (yda_ref, yret_ref, ymem_ref)
    gs = (g0_ref, g1_ref, g2_ref)
    merged = None
    for n in range(N_BRANCH):
        gate = gs[n][...].astype(F32) + bg_ref[n:n + 1, :]
        gate = 1.0 / (1.0 + jnp.exp(-gate))
        term = gate * jnp.dot(ys[n][...], wb_ref[n], preferred_element_type=F32)
        merged = term if merged is None else merged + term
    out = jnp.dot(merged.astype(BF16), wo_ref[...], preferred_element_type=F32)
    x1 = _layer_norm(alpha * x_ref[...] + out, lng_ref[...], lnb_ref[...])
    x1_ref[...] = x1
    x_hi = x1.astype(BF16)
    x_lo = (x1 - x_hi.astype(F32)).astype(BF16)
    lg_ref[...] = (jnp.dot(x_hi, wrh_ref[...], preferred_element_type=F32)
                   + jnp.dot(x_lo, wrh_ref[...], preferred_element_type=F32)
                   + jnp.dot(x_hi, wrl_ref[...], preferred_element_type=F32) + br_ref[...])


def _merge(x2, y_da, y_ret, y_mem, proj, b_gate, wb, wo, ln_g, ln_b, wr, br, *, alpha, tm):
    t, d = x2.shape
    tm = min(tm, t)
    gate_blk = 7
    row = lambda i: (i, 0)
    const2 = lambda i: (0, 0)
    once = pl.Buffered(1)
    wr_hi = wr.astype(BF16)
    wr_lo = (wr - wr_hi.astype(F32)).astype(BF16)
    kern = functools.partial(_merge_kernel, alpha=alpha)
    return pl.pallas_call(
        kern,
        grid=(t // tm,),
        in_specs=[pl.BlockSpec((tm, d), row), pl.BlockSpec((tm, d), row),
                  pl.BlockSpec((tm, d), row), pl.BlockSpec((tm, d), row),
                  pl.BlockSpec((tm, d), lambda i: (i, gate_blk)),
                  pl.BlockSpec((tm, d), lambda i: (i, gate_blk + 1)),
                  pl.BlockSpec((tm, d), lambda i: (i, gate_blk + 2)),
                  pl.BlockSpec((N_BRANCH, d), const2),
                  pl.BlockSpec((N_BRANCH, d, d), lambda i: (0, 0, 0), pipeline_mode=once),
                  pl.BlockSpec((d, d), const2, pipeline_mode=once),
                  pl.BlockSpec((1, d), const2), pl.BlockSpec((1, d), const2),
                  pl.BlockSpec((d, LANES), const2), pl.BlockSpec((d, LANES), const2),
                  pl.BlockSpec((1, LANES), const2)],
        out_specs=[pl.BlockSpec((tm, d), row), pl.BlockSpec((tm, LANES), row)],
        out_shape=[jax.ShapeDtypeStruct((t, d), F32), jax.ShapeDtypeStruct((t, LANES), F32)],
        compiler_params=_params(1),
        name="merge_ln1_router",
    )(x2, y_da, y_ret, y_mem, proj, proj, proj, b_gate, wb, wo, ln_g, ln_b, wr_hi, wr_lo, br)


PAIRS = ((0, 1), (0, 2), (0, 3), (1, 2), (1, 3), (2, 3))
N_CLASSES = N_GROUPS * len(PAIRS)
ROUTE_CLS, ROUTE_RANK, ROUTE_WA, ROUTE_WB = 0, 1, 2, 3


def _route_kernel(lg_ref, tril_ref, route_ref, cnt_ref, carry_ref):
    i = pl.program_id(0)

    @pl.when(i == 0)
    def _():
        carry_ref[...] = jnp.zeros_like(carry_ref)

    lg = lg_ref[...]
    tm = lg.shape[0]
    col = lax.broadcasted_iota(jnp.int32, lg.shape, 1).astype(F32)
    big = float(LANES)
    ninf = -jnp.inf

    def first_argmax(vals, vmax):
        return jnp.min(jnp.where(vals == vmax, col, big), axis=1, keepdims=True)

    gl = jnp.where(col < N_GROUPS, lg, ninf)
    gmax = jnp.max(gl, axis=1, keepdims=True)
    p_top = 1.0 / jnp.sum(jnp.exp(gl - gmax), axis=1, keepdims=True)
    g_idx = first_argmax(gl, gmax)
    lo = N_GROUPS + g_idx * EXPERTS_PER_GROUP
    el = jnp.where((col >= lo) & (col < lo + EXPERTS_PER_GROUP), lg, ninf)
    e1 = jnp.max(el, axis=1, keepdims=True)
    i1 = first_argmax(el, e1)
    el2 = jnp.where(col == i1, ninf, el)
    e2 = jnp.max(el2, axis=1, keepdims=True)
    i2 = first_argmax(el2, e2)
    r = jnp.exp(e2 - e1)
    w1 = p_top / (1.0 + r)
    w2 = p_top * r / (1.0 + r)

    a_loc = jnp.minimum(i1, i2) - lo
    b_loc = jnp.maximum(i1, i2) - lo
    pair = jnp.where(a_loc == 0.0, b_loc - 1.0, jnp.where(a_loc == 1.0, b_loc + 1.0, 5.0))
    cls = g_idx * float(len(PAIRS)) + pair
    top_is_a = i1 < i2
    w_a = jnp.where(top_is_a, w1, w2)
    w_b = jnp.where(top_is_a, w2, w1)

    onehot = jnp.where(col == cls, 1.0, 0.0)
    prefix = carry_ref[...] + jnp.dot(tril_ref[...], onehot.astype(BF16), preferred_element_type=F32)
    rank = jnp.sum(onehot * prefix, axis=1, keepdims=True) - 1.0
    carry_ref[...] = prefix[tm - 1:tm, :]
    cnt_ref[...] = prefix[tm - 1:tm, :]
    route_ref[...] = jnp.where(col == ROUTE_CLS, cls,
                               jnp.where(col == ROUTE_RANK, rank,
                                         jnp.where(col == ROUTE_WA, w_a,
                                                   jnp.where(col == ROUTE_WB, w_b, 0.0))))


def _route(logits, *, tm):
    t = logits.shape[0]
    tm = min(tm, t)
    tril = jnp.tril(jnp.ones((tm, tm), BF16))
    return pl.pallas_call(
        _route_kernel,
        grid=(t // tm,),
        in_specs=[pl.BlockSpec((tm, LANES), lambda i: (i, 0)),
                  pl.BlockSpec((tm, tm), lambda i: (0, 0))],
        out_specs=[pl.BlockSpec((tm, LANES), lambda i: (i, 0)),
                   pl.BlockSpec((1, LANES), lambda i: (0, 0))],
        out_shape=[jax.ShapeDtypeStruct((t, LANES), F32), jax.ShapeDtypeStruct((1, LANES), F32)],
        scratch_shapes=[pltpu.VMEM((1, LANES), F32)],
        compiler_params=_params(1),
        name="routing",
    )(logits, tril)


def _dispatch_plan(route, counts, *, tm):
    t = route.shape[0]
    n_tiles = t // tm + N_CLASSES
    cnt = counts[0, :N_CLASSES].astype(jnp.int32)
    padded = (cnt + tm - 1) // tm * tm
    ends = jnp.cumsum(padded)
    starts = ends - padded
    cls = route[:, ROUTE_CLS].astype(jnp.int32)
    pos = starts[cls] + route[:, ROUTE_RANK].astype(jnp.int32)
    n_used = ends[-1] // tm
    tile = jnp.arange(n_tiles, dtype=jnp.int32)
    used = (tile < n_used).astype(jnp.int32)
    first_row = jnp.minimum(tile, n_used - 1) * tm
    tile_cls = jnp.sum((ends[None, :] <= first_row[:, None]).astype(jnp.int32), axis=1)
    tile_cls = jnp.minimum(tile_cls, N_CLASSES - 1)
    pairs = jnp.asarray(PAIRS, jnp.int32)
    group = tile_cls // len(PAIRS)
    e_a = group * EXPERTS_PER_GROUP + pairs[tile_cls % len(PAIRS), 0]
    e_b = group * EXPERTS_PER_GROUP + pairs[tile_cls % len(PAIRS), 1]
    return pos, n_tiles, used, e_a, e_b


def _invert_kernel(pos_ref, src_ref, *, n_tokens, n_rows):
    def clear(r, _):
        src_ref[r] = 0
        return 0

    def put(t, _):
        src_ref[pos_ref[t]] = t
        return 0

    lax.fori_loop(0, n_rows, clear, 0, unroll=8)
    lax.fori_loop(0, n_tokens, put, 0, unroll=8)


def _invert(pos, *, n_rows):
    n_tokens = pos.shape[0]
    smem = pl.BlockSpec(memory_space=pltpu.SMEM)
    return pl.pallas_call(
        functools.partial(_invert_kernel, n_tokens=n_tokens, n_rows=n_rows),
        in_specs=[smem], out_specs=smem,
        out_shape=jax.ShapeDtypeStruct((n_rows,), jnp.int32),
        name="invert_permutation",
    )(pos)


def _gather_copy(tab_ref, buf_ref, sem, row, r):
    return pltpu.make_async_copy(tab_ref.at[pl.ds(row, 1), :], buf_ref.at[pl.ds(r, 1), :], sem)


def _gather_start(idx_ref, tab_ref, buf_ref, sem, tile, tm):
    def issue(r, _):
        _gather_copy(tab_ref, buf_ref, sem, idx_ref[tile * tm + r], r).start()
        return 0

    lax.fori_loop(0, tm, issue, 0, unroll=8)


def _gather_wait(tab_ref, buf_ref, sem, tm):
    pltpu.make_async_copy(tab_ref.at[pl.ds(0, tm), :], buf_ref, sem).wait()


def _gather_pipeline(idx_ref, tab_ref, buf_ref, sem_ref, tm, n_tiles):
    i = pl.program_id(0)
    slot = i & 1

    @pl.when(i == 0)
    def _():
        _gather_start(idx_ref, tab_ref, buf_ref.at[0], sem_ref.at[0], 0, tm)

    @pl.when(i + 1 < n_tiles)
    def _():
        _gather_start(idx_ref, tab_ref, buf_ref.at[1 - slot], sem_ref.at[1 - slot], i + 1, tm)

    _gather_wait(tab_ref, buf_ref.at[slot], sem_ref.at[slot], tm)
    return slot


def _dispatch_kernel(src_ref, x_ref, o_ref, buf_ref, sem_ref, *, tm, n_tiles):
    slot = _gather_pipeline(src_ref, x_ref, buf_ref, sem_ref, tm, n_tiles)
    o_ref[...] = buf_ref[slot].astype(o_ref.dtype)


def _dispatch(x1, src, *, tm, n_tiles):
    d = x1.shape[1]
    return pl.pallas_call(
        functools.partial(_dispatch_kernel, tm=tm, n_tiles=n_tiles),
        grid_spec=pltpu.PrefetchScalarGridSpec(
            num_scalar_prefetch=1, grid=(n_tiles,),
            in_specs=[pl.BlockSpec(memory_space=pl.ANY)],
            out_specs=pl.BlockSpec((tm, d), lambda i, src: (i, 0)),
            scratch_shapes=[pltpu.VMEM((2, tm, d), F32), pltpu.SemaphoreType.DMA((2,))]),
        out_shape=jax.ShapeDtypeStruct((n_tiles * tm, d), BF16),
        compiler_params=_params(1),
        name="moe_dispatch",
    )(src, x1)


def _ffn_kernel(ea_ref, eb_ref, used_ref, xs_ref, ws_ref, w1a_ref, w3a_ref, w2a_ref,
                w1b_ref, w3b_ref, w2b_ref, o_ref):
    i = pl.program_id(0)

    @pl.when(used_ref[i] != 0)
    def _():
        x = xs_ref[...]
        ws = ws_ref[...]

        def expert(w1_ref, w3_ref, w2_ref):
            a = jnp.dot(x, w1_ref[...], preferred_element_type=F32)
            b = jnp.dot(x, w3_ref[...], preferred_element_type=F32)
            hid = a * (1.0 / (1.0 + jnp.exp(-a))) * b
            return jnp.dot(hid.astype(BF16), w2_ref[...], preferred_element_type=F32)

        o_ref[...] = (ws[:, 0:1] * expert(w1a_ref, w3a_ref, w2a_ref)
                      + ws[:, 1:2] * expert(w1b_ref, w3b_ref, w2b_ref))

    @pl.when(used_ref[i] == 0)
    def _():
        o_ref[...] = jnp.zeros_like(o_ref)


def _ffn(xs, ws, e_a, e_b, used, w1, w3, w2, *, tm, n_tiles):
    d = xs.shape[1]
    f = w1.shape[-1]
    up_a = pl.BlockSpec((None, d, f), lambda i, ea, eb, used: (ea[i], 0, 0))
    up_b = pl.BlockSpec((None, d, f), lambda i, ea, eb, used: (eb[i], 0, 0))
    down_a = pl.BlockSpec((None, f, d), lambda i, ea, eb, used: (ea[i], 0, 0))
    down_b = pl.BlockSpec((None, f, d), lambda i, ea, eb, used: (eb[i], 0, 0))
    return pl.pallas_call(
        _ffn_kernel,
        grid_spec=pltpu.PrefetchScalarGridSpec(
            num_scalar_prefetch=3, grid=(n_tiles,),
            in_specs=[pl.BlockSpec((tm, d), lambda i, ea, eb, used: (i, 0)),
                      pl.BlockSpec((tm, 2), lambda i, ea, eb, used: (i, 0)),
                      up_a, up_a, down_a, up_b, up_b, down_b],
            out_specs=pl.BlockSpec((tm, d), lambda i, ea, eb, used: (i, 0))),
        out_shape=jax.ShapeDtypeStruct((n_tiles * tm, d), F32),
        compiler_params=_params(1),
        name="experts",
    )(e_a, e_b, used, xs, ws, w1, w3, w2, w1, w3, w2)


def _combine_kernel(pos_ref, x_ref, lng_ref, lnb_ref, ys_ref, o_ref, buf_ref, sem_ref,
                    *, tm, n_tiles, alpha):
    slot = _gather_pipeline(pos_ref, ys_ref, buf_ref, sem_ref, tm, n_tiles)
    o_ref[...] = _layer_norm(alpha * x_ref[...] + buf_ref[slot], lng_ref[...], lnb_ref[...])


def _combine(x1, ys, pos, ln_g, ln_b, *, alpha, tm):
    t, d = x1.shape
    n_tiles = t // tm
    return pl.pallas_call(
        functools.partial(_combine_kernel, tm=tm, n_tiles=n_tiles, alpha=alpha),
        grid_spec=pltpu.PrefetchScalarGridSpec(
            num_scalar_prefetch=1, grid=(n_tiles,),
            in_specs=[pl.BlockSpec((tm, d), lambda i, pos: (i, 0)),
                      pl.BlockSpec((1, d), lambda i, pos: (0, 0)),
                      pl.BlockSpec((1, d), lambda i, pos: (0, 0)),
                      pl.BlockSpec(memory_space=pl.ANY)],
            out_specs=pl.BlockSpec((tm, d), lambda i, pos: (i, 0)),
            scratch_shapes=[pltpu.VMEM((2, tm, d), F32), pltpu.SemaphoreType.DMA((2,))]),
        out_shape=jax.ShapeDtypeStruct((t, d), F32),
        compiler_params=_params(1),
        name="moe_combine_ln2",
    )(pos, x1, ln_g, ln_b, ys)


def _moe(x1, logits, w1, w3, w2, ln_g, ln_b, *, alpha, tm):
    tm = min(tm, x1.shape[0])
    route, counts = _route(logits, tm=tm)
    pos, n_tiles, used, e_a, e_b = _dispatch_plan(route, counts, tm=tm)
    src = _invert(pos, n_rows=n_tiles * tm)
    ws = route[src, ROUTE_WA:ROUTE_WB + 1]
    xs = _dispatch(x1, src, tm=tm, n_tiles=n_tiles)
    ys = _ffn(xs, ws, e_a, e_b, used, w1, w3, w2, tm=tm, n_tiles=n_tiles)
    return _combine(x1, ys, pos, ln_g, ln_b, alpha=alpha, tm=tm)


def kernel(x, mem, w_in, b_gate, w_mem_kv, da_lambda, da_norm_g, ret_gn_g, ret_gn_b, w_branch, w_o,
           ln1_g, ln1_b, w_rg, b_rg, w_re, b_re, w1, w3, w2, ln2_g, ln2_b):
    batch, seq, d = x.shape
    mem_len = mem.shape[1]
    depth = w_in.shape[0]
    alpha = (2.0 * depth) ** 0.25
    t = batch * seq
    x2 = x.reshape(t, d)
    for l in range(depth):
        lambda_init = 0.8 - 0.6 * math.exp(-0.3 * l)
        proj = _matmul(x2, w_in[l].astype(BF16), tm=2048, tn=1024, name="in_proj")
        mem_kv = _matmul(mem.reshape(batch * mem_len, d), w_mem_kv[l].astype(BF16),
                         tm=1024, tn=1024, name="mem_kv_proj")
        y_da = _diff_attention(proj, da_lambda[l], da_norm_g[l], batch=batch, seq=seq, d=d,
                               lambda_init=lambda_init, tile=256)
        y_ret = _retention(proj, ret_gn_g[l], ret_gn_b[l], batch=batch, seq=seq, d=d)
        y_mem = _memory_attention(proj, mem_kv, batch=batch, seq=seq, mem_len=mem_len, d=d, tq=512)
        n_route = N_GROUPS + N_EXPERTS
        wr = jnp.pad(jnp.concatenate([w_rg[l], w_re[l]], axis=1), ((0, 0), (0, LANES - n_route)))
        br = jnp.pad(jnp.concatenate([b_rg[l], b_re[l]]), (0, LANES - n_route)).reshape(1, LANES)
        x1, logits = _merge(x2, y_da, y_ret, y_mem, proj, b_gate[l].reshape(N_BRANCH, d),
                            w_branch[l].astype(BF16), w_o[l].astype(BF16),
                            ln1_g[l].reshape(1, d), ln1_b[l].reshape(1, d), wr, br,
                            alpha=alpha, tm=512)
        x2 = _moe(x1, logits, w1[l].astype(BF16), w3[l].astype(BF16), w2[l].astype(BF16),
                  ln2_g[l].reshape(1, d), ln2_b[l].reshape(1, d), alpha=alpha, tm=256)
    return x2.reshape(batch, seq, d)
```

```python
import functools
import math

import jax
import jax.numpy as jnp
import numpy as np
from jax import lax
from jax.experimental import pallas as pl
from jax.experimental.pallas import tpu as pltpu

BF16 = jnp.bfloat16
F32 = jnp.float32

DA_HEADS = 8
RET_HEADS = 4
RET_CHUNK = 128
MEM_HEADS = 4
N_BRANCH = 3
N_GROUPS = 4
EXPERTS_PER_GROUP = 4
N_EXPERTS = N_GROUPS * EXPERTS_PER_GROUP
EPS = 1e-5
LANES = 128
VMEM_LIMIT = 56 * 1024 * 1024

_NT = (((1,), (1,)), ((), ()))


def _params(n_axes):
    return pltpu.CompilerParams(dimension_semantics=("arbitrary",) * n_axes,
                                vmem_limit_bytes=VMEM_LIMIT)


def _layer_norm(z, g, b):
    mu = jnp.mean(z, axis=-1, keepdims=True)
    zc = z - mu
    var = jnp.mean(zc * zc, axis=-1, keepdims=True)
    return zc * lax.rsqrt(var + EPS) * g + b


def _matmul_kernel(x_ref, w_ref, o_ref):
    o_ref[...] = jnp.dot(x_ref[...].astype(BF16), w_ref[...],
                         preferred_element_type=F32).astype(o_ref.dtype)


def _matmul(x, w, *, tm, tn, name):
    m, k = x.shape
    n = w.shape[1]
    tm, tn = min(tm, m), min(tn, n)
    return pl.pallas_call(
        _matmul_kernel,
        grid=(m // tm, n // tn),
        in_specs=[pl.BlockSpec((tm, k), lambda i, j: (i, 0)),
                  pl.BlockSpec((k, tn), lambda i, j: (0, j))],
        out_specs=pl.BlockSpec((tm, tn), lambda i, j: (i, j)),
        out_shape=jax.ShapeDtypeStruct((m, n), BF16),
        compiler_params=_params(2),
        name=name,
    )(x, w)


DA_ONES_ROWS = 16
DA_MASKED = -1e30
DA_CHUNK_ROWS = 32
LOG2_E = 1.4426950408889634


def _da_schedule(n_blk):
    full = [(qi, p) for qi in range(n_blk) for p in range(qi // 2)]
    order = full + [(qi, qi // 2) for qi in range(n_blk)]
    order.append(order[-1])
    table = np.asarray(order, np.int32)
    return len(full), table[:, 0], table[:, 1]


def _da_kernel(sq_ref, sp_ref, slope_ref, lam_ref, g_ref, q_ref, k_ref, v_ref, o_ref,
               ka0_ref, ka1_ref, qa0_ref, qa1_ref, vt_ref, s_ref, acc0_ref, acc1_ref, m0_ref, m1_ref,
               *, tile, dk, n_blk, n_full, lambda_init):
    h = pl.program_id(1)
    a_log2 = slope_ref[h] * LOG2_E
    w = 2 * dk
    ka_refs = (ka0_ref, ka1_ref)
    qa_refs = (qa0_ref, qa1_ref)
    acc_refs = (acc0_ref, acc1_ref)
    m_refs = (m0_ref, m1_ref)
    lane = lax.broadcasted_iota(jnp.int32, (tile, w), 1)
    pos = lax.broadcasted_iota(jnp.int32, (tile, w), 0).astype(F32)
    pieces = []
    rest = jnp.full((tile, w), a_log2, F32)
    for _ in range(3):
        piece = rest.astype(BF16).astype(F32)
        pieces.append(piece)
        rest = rest - piece
    in_comp, k_bias, q_bias = [], [], []
    for c in range(2):
        free = dk * (1 - c)
        in_comp.append((lane >= c * dk) & (lane < (c + 1) * dk))
        kb = jnp.where((lane >= free) & (lane < free + 3), pos, 0.0)
        qb = jnp.where((lane >= free + 3) & (lane < free + 6), -pos, 0.0)
        for i in range(3):
            kb = jnp.where(lane == free + 3 + i, pieces[i], kb)
            qb = jnp.where(lane == free + i, pieces[i], qb)
        k_bias.append(kb)
        q_bias.append(qb)

    def prepare(pp, _):
        for jj in range(2):
            blk = 2 * pp + jj
            rows = pl.ds(pl.multiple_of(blk * tile, tile), tile)
            kf = k_ref[rows, :].astype(F32)
            qf = q_ref[rows, :].astype(F32) * (dk ** -0.5 * LOG2_E)
            for c in range(2):
                ka_refs[c][rows, :] = jnp.where(in_comp[c], kf, k_bias[c]).astype(BF16)
                qa_refs[c][rows, :] = jnp.where(in_comp[c], qf, q_bias[c]).astype(BF16)
                acc_refs[c][blk] = jnp.zeros((w + DA_ONES_ROWS, tile), F32)
                m_refs[c][blk] = jnp.full((1, tile), -jnp.inf, F32)
            vt_ref[pp, 0:w, jj * tile:(jj + 1) * tile] = v_ref[rows, :].astype(F32).T.astype(BF16)
        vt_ref[pp, w:w + DA_ONES_ROWS, :] = jnp.ones((DA_ONES_ROWS, 2 * tile), BF16)
        return 0

    lax.fori_loop(0, n_blk // 2, prepare, 0)

    lv = lam_ref[...]
    lam = (jnp.exp(jnp.sum(lv[0:1] * lv[1:2], axis=1, keepdims=True))
           - jnp.exp(jnp.sum(lv[2:3] * lv[3:4], axis=1, keepdims=True)) + lambda_init)
    chunk = DA_CHUNK_ROWS
    n_chunks = tile // chunk
    rel0 = (lax.broadcasted_iota(jnp.int32, (chunk, tile), 0)
            - lax.broadcasted_iota(jnp.int32, (chunk, tile), 1))

    def issue_scores(t, slot, n_kv):
        qi, p = sq_ref[t], sp_ref[t]
        q_rows = pl.ds(pl.multiple_of(qi * tile, tile), tile)
        for jj in range(n_kv):
            k_rows = pl.ds(pl.multiple_of((2 * p + jj) * tile, tile), tile)
            for c in range(2):
                s_ref[slot, 2 * jj + c] = lax.dot_general(
                    ka_refs[c][k_rows, :], qa_refs[c][q_rows, :], _NT, preferred_element_type=F32)

    def consume(t, slot, diag):
        n_kv = 1 if diag == 0 else 2
        qi, p = sq_ref[t], sp_ref[t]
        offs = [a_log2 * ((2 * p + jj - qi) * tile).astype(F32) for jj in range(n_kv)]
        accs = []
        for c in range(2):
            tiles = [s_ref.at[slot, 2 * jj + c] for jj in range(n_kv)]
            m_old = m_refs[c][qi]
            m_new = m_old
            for jj in range(n_kv):
                col_max = None
                for r in range(n_chunks):
                    rows = slice(r * chunk, (r + 1) * chunk)
                    x = tiles[jj][rows, :]
                    if jj == diag:
                        x = jnp.where(rel0 <= -r * chunk, x, DA_MASKED)
                        tiles[jj][rows, :] = x
                    col_max = x if col_max is None else jnp.maximum(col_max, x)
                m_new = jnp.maximum(m_new, jnp.max(col_max, axis=0, keepdims=True) + offs[jj])
            pts = []
            for jj in range(n_kv):
                shift = m_new - offs[jj]
                for r in range(n_chunks):
                    x = tiles[jj][r * chunk:(r + 1) * chunk, :]
                    pts.append(jnp.exp2(x - shift).astype(BF16))
            vt = vt_ref[p] if n_kv == 2 else vt_ref[p, :, 0:tile]
            pv = jnp.dot(vt, jnp.concatenate(pts, axis=0), preferred_element_type=F32)
            acc = jnp.exp2(m_old - m_new) * acc_refs[c][qi] + pv
            if diag is None:
                acc_refs[c][qi] = acc
                m_refs[c][qi] = m_new
            else:
                accs.append(acc)
        if diag is not None:
            a0, a1 = accs
            ot = a0[0:w] / a0[w:w + 1] - lam * (a1[0:w] / a1[w:w + 1])
            o = ot.T
            o = o * lax.rsqrt(jnp.mean(o * o, axis=1, keepdims=True) + EPS) * g_ref[...]
            q_rows = pl.ds(pl.multiple_of(qi * tile, tile), tile)
            o_ref[q_rows, :] = (o * (1.0 - lambda_init)).astype(o_ref.dtype)

    def two_entries(i, base, last):
        t = base + 2 * i
        issue_scores(t + 1, 1, 2)
        consume(t, 0, 0 if last else None)
        issue_scores(t + 2, 0, 1 if last else 2)
        consume(t + 1, 1, 1 if last else None)
        return 0

    issue_scores(0, 0, 2 if n_full else 1)
    lax.fori_loop(0, n_full // 2, lambda i, _: two_entries(i, 0, False), 0)
    lax.fori_loop(0, n_blk // 2, lambda i, _: two_entries(i, n_full, True), 0)


def _diff_attention(proj, lam_vecs, norm_g, *, batch, seq, d, lambda_init, tile):
    dk = d // (2 * DA_HEADS)
    w = 2 * dk
    tile = min(tile, seq)
    assert tile <= 256 and seq % (2 * tile) == 0
    n_blk = seq // tile
    hpb = d // w
    slopes = 2.0 ** (-8.0 * (jnp.arange(DA_HEADS, dtype=F32) + 1.0) / DA_HEADS)
    n_full, sched_q, sched_p = _da_schedule(n_blk)
    kern = functools.partial(_da_kernel, tile=tile, dk=dk, n_blk=n_blk, n_full=n_full,
                             lambda_init=lambda_init)
    smem = pl.BlockSpec(memory_space=pltpu.SMEM)
    return pl.pallas_call(
        kern,
        grid=(batch, DA_HEADS),
        in_specs=[smem, smem, smem,
                  pl.BlockSpec((4, dk), lambda b, h: (0, 0)),
                  pl.BlockSpec((1, w), lambda b, h: (0, 0)),
                  pl.BlockSpec((seq, w), lambda b, h: (b, h)),
                  pl.BlockSpec((seq, w), lambda b, h: (b, hpb + h)),
                  pl.BlockSpec((seq, w), lambda b, h: (b, 2 * hpb + h))],
        out_specs=pl.BlockSpec((seq, w), lambda b, h: (b, h)),
        out_shape=jax.ShapeDtypeStruct((batch * seq, d), BF16),
        scratch_shapes=[pltpu.VMEM((seq, w), BF16), pltpu.VMEM((seq, w), BF16),
                        pltpu.VMEM((seq, w), BF16), pltpu.VMEM((seq, w), BF16),
                        pltpu.VMEM((n_blk // 2, w + DA_ONES_ROWS, 2 * tile), BF16),
                        pltpu.VMEM((2, 4, tile, tile), F32),
                        pltpu.VMEM((n_blk, w + DA_ONES_ROWS, tile), F32),
                        pltpu.VMEM((n_blk, w + DA_ONES_ROWS, tile), F32),
                        pltpu.VMEM((n_blk, 1, tile), F32),
                        pltpu.VMEM((n_blk, 1, tile), F32)],
        compiler_params=_params(2),
        name="diff_attention",
    )(jnp.asarray(sched_q), jnp.asarray(sched_p), slopes, lam_vecs, norm_g.reshape(1, w),
      proj, proj, proj)


def _ret_kernel(cd_ref, intra_ref, qd_ref, kd_ref, gng_ref, gnb_ref,
                q_ref, k_ref, v_ref, g_ref, o_ref, state_ref, *, n_chunks, dk, dv):
    c_len = RET_CHUNK
    heads = range(RET_HEADS)
    state_ref[...] = jnp.zeros_like(state_ref)

    def chunk(c, _):
        rows = pl.ds(pl.multiple_of(c * c_len, c_len), c_len)
        q = [q_ref[rows, h * dk:(h + 1) * dk] for h in heads]
        k = [k_ref[rows, h * dk:(h + 1) * dk] for h in heads]
        v = [v_ref[rows, h * dv:(h + 1) * dv] for h in heads]
        state = [state_ref[h] for h in heads]
        s = [lax.dot_general(q[h], k[h], _NT, preferred_element_type=F32) for h in heads]
        cross = [jnp.dot(q[h], state[h].astype(BF16), preferred_element_type=F32) for h in heads]
        kv = [jnp.dot((k[h].astype(F32) * kd_ref[h]).T.astype(BF16), v[h],
                      preferred_element_type=F32) for h in heads]
        o = [jnp.dot((s[h] * intra_ref[h]).astype(BF16), v[h], preferred_element_type=F32)
             + cross[h] * qd_ref[h] for h in heads]
        for h in heads:
            state_ref[h] = state[h] * cd_ref[h] + kv[h]
            cols = slice(h * dv, (h + 1) * dv)
            mu = jnp.mean(o[h], axis=1, keepdims=True)
            oc = o[h] - mu
            var = jnp.mean(oc * oc, axis=1, keepdims=True)
            y = oc * lax.rsqrt(var + EPS) * gng_ref[:, cols] + gnb_ref[:, cols]
            gate = g_ref[rows, cols].astype(F32)
            y = gate * (1.0 / (1.0 + jnp.exp(-gate))) * y
            o_ref[rows, cols] = y.astype(o_ref.dtype)
        return 0

    lax.fori_loop(0, n_chunks, chunk, 0)


def _retention(proj, gn_g, gn_b, *, batch, seq, d):
    dv = d // RET_HEADS
    dk = dv // 2
    c_len = RET_CHUNK
    n_chunks = seq // c_len
    log_g = jnp.log(1.0 - 2.0 ** (-5.0 - jnp.arange(RET_HEADS, dtype=F32)))
    idx = jnp.arange(c_len, dtype=F32)
    rel = idx[:, None] - idx[None, :]
    intra = jnp.where(rel[None] >= 0, jnp.exp(rel[None] * log_g[:, None, None]), 0.0) * dk ** -0.5
    q_decay = jnp.exp((idx[None, :] + 1.0) * log_g[:, None])[:, :, None]
    k_decay = (jnp.exp((c_len - 1.0 - idx[None, :]) * log_g[:, None]) * dk ** -0.5)[:, :, None]
    chunk_decay = jnp.exp(c_len * log_g)
    hk = RET_HEADS * dk
    q_blk = 3 * d // hk
    k_blk = q_blk + 1
    v_blk = (3 * d + 2 * hk) // d
    g_blk = v_blk + 1
    kern = functools.partial(_ret_kernel, n_chunks=n_chunks, dk=dk, dv=dv)
    const3 = lambda b: (0, 0, 0)
    return pl.pallas_call(
        kern,
        grid=(batch,),
        in_specs=[pl.BlockSpec(memory_space=pltpu.SMEM),
                  pl.BlockSpec((RET_HEADS, c_len, c_len), const3),
                  pl.BlockSpec((RET_HEADS, c_len, 1), const3),
                  pl.BlockSpec((RET_HEADS, c_len, 1), const3),
                  pl.BlockSpec((1, d), lambda b: (0, 0)),
                  pl.BlockSpec((1, d), lambda b: (0, 0)),
                  pl.BlockSpec((seq, hk), lambda b: (b, q_blk)),
                  pl.BlockSpec((seq, hk), lambda b: (b, k_blk)),
                  pl.BlockSpec((seq, d), lambda b: (b, v_blk)),
                  pl.BlockSpec((seq, d), lambda b: (b, g_blk))],
        out_specs=pl.BlockSpec((seq, d), lambda b: (b, 0)),
        out_shape=jax.ShapeDtypeStruct((batch * seq, d), BF16),
        scratch_shapes=[pltpu.VMEM((RET_HEADS, dk, dv), F32)],
        compiler_params=_params(1),
        name="retention",
    )(chunk_decay, intra, q_decay, k_decay, gn_g.reshape(1, d), gn_b.reshape(1, d),
      proj, proj, proj, proj)


def _mem_kernel(q_ref, kv_ref, o_ref, *, d, dh):
    for h in range(MEM_HEADS):
        q = q_ref[:, h * dh:(h + 1) * dh]
        k = kv_ref[:, h * dh:(h + 1) * dh]
        v = kv_ref[:, d + h * dh:d + (h + 1) * dh]
        s = lax.dot_general(q, k, _NT, preferred_element_type=F32) * dh ** -0.5
        p = jnp.exp(s - jnp.max(s, axis=1, keepdims=True))
        l = jnp.sum(p, axis=1, keepdims=True)
        o = jnp.dot(p.astype(BF16), v, preferred_element_type=F32) / l
        o_ref[:, h * dh:(h + 1) * dh] = o.astype(o_ref.dtype)


def _memory_attention(proj, mem_kv, *, batch, seq, mem_len, d, tq):
    dh = d // MEM_HEADS
    tq = min(tq, seq)
    nq = seq // tq
    q_blk = 6
    kern = functools.partial(_mem_kernel, d=d, dh=dh)
    return pl.pallas_call(
        kern,
        grid=(batch, nq),
        in_specs=[pl.BlockSpec((tq, d), lambda b, i: (b * nq + i, q_blk)),
                  pl.BlockSpec((mem_len, 2 * d), lambda b, i: (b, 0))],
        out_specs=pl.BlockSpec((tq, d), lambda b, i: (b * nq + i, 0)),
        out_shape=jax.ShapeDtypeStruct((batch * seq, d), BF16),
        compiler_params=_params(2),
        name="memory_attention",
    )(proj, mem_kv)


def _merge_kernel(x_ref, yda_ref, yret_ref, ymem_ref, g0_ref, g1_ref, g2_ref, bg_ref,
                  wb_ref, wo_ref, lng_ref, lnb_ref, wrh_ref, wrl_ref, br_ref, x1_ref, lg_ref,
                  *, alpha):
    ys = (yda_ref, yret_ref, ymem_ref)
    gs = (g0_ref, g1_ref, g2_ref)
    merged = None
    for n in range(N_BRANCH):
        gate = gs[n][...].astype(F32) + bg_ref[n:n + 1, :]
        gate = 1.0 / (1.0 + jnp.exp(-gate))
        term = gate * jnp.dot(ys[n][...], wb_ref[n], preferred_element_type=F32)
        merged = term if merged is None else merged + term
    out = jnp.dot(merged.astype(BF16), wo_ref[...], preferred_element_type=F32)
    x1 = _layer_norm(alpha * x_ref[...] + out, lng_ref[...], lnb_ref[...])
    _to_slabs(x1_ref, x1, x1.shape[0], x1.shape[1] // LANES)
    x_hi = x1.astype(BF16)
    x_lo = (x1 - x_hi.astype(F32)).astype(BF16)
    lg_ref[...] = (jnp.dot(x_hi, wrh_ref[...], preferred_element_type=F32)
                   + jnp.dot(x_lo, wrh_ref[...], preferred_element_type=F32)
                   + jnp.dot(x_hi, wrl_ref[...], preferred_element_type=F32) + br_ref[...])


def _merge(x2, y_da, y_ret, y_mem, proj, b_gate, wb, wo, ln_g, ln_b, wr, br, *, alpha, tm):
    t, d = x2.shape
    tm = min(tm, t)
    gate_blk = 7
    row = lambda i: (i, 0)
    const2 = lambda i: (0, 0)
    once = pl.Buffered(1)
    wr_hi = wr.astype(BF16)
    wr_lo = (wr - wr_hi.astype(F32)).astype(BF16)
    kern = functools.partial(_merge_kernel, alpha=alpha)
    return pl.pallas_call(
        kern,
        grid=(t // tm,),
        in_specs=[pl.BlockSpec((tm, d), row), pl.BlockSpec((tm, d), row),
                  pl.BlockSpec((tm, d), row), pl.BlockSpec((tm, d), row),
                  pl.BlockSpec((tm, d), lambda i: (i, gate_blk)),
                  pl.BlockSpec((tm, d), lambda i: (i, gate_blk + 1)),
                  pl.BlockSpec((tm, d), lambda i: (i, gate_blk + 2)),
                  pl.BlockSpec((N_BRANCH, d), const2),
                  pl.BlockSpec((N_BRANCH, d, d), lambda i: (0, 0, 0), pipeline_mode=once),
                  pl.BlockSpec((d, d), const2, pipeline_mode=once),
                  pl.BlockSpec((1, d), const2), pl.BlockSpec((1, d), const2),
                  pl.BlockSpec((d, LANES), const2), pl.BlockSpec((d, LANES), const2),
                  pl.BlockSpec((1, LANES), const2)],
        out_specs=[pl.BlockSpec((tm * d // LANES, LANES), row), pl.BlockSpec((tm, LANES), row)],
        out_shape=[jax.ShapeDtypeStruct((t * d // LANES, LANES), F32),
                   jax.ShapeDtypeStruct((t, LANES), F32)],
        compiler_params=_params(1),
        name="merge_ln1_router",
    )(x2, y_da, y_ret, y_mem, proj, proj, proj, b_gate, wb, wo, ln_g, ln_b, wr_hi, wr_lo, br)


PAIRS = ((0, 1), (0, 2), (0, 3), (1, 2), (1, 3), (2, 3))
N_CLASSES = N_GROUPS * len(PAIRS)
ROUTE_CLS, ROUTE_RANK, ROUTE_WA, ROUTE_WB = 0, 1, 2, 3


def _route_kernel(lg_ref, tril_ref, route_ref, key_ref, cnt_ref, carry_ref, *, key_shift):
    i = pl.program_id(0)

    @pl.when(i == 0)
    def _():
        carry_ref[...] = jnp.zeros_like(carry_ref)

    lg = lg_ref[...]
    tm = lg.shape[0]
    col = lax.broadcasted_iota(jnp.int32, lg.shape, 1).astype(F32)
    big = float(LANES)
    ninf = -jnp.inf

    def first_argmax(vals, vmax):
        return jnp.min(jnp.where(vals == vmax, col, big), axis=1, keepdims=True)

    gl = jnp.where(col < N_GROUPS, lg, ninf)
    gmax = jnp.max(gl, axis=1, keepdims=True)
    p_top = 1.0 / jnp.sum(jnp.exp(gl - gmax), axis=1, keepdims=True)
    g_idx = first_argmax(gl, gmax)
    lo = N_GROUPS + g_idx * EXPERTS_PER_GROUP
    el = jnp.where((col >= lo) & (col < lo + EXPERTS_PER_GROUP), lg, ninf)
    e1 = jnp.max(el, axis=1, keepdims=True)
    i1 = first_argmax(el, e1)
    el2 = jnp.where(col == i1, ninf, el)
    e2 = jnp.max(el2, axis=1, keepdims=True)
    i2 = first_argmax(el2, e2)
    r = jnp.exp(e2 - e1)
    w1 = p_top / (1.0 + r)
    w2 = p_top * r / (1.0 + r)

    a_loc = jnp.minimum(i1, i2) - lo
    b_loc = jnp.maximum(i1, i2) - lo
    pair = jnp.where(a_loc == 0.0, b_loc - 1.0, jnp.where(a_loc == 1.0, b_loc + 1.0, 5.0))
    cls = g_idx * float(len(PAIRS)) + pair
    top_is_a = i1 < i2
    w_a = jnp.where(top_is_a, w1, w2)
    w_b = jnp.where(top_is_a, w2, w1)

    onehot = jnp.where(col == cls, 1.0, 0.0)
    prefix = carry_ref[...] + jnp.dot(tril_ref[...], onehot.astype(BF16), preferred_element_type=F32)
    rank = jnp.sum(onehot * prefix, axis=1, keepdims=True) - 1.0
    carry_ref[...] = prefix[tm - 1:tm, :]
    cnt_ref[...] = prefix[tm - 1:tm, :].astype(jnp.int32)
    route_ref[...] = jnp.where(col == ROUTE_CLS, cls,
                               jnp.where(col == ROUTE_RANK, rank,
                                         jnp.where(col == ROUTE_WA, w_a,
                                                   jnp.where(col == ROUTE_WB, w_b, 0.0))))
    key = jnp.where(col == 0.0, cls * float(1 << key_shift) + rank, 0.0).T
    key_ref[...] = key[0:1, :].astype(jnp.int32)


def _route(logits, *, tm, key_shift):
    t = logits.shape[0]
    tm = min(tm, t)
    tril = jnp.tril(jnp.ones((tm, tm), BF16))
    return pl.pallas_call(
        functools.partial(_route_kernel, key_shift=key_shift),
        grid=(t // tm,),
        in_specs=[pl.BlockSpec((tm, LANES), lambda i: (i, 0)),
                  pl.BlockSpec((tm, tm), lambda i: (0, 0))],
        out_specs=[pl.BlockSpec((tm, LANES), lambda i: (i, 0)),
                   pl.BlockSpec((1, tm), lambda i: (0, i)),
                   pl.BlockSpec((1, LANES), lambda i: (0, 0))],
        out_shape=[jax.ShapeDtypeStruct((t, LANES), F32),
                   jax.ShapeDtypeStruct((1, t), jnp.int32),
                   jax.ShapeDtypeStruct((1, LANES), jnp.int32)],
        scratch_shapes=[pltpu.VMEM((1, LANES), F32)],
        compiler_params=_params(1),
        name="routing",
    )(logits, tril)


def _plan_kernel(key_ref, cnt_ref, pos_ref, src_ref, ea_ref, eb_ref, used_ref, starts_ref,
                 *, n_tokens, n_tiles, tm, key_shift):
    tm_shift = tm.bit_length() - 1
    row0 = jnp.int32(0)
    tile0 = jnp.int32(0)
    last_a = jnp.int32(0)
    last_b = jnp.int32(1)
    for c in range(N_CLASSES):
        e_a = c // len(PAIRS) * EXPERTS_PER_GROUP + PAIRS[c % len(PAIRS)][0]
        e_b = c // len(PAIRS) * EXPERTS_PER_GROUP + PAIRS[c % len(PAIRS)][1]
        cnt = cnt_ref[c]
        n_cls_tiles = lax.shift_right_logical(cnt + (tm - 1), tm_shift)
        starts_ref[c] = row0

        def mark(i, _, e_a=e_a, e_b=e_b):
            ea_ref[i] = e_a
            eb_ref[i] = e_b
            used_ref[i] = 1
            return 0

        lax.fori_loop(tile0, tile0 + n_cls_tiles, mark, 0)
        has = n_cls_tiles > 0
        last_a = jnp.where(has, e_a, last_a)
        last_b = jnp.where(has, e_b, last_b)
        row_end = row0 + n_cls_tiles * tm

        def pad(r, _):
            src_ref[r] = 0
            return 0

        lax.fori_loop(row0 + cnt, row_end, pad, 0)
        row0 = row_end
        tile0 = tile0 + n_cls_tiles

    def unused(i, _):
        ea_ref[i] = last_a
        eb_ref[i] = last_b
        used_ref[i] = 0
        return 0

    lax.fori_loop(tile0, n_tiles, unused, 0)
    lax.fori_loop(row0, n_tiles * tm, pad, 0)

    def put(t, _):
        key = key_ref[t]
        p = starts_ref[lax.shift_right_logical(key, key_shift)] + (key & ((1 << key_shift) - 1))
        pos_ref[t] = p
        src_ref[p] = t
        return 0

    lax.fori_loop(0, n_tokens, put, 0, unroll=8)


def _plan(key, counts, *, tm, key_shift):
    n_tokens = key.shape[0]
    assert tm & (tm - 1) == 0
    n_tiles = n_tokens // tm + N_CLASSES
    smem = pl.BlockSpec(memory_space=pltpu.SMEM)
    tiles = jax.ShapeDtypeStruct((n_tiles,), jnp.int32)
    pos, src, e_a, e_b, used = pl.pallas_call(
        functools.partial(_plan_kernel, n_tokens=n_tokens, n_tiles=n_tiles, tm=tm,
                          key_shift=key_shift),
        in_specs=[smem, smem], out_specs=[smem] * 5,
        out_shape=[jax.ShapeDtypeStruct((n_tokens,), jnp.int32),
                   jax.ShapeDtypeStruct((n_tiles * tm,), jnp.int32), tiles, tiles, tiles],
        scratch_shapes=[pltpu.SMEM((N_CLASSES,), jnp.int32)],
        name="moe_plan",
    )(key, counts)
    return pos, src, e_a, e_b, used, n_tiles


def _to_slabs(ref, value, n_rows, stride, offset=0):
    for j in range(value.shape[1] // LANES):
        ref[pl.ds(offset + j, n_rows, stride=stride), :] = value[:, j * LANES:(j + 1) * LANES]


def _from_slabs(ref, n_rows, n_blocks, stride, offset=0):
    return jnp.concatenate([ref[pl.ds(offset + j, n_rows, stride=stride), :]
                            for j in range(n_blocks)], axis=1)


def _gather_copy(tab_ref, buf_ref, sem, row, r, rec):
    return pltpu.make_async_copy(tab_ref.at[pl.ds(pl.multiple_of(row * rec, rec), rec), :],
                                 buf_ref.at[pl.ds(pl.multiple_of(r * rec, rec), rec), :], sem)


def _gather_start(idx_ref, tab_ref, buf_ref, sem, tile, tm, rec):
    def issue(r, _):
        _gather_copy(tab_ref, buf_ref, sem, idx_ref[tile * tm + r], r, rec).start()
        return 0

    lax.fori_loop(0, tm, issue, 0, unroll=8)


def _gather_wait(tab_ref, buf_ref, sem, tm, rec):
    pltpu.make_async_copy(tab_ref.at[pl.ds(0, tm * rec), :], buf_ref, sem).wait()


def _gather_pipeline(idx_ref, tab_ref, buf_ref, sem_ref, tm, n_tiles, rec, wanted=None):
    i = pl.program_id(0)
    slot = i & 1
    nxt = jnp.minimum(i + 1, n_tiles - 1)
    want = (lambda tile: tile >= 0) if wanted is None else wanted

    @pl.when((i == 0) & want(i))
    def _():
        _gather_start(idx_ref, tab_ref, buf_ref.at[0], sem_ref.at[0], 0, tm, rec)

    @pl.when((i + 1 < n_tiles) & want(nxt))
    def _():
        _gather_start(idx_ref, tab_ref, buf_ref.at[1 - slot], sem_ref.at[1 - slot], nxt, tm, rec)

    @pl.when(want(i))
    def _():
        _gather_wait(tab_ref, buf_ref.at[slot], sem_ref.at[slot], tm, rec)

    return slot


def _experts_kernel(src_ref, ea_ref, eb_ref, used_ref, x_ref, w1a_ref, w3a_ref, w2a_ref,
                    w1b_ref, w3b_ref, w2b_ref, o_ref, buf_ref, sem_ref, *, tm, n_tiles, n_blocks):
    i = pl.program_id(0)
    used = lambda tile: used_ref[tile] != 0
    slot = _gather_pipeline(src_ref, x_ref, buf_ref, sem_ref, tm, n_tiles, n_blocks, wanted=used)

    @pl.when(used(i))
    def _():
        x = _from_slabs(buf_ref.at[slot], tm, n_blocks, n_blocks).astype(BF16)

        def expert(w1_ref, w3_ref, w2_ref):
            a = jnp.dot(x, w1_ref[...], preferred_element_type=F32)
            b = jnp.dot(x, w3_ref[...], preferred_element_type=F32)
            hid = a * (1.0 / (1.0 + jnp.exp(-a))) * b
            return jnp.dot(hid.astype(BF16), w2_ref[...], preferred_element_type=F32)

        _to_slabs(o_ref, expert(w1a_ref, w3a_ref, w2a_ref), tm, 2 * n_blocks)
        _to_slabs(o_ref, expert(w1b_ref, w3b_ref, w2b_ref), tm, 2 * n_blocks, offset=n_blocks)

    @pl.when(jnp.logical_not(used(i)))
    def _():
        o_ref[...] = jnp.zeros_like(o_ref)


def _experts(x1s, src, e_a, e_b, used, w1, w3, w2, *, tm, n_tiles):
    d, f = w1.shape[1], w1.shape[2]
    n_blocks = d // LANES
    up_a = pl.BlockSpec((None, d, f), lambda i, src, ea, eb, used: (ea[i], 0, 0))
    up_b = pl.BlockSpec((None, d, f), lambda i, src, ea, eb, used: (eb[i], 0, 0))
    down_a = pl.BlockSpec((None, f, d), lambda i, src, ea, eb, used: (ea[i], 0, 0))
    down_b = pl.BlockSpec((None, f, d), lambda i, src, ea, eb, used: (eb[i], 0, 0))
    return pl.pallas_call(
        functools.partial(_experts_kernel, tm=tm, n_tiles=n_tiles, n_blocks=n_blocks),
        grid_spec=pltpu.PrefetchScalarGridSpec(
            num_scalar_prefetch=4, grid=(n_tiles,),
            in_specs=[pl.BlockSpec(memory_space=pl.ANY),
                      up_a, up_a, down_a, up_b, up_b, down_b],
            out_specs=pl.BlockSpec((tm * 2 * n_blocks, LANES), lambda i, src, ea, eb, used: (i, 0)),
            scratch_shapes=[pltpu.VMEM((2, tm * n_blocks, LANES), F32),
                            pltpu.SemaphoreType.DMA((2,))]),
        out_shape=jax.ShapeDtypeStruct((n_tiles * tm * 2 * n_blocks, LANES), F32),
        compiler_params=_params(1),
        name="experts",
    )(src, e_a, e_b, used, x1s, w1, w3, w2, w1, w3, w2)


def _combine_kernel(pos_ref, x_ref, route_ref, lng_ref, lnb_ref, ys_ref, o_ref, buf_ref, sem_ref,
                    *, tm, n_tiles, n_blocks, alpha):
    slot = _gather_pipeline(pos_ref, ys_ref, buf_ref, sem_ref, tm, n_tiles, 2 * n_blocks)
    ys = buf_ref.at[slot]
    route = route_ref[...]
    moe = (route[:, ROUTE_WA:ROUTE_WA + 1] * _from_slabs(ys, tm, n_blocks, 2 * n_blocks)
           + route[:, ROUTE_WB:ROUTE_WB + 1] * _from_slabs(ys, tm, n_blocks, 2 * n_blocks,
                                                           offset=n_blocks))
    x1 = _from_slabs(x_ref, tm, n_blocks, n_blocks)
    o_ref[...] = _layer_norm(alpha * x1 + moe, lng_ref[...], lnb_ref[...])


def _combine(x1s, route, ys, pos, ln_g, ln_b, *, alpha, tm):
    t = route.shape[0]
    d = ln_g.shape[1]
    n_blocks = d // LANES
    n_tiles = t // tm
    return pl.pallas_call(
        functools.partial(_combine_kernel, tm=tm, n_tiles=n_tiles, n_blocks=n_blocks, alpha=alpha),
        grid_spec=pltpu.PrefetchScalarGridSpec(
            num_scalar_prefetch=1, grid=(n_tiles,),
            in_specs=[pl.BlockSpec((tm * n_blocks, LANES), lambda i, pos: (i, 0)),
                      pl.BlockSpec((tm, LANES), lambda i, pos: (i, 0)),
                      pl.BlockSpec((1, d), lambda i, pos: (0, 0)),
                      pl.BlockSpec((1, d), lambda i, pos: (0, 0)),
                      pl.BlockSpec(memory_space=pl.ANY)],
            out_specs=pl.BlockSpec((tm, d), lambda i, pos: (i, 0)),
            scratch_shapes=[pltpu.VMEM((2, tm * 2 * n_blocks, LANES), F32),
                            pltpu.SemaphoreType.DMA((2,))]),
        out_shape=jax.ShapeDtypeStruct((t, d), F32),
        compiler_params=_params(1),
        name="moe_combine_ln2",
    )(pos, x1s, route, ln_g, ln_b, ys)


def _moe(x1s, logits, w1, w3, w2, ln_g, ln_b, *, alpha, tm):
    t = logits.shape[0]
    tm = min(tm, t)
    key_shift = max(t - 1, 1).bit_length()
    assert N_CLASSES << key_shift < 1 << 24
    route, key, counts = _route(logits, tm=tm, key_shift=key_shift)
    pos, src, e_a, e_b, used, n_tiles = _plan(key.reshape(t), counts.reshape(LANES), tm=tm,
                                              key_shift=key_shift)
    ys = _experts(x1s, src, e_a, e_b, used, w1, w3, w2, tm=tm, n_tiles=n_tiles)
    return _combine(x1s, route, ys, pos, ln_g, ln_b, alpha=alpha, tm=tm)


def kernel(x, mem, w_in, b_gate, w_mem_kv, da_lambda, da_norm_g, ret_gn_g, ret_gn_b, w_branch, w_o,
           ln1_g, ln1_b, w_rg, b_rg, w_re, b_re, w1, w3, w2, ln2_g, ln2_b):
    batch, seq, d = x.shape
    mem_len = mem.shape[1]
    depth = w_in.shape[0]
    alpha = (2.0 * depth) ** 0.25
    t = batch * seq
    x2 = x.reshape(t, d)
    for l in range(depth):
        lambda_init = 0.8 - 0.6 * math.exp(-0.3 * l)
        proj = _matmul(x2, w_in[l].astype(BF16), tm=2048, tn=1024, name="in_proj")
        mem_kv = _matmul(mem.reshape(batch * mem_len, d), w_mem_kv[l].astype(BF16),
                         tm=1024, tn=1024, name="mem_kv_proj")
        y_da = _diff_attention(proj, da_lambda[l], da_norm_g[l], batch=batch, seq=seq, d=d,
                               lambda_init=lambda_init, tile=256)
        y_ret = _retention(proj, ret_gn_g[l], ret_gn_b[l], batch=batch, seq=seq, d=d)
        y_mem = _memory_attention(proj, mem_kv, batch=batch, seq=seq, mem_len=mem_len, d=d, tq=512)
        n_route = N_GROUPS + N_EXPERTS
        wr = jnp.pad(jnp.concatenate([w_rg[l], w_re[l]], axis=1), ((0, 0), (0, LANES - n_route)))
        br = jnp.pad(jnp.concatenate([b_rg[l], b_re[l]]), (0, LANES - n_route)).reshape(1, LANES)
        x1, logits = _merge(x2, y_da, y_ret, y_mem, proj, b_gate[l].reshape(N_BRANCH, d),
                            w_branch[l].astype(BF16), w_o[l].astype(BF16),
                            ln1_g[l].reshape(1, d), ln1_b[l].reshape(1, d), wr, br,
                            alpha=alpha, tm=512)
        x2 = _moe(x1, logits, w1[l].astype(BF16), w3[l].astype(BF16), w2[l].astype(BF16),
                  ln2_g[l].reshape(1, d), ln2_b[l].reshape(1, d), alpha=alpha, tm=256)
    return x2.reshape(batch, seq, d)
```

```python
import functools
import math

import jax
import jax.numpy as jnp
import numpy as np
from jax import lax
from jax.experimental import pallas as pl
from jax.experimental.pallas import tpu as pltpu

BF16 = jnp.bfloat16
F32 = jnp.float32

DA_HEADS = 8
RET_HEADS = 4
RET_CHUNK = 128
MEM_HEADS = 4
N_BRANCH = 3
N_GROUPS = 4
EXPERTS_PER_GROUP = 4
N_EXPERTS = N_GROUPS * EXPERTS_PER_GROUP
EPS = 1e-5
LANES = 128
VMEM_LIMIT = 56 * 1024 * 1024

_NT = (((1,), (1,)), ((), ()))


def _params(n_axes):
    return pltpu.CompilerParams(dimension_semantics=("arbitrary",) * n_axes,
                                vmem_limit_bytes=VMEM_LIMIT)


def _layer_norm(z, g, b):
    mu = jnp.mean(z, axis=-1, keepdims=True)
    zc = z - mu
    var = jnp.mean(zc * zc, axis=-1, keepdims=True)
    return zc * lax.rsqrt(var + EPS) * g + b


def _matmul_kernel(x_ref, w_ref, o_ref):
    o_ref[...] = jnp.dot(x_ref[...].astype(BF16), w_ref[...],
                         preferred_element_type=F32).astype(o_ref.dtype)


def _matmul(x, w, *, tm, tn, name):
    m, k = x.shape
    n = w.shape[1]
    tm, tn = min(tm, m), min(tn, n)
    return pl.pallas_call(
        _matmul_kernel,
        grid=(m // tm, n // tn),
        in_specs=[pl.BlockSpec((tm, k), lambda i, j: (i, 0)),
                  pl.BlockSpec((k, tn), lambda i, j: (0, j))],
        out_specs=pl.BlockSpec((tm, tn), lambda i, j: (i, j)),
        out_shape=jax.ShapeDtypeStruct((m, n), BF16),
        compiler_params=_params(2),
        name=name,
    )(x, w)


DA_ONES_ROWS = 16
DA_MASKED = -1e30
DA_CHUNK_ROWS = 32
LOG2_E = 1.4426950408889634


def _da_schedule(n_blk):
    full = [(qi, p) for qi in range(n_blk) for p in range(qi // 2)]
    order = full + [(qi, qi // 2) for qi in range(n_blk)]
    order.append(order[-1])
    table = np.asarray(order, np.int32)
    return len(full), table[:, 0], table[:, 1]


def _da_kernel(sq_ref, sp_ref, slope_ref, lam_ref, g_ref, q_ref, k_ref, v_ref, o_ref,
               ka0_ref, ka1_ref, qa0_ref, qa1_ref, vt_ref, s_ref, acc0_ref, acc1_ref, m0_ref, m1_ref,
               *, tile, dk, n_blk, n_full, lambda_init):
    h = pl.program_id(1)
    a_log2 = slope_ref[h] * LOG2_E
    w = 2 * dk
    ka_refs = (ka0_ref, ka1_ref)
    qa_refs = (qa0_ref, qa1_ref)
    acc_refs = (acc0_ref, acc1_ref)
    m_refs = (m0_ref, m1_ref)
    lane = lax.broadcasted_iota(jnp.int32, (tile, w), 1)
    pos = lax.broadcasted_iota(jnp.int32, (tile, w), 0).astype(F32)
    pieces = []
    rest = jnp.full((tile, w), a_log2, F32)
    for _ in range(3):
        piece = rest.astype(BF16).astype(F32)
        pieces.append(piece)
        rest = rest - piece
    in_comp, k_bias, q_bias = [], [], []
    for c in range(2):
        free = dk * (1 - c)
        in_comp.append((lane >= c * dk) & (lane < (c + 1) * dk))
        kb = jnp.where((lane >= free) & (lane < free + 3), pos, 0.0)
        qb = jnp.where((lane >= free + 3) & (lane < free + 6), -pos, 0.0)
        for i in range(3):
            kb = jnp.where(lane == free + 3 + i, pieces[i], kb)
            qb = jnp.where(lane == free + i, pieces[i], qb)
        k_bias.append(kb)
        q_bias.append(qb)

    def prepare(pp, _):
        for jj in range(2):
            blk = 2 * pp + jj
            rows = pl.ds(pl.multiple_of(blk * tile, tile), tile)
            kf = k_ref[rows, :].astype(F32)
            qf = q_ref[rows, :].astype(F32) * (dk ** -0.5 * LOG2_E)
            for c in range(2):
                ka_refs[c][rows, :] = jnp.where(in_comp[c], kf, k_bias[c]).astype(BF16)
                qa_refs[c][rows, :] = jnp.where(in_comp[c], qf, q_bias[c]).astype(BF16)
                acc_refs[c][blk] = jnp.zeros((w + DA_ONES_ROWS, tile), F32)
                m_refs[c][blk] = jnp.full((1, tile), -jnp.inf, F32)
            vt_ref[pp, 0:w, jj * tile:(jj + 1) * tile] = v_ref[rows, :].astype(F32).T.astype(BF16)
        vt_ref[pp, w:w + DA_ONES_ROWS, :] = jnp.ones((DA_ONES_ROWS, 2 * tile), BF16)
        return 0

    lax.fori_loop(0, n_blk // 2, prepare, 0)

    lv = lam_ref[...]
    lam = (jnp.exp(jnp.sum(lv[0:1] * lv[1:2], axis=1, keepdims=True))
           - jnp.exp(jnp.sum(lv[2:3] * lv[3:4], axis=1, keepdims=True)) + lambda_init)
    chunk = DA_CHUNK_ROWS
    n_chunks = tile // chunk
    rel0 = (lax.broadcasted_iota(jnp.int32, (chunk, tile), 0)
            - lax.broadcasted_iota(jnp.int32, (chunk, tile), 1))

    def issue_scores(t, slot, n_kv):
        qi, p = sq_ref[t], sp_ref[t]
        q_rows = pl.ds(pl.multiple_of(qi * tile, tile), tile)
        for jj in range(n_kv):
            k_rows = pl.ds(pl.multiple_of((2 * p + jj) * tile, tile), tile)
            for c in range(2):
                s_ref[slot, 2 * jj + c] = lax.dot_general(
                    ka_refs[c][k_rows, :], qa_refs[c][q_rows, :], _NT, preferred_element_type=F32)

    def consume(t, slot, diag):
        n_kv = 1 if diag == 0 else 2
        qi, p = sq_ref[t], sp_ref[t]
        offs = [a_log2 * ((2 * p + jj - qi) * tile).astype(F32) for jj in range(n_kv)]
        accs = []
        for c in range(2):
            tiles = [s_ref.at[slot, 2 * jj + c] for jj in range(n_kv)]
            m_old = m_refs[c][qi]
            m_new = m_old
            for jj in range(n_kv):
                col_max = None
                for r in range(n_chunks):
                    rows = slice(r * chunk, (r + 1) * chunk)
                    x = tiles[jj][rows, :]
                    if jj == diag:
                        x = jnp.where(rel0 <= -r * chunk, x, DA_MASKED)
                        tiles[jj][rows, :] = x
                    col_max = x if col_max is None else jnp.maximum(col_max, x)
                m_new = jnp.maximum(m_new, jnp.max(col_max, axis=0, keepdims=True) + offs[jj])
            pts = []
            for jj in range(n_kv):
                shift = m_new - offs[jj]
                for r in range(n_chunks):
                    x = tiles[jj][r * chunk:(r + 1) * chunk, :]
                    pts.append(jnp.exp2(x - shift).astype(BF16))
            vt = vt_ref[p] if n_kv == 2 else vt_ref[p, :, 0:tile]
            pv = jnp.dot(vt, jnp.concatenate(pts, axis=0), preferred_element_type=F32)
            acc = jnp.exp2(m_old - m_new) * acc_refs[c][qi] + pv
            if diag is None:
                acc_refs[c][qi] = acc
                m_refs[c][qi] = m_new
            else:
                accs.append(acc)
        if diag is not None:
            a0, a1 = accs
            ot = a0[0:w] / a0[w:w + 1] - lam * (a1[0:w] / a1[w:w + 1])
            o = ot.T
            o = o * lax.rsqrt(jnp.mean(o * o, axis=1, keepdims=True) + EPS) * g_ref[...]
            q_rows = pl.ds(pl.multiple_of(qi * tile, tile), tile)
            o_ref[q_rows, :] = (o * (1.0 - lambda_init)).astype(o_ref.dtype)

    def two_entries(i, base, last):
        t = base + 2 * i
        issue_scores(t + 1, 1, 2)
        consume(t, 0, 0 if last else None)
        issue_scores(t + 2, 0, 1 if last else 2)
        consume(t + 1, 1, 1 if last else None)
        return 0

    issue_scores(0, 0, 2 if n_full else 1)
    lax.fori_loop(0, n_full // 2, lambda i, _: two_entries(i, 0, False), 0)
    lax.fori_loop(0, n_blk // 2, lambda i, _: two_entries(i, n_full, True), 0)


def _diff_attention(proj, lam_vecs, norm_g, *, batch, seq, d, lambda_init, tile):
    dk = d // (2 * DA_HEADS)
    w = 2 * dk
    tile = min(tile, seq)
    assert tile <= 256 and seq % (2 * tile) == 0
    n_blk = seq // tile
    hpb = d // w
    slopes = 2.0 ** (-8.0 * (jnp.arange(DA_HEADS, dtype=F32) + 1.0) / DA_HEADS)
    n_full, sched_q, sched_p = _da_schedule(n_blk)
    kern = functools.partial(_da_kernel, tile=tile, dk=dk, n_blk=n_blk, n_full=n_full,
                             lambda_init=lambda_init)
    smem = pl.BlockSpec(memory_space=pltpu.SMEM)
    return pl.pallas_call(
        kern,
        grid=(batch, DA_HEADS),
        in_specs=[smem, smem, smem,
                  pl.BlockSpec((4, dk), lambda b, h: (0, 0)),
                  pl.BlockSpec((1, w), lambda b, h: (0, 0)),
                  pl.BlockSpec((seq, w), lambda b, h: (b, h)),
                  pl.BlockSpec((seq, w), lambda b, h: (b, hpb + h)),
                  pl.BlockSpec((seq, w), lambda b, h: (b, 2 * hpb + h))],
        out_specs=pl.BlockSpec((seq, w), lambda b, h: (b, h)),
        out_shape=jax.ShapeDtypeStruct((batch * seq, d), BF16),
        scratch_shapes=[pltpu.VMEM((seq, w), BF16), pltpu.VMEM((seq, w), BF16),
                        pltpu.VMEM((seq, w), BF16), pltpu.VMEM((seq, w), BF16),
                        pltpu.VMEM((n_blk // 2, w + DA_ONES_ROWS, 2 * tile), BF16),
                        pltpu.VMEM((2, 4, tile, tile), F32),
                        pltpu.VMEM((n_blk, w + DA_ONES_ROWS, tile), F32),
                        pltpu.VMEM((n_blk, w + DA_ONES_ROWS, tile), F32),
                        pltpu.VMEM((n_blk, 1, tile), F32),
                        pltpu.VMEM((n_blk, 1, tile), F32)],
        compiler_params=_params(2),
        name="diff_attention",
    )(jnp.asarray(sched_q), jnp.asarray(sched_p), slopes, lam_vecs, norm_g.reshape(1, w),
      proj, proj, proj)


def _ret_kernel(cd_ref, intra_ref, qd_ref, kd_ref, gng_ref, gnb_ref,
                q_ref, k_ref, v_ref, g_ref, o_ref, state_ref, *, n_chunks, dk, dv):
    c_len = RET_CHUNK
    heads = range(RET_HEADS)
    state_ref[...] = jnp.zeros_like(state_ref)

    def chunk(c, _):
        rows = pl.ds(pl.multiple_of(c * c_len, c_len), c_len)
        q = [q_ref[rows, h * dk:(h + 1) * dk] for h in heads]
        k = [k_ref[rows, h * dk:(h + 1) * dk] for h in heads]
        v = [v_ref[rows, h * dv:(h + 1) * dv] for h in heads]
        state = [state_ref[h] for h in heads]
        s = [lax.dot_general(q[h], k[h], _NT, preferred_element_type=F32) for h in heads]
        cross = [jnp.dot(q[h], state[h].astype(BF16), preferred_element_type=F32) for h in heads]
        kv = [jnp.dot((k[h].astype(F32) * kd_ref[h]).T.astype(BF16), v[h],
                      preferred_element_type=F32) for h in heads]
        o = [jnp.dot((s[h] * intra_ref[h]).astype(BF16), v[h], preferred_element_type=F32)
             + cross[h] * qd_ref[h] for h in heads]
        for h in heads:
            state_ref[h] = state[h] * cd_ref[h] + kv[h]
            cols = slice(h * dv, (h + 1) * dv)
            mu = jnp.mean(o[h], axis=1, keepdims=True)
            oc = o[h] - mu
            var = jnp.mean(oc * oc, axis=1, keepdims=True)
            y = oc * lax.rsqrt(var + EPS) * gng_ref[:, cols] + gnb_ref[:, cols]
            gate = g_ref[rows, cols].astype(F32)
            y = gate * (1.0 / (1.0 + jnp.exp(-gate))) * y
            o_ref[rows, cols] = y.astype(o_ref.dtype)
        return 0

    lax.fori_loop(0, n_chunks, chunk, 0)


def _retention(proj, gn_g, gn_b, *, batch, seq, d):
    dv = d // RET_HEADS
    dk = dv // 2
    c_len = RET_CHUNK
    n_chunks = seq // c_len
    log_g = jnp.log(1.0 - 2.0 ** (-5.0 - jnp.arange(RET_HEADS, dtype=F32)))
    idx = jnp.arange(c_len, dtype=F32)
    rel = idx[:, None] - idx[None, :]
    intra = jnp.where(rel[None] >= 0, jnp.exp(rel[None] * log_g[:, None, None]), 0.0) * dk ** -0.5
    q_decay = jnp.exp((idx[None, :] + 1.0) * log_g[:, None])[:, :, None]
    k_decay = (jnp.exp((c_len - 1.0 - idx[None, :]) * log_g[:, None]) * dk ** -0.5)[:, :, None]
    chunk_decay = jnp.exp(c_len * log_g)
    hk = RET_HEADS * dk
    q_blk = 3 * d // hk
    k_blk = q_blk + 1
    v_blk = (3 * d + 2 * hk) // d
    g_blk = v_blk + 1
    kern = functools.partial(_ret_kernel, n_chunks=n_chunks, dk=dk, dv=dv)
    const3 = lambda b: (0, 0, 0)
    return pl.pallas_call(
        kern,
        grid=(batch,),
        in_specs=[pl.BlockSpec(memory_space=pltpu.SMEM),
                  pl.BlockSpec((RET_HEADS, c_len, c_len), const3),
                  pl.BlockSpec((RET_HEADS, c_len, 1), const3),
                  pl.BlockSpec((RET_HEADS, c_len, 1), const3),
                  pl.BlockSpec((1, d), lambda b: (0, 0)),
                  pl.BlockSpec((1, d), lambda b: (0, 0)),
                  pl.BlockSpec((seq, hk), lambda b: (b, q_blk)),
                  pl.BlockSpec((seq, hk), lambda b: (b, k_blk)),
                  pl.BlockSpec((seq, d), lambda b: (b, v_blk)),
                  pl.BlockSpec((seq, d), lambda b: (b, g_blk))],
        out_specs=pl.BlockSpec((seq, d), lambda b: (b, 0)),
        out_shape=jax.ShapeDtypeStruct((batch * seq, d), BF16),
        scratch_shapes=[pltpu.VMEM((RET_HEADS, dk, dv), F32)],
        compiler_params=_params(1),
        name="retention",
    )(chunk_decay, intra, q_decay, k_decay, gn_g.reshape(1, d), gn_b.reshape(1, d),
      proj, proj, proj, proj)


def _mem_kernel(q_ref, kv_ref, o_ref, *, d, dh):
    for h in range(MEM_HEADS):
        q = q_ref[:, h * dh:(h + 1) * dh]
        k = kv_ref[:, h * dh:(h + 1) * dh]
        v = kv_ref[:, d + h * dh:d + (h + 1) * dh]
        s = lax.dot_general(q, k, _NT, preferred_element_type=F32) * dh ** -0.5
        p = jnp.exp(s - jnp.max(s, axis=1, keepdims=True))
        l = jnp.sum(p, axis=1, keepdims=True)
        o = jnp.dot(p.astype(BF16), v, preferred_element_type=F32) / l
        o_ref[:, h * dh:(h + 1) * dh] = o.astype(o_ref.dtype)


def _memory_attention(proj, mem_kv, *, batch, seq, mem_len, d, tq):
    dh = d // MEM_HEADS
    tq = min(tq, seq)
    nq = seq // tq
    q_blk = 6
    kern = functools.partial(_mem_kernel, d=d, dh=dh)
    return pl.pallas_call(
        kern,
        grid=(batch, nq),
        in_specs=[pl.BlockSpec((tq, d), lambda b, i: (b * nq + i, q_blk)),
                  pl.BlockSpec((mem_len, 2 * d), lambda b, i: (b, 0))],
        out_specs=pl.BlockSpec((tq, d), lambda b, i: (b * nq + i, 0)),
        out_shape=jax.ShapeDtypeStruct((batch * seq, d), BF16),
        compiler_params=_params(2),
        name="memory_attention",
    )(proj, mem_kv)


def _merge_kernel(x_ref, yda_ref, yret_ref, ymem_ref, g0_ref, g1_ref, g2_ref, bg_ref,
                  wb_ref, wo_ref, lng_ref, lnb_ref, wrh_ref, wrl_ref, br_ref, x1_ref, lg_ref,
                  *, alpha):
    ys = (yda_ref, yret_ref, ymem_ref)
    gs = (g0_ref, g1_ref, g2_ref)
    merged = None
    for n in range(N_BRANCH):
        gate = gs[n][...].astype(F32) + bg_ref[n:n + 1, :]
        gate = 1.0 / (1.0 + jnp.exp(-gate))
        term = gate * jnp.dot(ys[n][...], wb_ref[n], preferred_element_type=F32)
        merged = term if merged is None else merged + term
    out = jnp.dot(merged.astype(BF16), wo_ref[...], preferred_element_type=F32)
    x1 = _layer_norm(alpha * x_ref[...] + out, lng_ref[...], lnb_ref[...])
    tm, n_blocks = x1.shape[0], x1.shape[1] // LANES
    _to_slabs(x1_ref, x1, tm, _slab_pitch(n_blocks))
    _zero_slab_padding(x1_ref, tm, _slab_pitch(n_blocks), n_blocks)
    x_hi = x1.astype(BF16)
    x_lo = (x1 - x_hi.astype(F32)).astype(BF16)
    lg_ref[...] = (jnp.dot(x_hi, wrh_ref[...], preferred_element_type=F32)
                   + jnp.dot(x_lo, wrh_ref[...], preferred_element_type=F32)
                   + jnp.dot(x_hi, wrl_ref[...], preferred_element_type=F32) + br_ref[...])


def _merge(x2, y_da, y_ret, y_mem, proj, b_gate, wb, wo, ln_g, ln_b, wr, br, *, alpha, tm):
    t, d = x2.shape
    tm = min(tm, t)
    gate_blk = 7
    row = lambda i: (i, 0)
    const2 = lambda i: (0, 0)
    once = pl.Buffered(1)
    x_pitch = _slab_pitch(d // LANES)
    wr_hi = wr.astype(BF16)
    wr_lo = (wr - wr_hi.astype(F32)).astype(BF16)
    kern = functools.partial(_merge_kernel, alpha=alpha)
    return pl.pallas_call(
        kern,
        grid=(t // tm,),
        in_specs=[pl.BlockSpec((tm, d), row), pl.BlockSpec((tm, d), row),
                  pl.BlockSpec((tm, d), row), pl.BlockSpec((tm, d), row),
                  pl.BlockSpec((tm, d), lambda i: (i, gate_blk)),
                  pl.BlockSpec((tm, d), lambda i: (i, gate_blk + 1)),
                  pl.BlockSpec((tm, d), lambda i: (i, gate_blk + 2)),
                  pl.BlockSpec((N_BRANCH, d), const2),
                  pl.BlockSpec((N_BRANCH, d, d), lambda i: (0, 0, 0), pipeline_mode=once),
                  pl.BlockSpec((d, d), const2, pipeline_mode=once),
                  pl.BlockSpec((1, d), const2), pl.BlockSpec((1, d), const2),
                  pl.BlockSpec((d, LANES), const2), pl.BlockSpec((d, LANES), const2),
                  pl.BlockSpec((1, LANES), const2)],
        out_specs=[pl.BlockSpec((tm * x_pitch, LANES), row), pl.BlockSpec((tm, LANES), row)],
        out_shape=[jax.ShapeDtypeStruct((t * x_pitch, LANES), F32),
                   jax.ShapeDtypeStruct((t, LANES), F32)],
        compiler_params=_params(1),
        name="merge_ln1_router",
    )(x2, y_da, y_ret, y_mem, proj, proj, proj, b_gate, wb, wo, ln_g, ln_b, wr_hi, wr_lo, br)


PAIRS = ((0, 1), (0, 2), (0, 3), (1, 2), (1, 3), (2, 3))
N_CLASSES = N_GROUPS * len(PAIRS)
ROUTE_CLS, ROUTE_RANK, ROUTE_WA, ROUTE_WB = 0, 1, 2, 3


def _route_kernel(lg_ref, tril_ref, route_ref, key_ref, cnt_ref, carry_ref, *, key_shift):
    i = pl.program_id(0)

    @pl.when(i == 0)
    def _():
        carry_ref[...] = jnp.zeros_like(carry_ref)

    lg = lg_ref[...]
    tm = lg.shape[0]
    col = lax.broadcasted_iota(jnp.int32, lg.shape, 1).astype(F32)
    big = float(LANES)
    ninf = -jnp.inf

    def first_argmax(vals, vmax):
        return jnp.min(jnp.where(vals == vmax, col, big), axis=1, keepdims=True)

    gl = jnp.where(col < N_GROUPS, lg, ninf)
    gmax = jnp.max(gl, axis=1, keepdims=True)
    p_top = 1.0 / jnp.sum(jnp.exp(gl - gmax), axis=1, keepdims=True)
    g_idx = first_argmax(gl, gmax)
    lo = N_GROUPS + g_idx * EXPERTS_PER_GROUP
    el = jnp.where((col >= lo) & (col < lo + EXPERTS_PER_GROUP), lg, ninf)
    e1 = jnp.max(el, axis=1, keepdims=True)
    i1 = first_argmax(el, e1)
    el2 = jnp.where(col == i1, ninf, el)
    e2 = jnp.max(el2, axis=1, keepdims=True)
    i2 = first_argmax(el2, e2)
    r = jnp.exp(e2 - e1)
    w1 = p_top / (1.0 + r)
    w2 = p_top * r / (1.0 + r)

    a_loc = jnp.minimum(i1, i2) - lo
    b_loc = jnp.maximum(i1, i2) - lo
    pair = jnp.where(a_loc == 0.0, b_loc - 1.0, jnp.where(a_loc == 1.0, b_loc + 1.0, 5.0))
    cls = g_idx * float(len(PAIRS)) + pair
    top_is_a = i1 < i2
    w_a = jnp.where(top_is_a, w1, w2)
    w_b = jnp.where(top_is_a, w2, w1)

    onehot = jnp.where(col == cls, 1.0, 0.0)
    prefix = carry_ref[...] + jnp.dot(tril_ref[...], onehot.astype(BF16), preferred_element_type=F32)
    rank = jnp.sum(onehot * prefix, axis=1, keepdims=True) - 1.0
    carry_ref[...] = prefix[tm - 1:tm, :]
    cnt_ref[...] = prefix[tm - 1:tm, :].astype(jnp.int32)
    route_ref[...] = jnp.where(col == ROUTE_CLS, cls,
                               jnp.where(col == ROUTE_RANK, rank,
                                         jnp.where(col == ROUTE_WA, w_a,
                                                   jnp.where(col == ROUTE_WB, w_b, 0.0))))
    key = jnp.where(col == 0.0, cls * float(1 << key_shift) + rank, 0.0).T
    key_ref[...] = key[0:1, :].astype(jnp.int32)


def _route(logits, *, tm, key_shift):
    t = logits.shape[0]
    tm = min(tm, t)
    tril = jnp.tril(jnp.ones((tm, tm), BF16))
    return pl.pallas_call(
        functools.partial(_route_kernel, key_shift=key_shift),
        grid=(t // tm,),
        in_specs=[pl.BlockSpec((tm, LANES), lambda i: (i, 0)),
                  pl.BlockSpec((tm, tm), lambda i: (0, 0))],
        out_specs=[pl.BlockSpec((tm, LANES), lambda i: (i, 0)),
                   pl.BlockSpec((1, tm), lambda i: (0, i)),
                   pl.BlockSpec((1, LANES), lambda i: (0, 0))],
        out_shape=[jax.ShapeDtypeStruct((t, LANES), F32),
                   jax.ShapeDtypeStruct((1, t), jnp.int32),
                   jax.ShapeDtypeStruct((1, LANES), jnp.int32)],
        scratch_shapes=[pltpu.VMEM((1, LANES), F32)],
        compiler_params=_params(1),
        name="routing",
    )(logits, tril)


def _plan_kernel(key_ref, cnt_ref, pos_ref, src_ref, ea_ref, eb_ref, used_ref, starts_ref,
                 *, n_tokens, n_tiles, tm, key_shift):
    tm_shift = tm.bit_length() - 1
    row0 = jnp.int32(0)
    tile0 = jnp.int32(0)
    last_a = jnp.int32(0)
    last_b = jnp.int32(1)
    for c in range(N_CLASSES):
        e_a = c // len(PAIRS) * EXPERTS_PER_GROUP + PAIRS[c % len(PAIRS)][0]
        e_b = c // len(PAIRS) * EXPERTS_PER_GROUP + PAIRS[c % len(PAIRS)][1]
        cnt = cnt_ref[c]
        n_cls_tiles = lax.shift_right_logical(cnt + (tm - 1), tm_shift)
        starts_ref[c] = row0

        def mark(i, _, e_a=e_a, e_b=e_b):
            ea_ref[i] = e_a
            eb_ref[i] = e_b
            used_ref[i] = 1
            return 0

        lax.fori_loop(tile0, tile0 + n_cls_tiles, mark, 0)
        has = n_cls_tiles > 0
        last_a = jnp.where(has, e_a, last_a)
        last_b = jnp.where(has, e_b, last_b)
        row_end = row0 + n_cls_tiles * tm

        def pad(r, _):
            src_ref[r] = 0
            return 0

        lax.fori_loop(row0 + cnt, row_end, pad, 0)
        row0 = row_end
        tile0 = tile0 + n_cls_tiles

    def unused(i, _):
        ea_ref[i] = last_a
        eb_ref[i] = last_b
        used_ref[i] = 0
        return 0

    lax.fori_loop(tile0, n_tiles, unused, 0)
    lax.fori_loop(row0, n_tiles * tm, pad, 0)

    def put(t, _):
        key = key_ref[t]
        p = starts_ref[lax.shift_right_logical(key, key_shift)] + (key & ((1 << key_shift) - 1))
        pos_ref[t] = p
        src_ref[p] = t
        return 0

    lax.fori_loop(0, n_tokens, put, 0, unroll=8)


def _plan(key, counts, *, tm, key_shift):
    n_tokens = key.shape[0]
    assert tm & (tm - 1) == 0
    n_tiles = n_tokens // tm + N_CLASSES
    smem = pl.BlockSpec(memory_space=pltpu.SMEM)
    tiles = jax.ShapeDtypeStruct((n_tiles,), jnp.int32)
    pos, src, e_a, e_b, used = pl.pallas_call(
        functools.partial(_plan_kernel, n_tokens=n_tokens, n_tiles=n_tiles, tm=tm,
                          key_shift=key_shift),
        in_specs=[smem, smem], out_specs=[smem] * 5,
        out_shape=[jax.ShapeDtypeStruct((n_tokens,), jnp.int32),
                   jax.ShapeDtypeStruct((n_tiles * tm,), jnp.int32), tiles, tiles, tiles],
        scratch_shapes=[pltpu.SMEM((N_CLASSES,), jnp.int32)],
        name="moe_plan",
    )(key, counts)
    return pos, src, e_a, e_b, used, n_tiles


def _slab_pitch(n_blocks):
    return n_blocks + 4 if n_blocks % 8 == 0 else n_blocks


def _to_slabs(ref, value, n_rows, pitch, offset=0):
    for j in range(value.shape[1] // LANES):
        ref[pl.ds(offset + j, n_rows, stride=pitch), :] = value[:, j * LANES:(j + 1) * LANES]


def _zero_slab_padding(ref, n_rows, pitch, n_blocks):
    for j in range(n_blocks, pitch):
        ref[pl.ds(j, n_rows, stride=pitch), :] = jnp.zeros((n_rows, LANES), ref.dtype)


def _from_slabs(ref, n_rows, n_blocks, pitch, offset=0):
    return jnp.concatenate([ref[pl.ds(offset + j, n_rows, stride=pitch), :]
                            for j in range(n_blocks)], axis=1)


def _gather_copy(tab_ref, buf_ref, sem, row, r, rec, pitch):
    return pltpu.make_async_copy(tab_ref.at[pl.ds(pl.multiple_of(row * pitch, 4), rec), :],
                                 buf_ref.at[pl.ds(pl.multiple_of(r * pitch, 4), rec), :], sem)


def _gather_start(idx_ref, tab_ref, buf_ref, sem, tile, tm, rec, pitch):
    def issue(r, _):
        _gather_copy(tab_ref, buf_ref, sem, idx_ref[tile * tm + r], r, rec, pitch).start()
        return 0

    lax.fori_loop(0, tm, issue, 0, unroll=8)


def _gather_wait(tab_ref, buf_ref, sem, tm, rec):
    pltpu.make_async_copy(tab_ref.at[pl.ds(0, tm * rec), :], buf_ref.at[pl.ds(0, tm * rec), :],
                          sem).wait()


def _gather_pipeline(idx_ref, tab_ref, buf_ref, sem_ref, tm, n_tiles, rec, pitch):
    i = pl.program_id(0)
    slot = i & 1

    @pl.when(i == 0)
    def _():
        _gather_start(idx_ref, tab_ref, buf_ref.at[0], sem_ref.at[0], 0, tm, rec, pitch)

    @pl.when(i + 1 < n_tiles)
    def _():
        _gather_start(idx_ref, tab_ref, buf_ref.at[1 - slot], sem_ref.at[1 - slot], i + 1, tm,
                      rec, pitch)

    _gather_wait(tab_ref, buf_ref.at[slot], sem_ref.at[slot], tm, rec)
    return slot


def _experts_kernel(src_ref, ea_ref, eb_ref, used_ref, x_ref, w1a_ref, w3a_ref, w2a_ref,
                    w1b_ref, w3b_ref, w2b_ref, o_ref, buf_ref, sem_ref, *, tm, n_tiles, n_blocks):
    i = pl.program_id(0)
    slot = i & 1
    used = lambda tile: used_ref[tile] != 0
    in_flight = (i == 0) | used(jnp.maximum(i - 1, 0))
    x_pitch = _slab_pitch(n_blocks)
    y_pitch = _slab_pitch(2 * n_blocks)

    @pl.when(i == 0)
    def _():
        _gather_start(src_ref, x_ref, buf_ref.at[0], sem_ref.at[0], 0, tm, n_blocks, x_pitch)

    @pl.when(used(i))
    def _():
        _gather_wait(x_ref, buf_ref.at[slot], sem_ref.at[slot], tm, n_blocks)
        x = _from_slabs(buf_ref.at[slot], tm, n_blocks, x_pitch).astype(BF16)
        ups = [(jnp.dot(x, w1_ref[...], preferred_element_type=F32),
                jnp.dot(x, w3_ref[...], preferred_element_type=F32))
               for w1_ref, w3_ref in ((w1a_ref, w3a_ref), (w1b_ref, w3b_ref))]
        for r in range(tm):
            _gather_copy(x_ref, buf_ref.at[1 - slot], sem_ref.at[1 - slot],
                         src_ref[(i + 1) * tm + r], r, n_blocks, x_pitch).start()
        for (a, b), w2_ref, offset in zip(ups, (w2a_ref, w2b_ref), (0, n_blocks)):
            hid = a * (1.0 / (1.0 + jnp.exp(-a))) * b
            y = jnp.dot(hid.astype(BF16), w2_ref[...], preferred_element_type=F32)
            _to_slabs(o_ref, y, tm, y_pitch, offset=offset)
        _zero_slab_padding(o_ref, tm, y_pitch, 2 * n_blocks)

    @pl.when(jnp.logical_not(used(i)))
    def _():
        @pl.when(in_flight)
        def _():
            _gather_wait(x_ref, buf_ref.at[slot], sem_ref.at[slot], tm, n_blocks)

        o_ref[...] = jnp.zeros_like(o_ref)


def _experts(x1s, src, e_a, e_b, used, w1, w3, w2, *, tm, n_tiles):
    d, f = w1.shape[1], w1.shape[2]
    n_blocks = d // LANES
    x_pitch = _slab_pitch(n_blocks)
    y_pitch = _slab_pitch(2 * n_blocks)
    up_a = pl.BlockSpec((None, d, f), lambda i, src, ea, eb, used: (ea[i], 0, 0))
    up_b = pl.BlockSpec((None, d, f), lambda i, src, ea, eb, used: (eb[i], 0, 0))
    down_a = pl.BlockSpec((None, f, d), lambda i, src, ea, eb, used: (ea[i], 0, 0))
    down_b = pl.BlockSpec((None, f, d), lambda i, src, ea, eb, used: (eb[i], 0, 0))
    return pl.pallas_call(
        functools.partial(_experts_kernel, tm=tm, n_tiles=n_tiles, n_blocks=n_blocks),
        grid_spec=pltpu.PrefetchScalarGridSpec(
            num_scalar_prefetch=4, grid=(n_tiles,),
            in_specs=[pl.BlockSpec(memory_space=pl.ANY),
                      up_a, up_a, down_a, up_b, up_b, down_b],
            out_specs=pl.BlockSpec((tm * y_pitch, LANES), lambda i, src, ea, eb, used: (i, 0)),
            scratch_shapes=[pltpu.VMEM((2, tm * x_pitch, LANES), F32),
                            pltpu.SemaphoreType.DMA((2,))]),
        out_shape=jax.ShapeDtypeStruct((n_tiles * tm * y_pitch, LANES), F32),
        compiler_params=_params(1),
        name="experts",
    )(src, e_a, e_b, used, x1s, w1, w3, w2, w1, w3, w2)


def _combine_kernel(pos_ref, x_ref, route_ref, lng_ref, lnb_ref, ys_ref, o_ref, buf_ref, sem_ref,
                    *, tm, n_tiles, n_blocks, alpha):
    x_pitch = _slab_pitch(n_blocks)
    y_pitch = _slab_pitch(2 * n_blocks)
    slot = _gather_pipeline(pos_ref, ys_ref, buf_ref, sem_ref, tm, n_tiles, 2 * n_blocks, y_pitch)
    ys = buf_ref.at[slot]
    route = route_ref[...]
    moe = (route[:, ROUTE_WA:ROUTE_WA + 1] * _from_slabs(ys, tm, n_blocks, y_pitch)
           + route[:, ROUTE_WB:ROUTE_WB + 1] * _from_slabs(ys, tm, n_blocks, y_pitch,
                                                           offset=n_blocks))
    x1 = _from_slabs(x_ref, tm, n_blocks, x_pitch)
    o_ref[...] = _layer_norm(alpha * x1 + moe, lng_ref[...], lnb_ref[...])


def _combine(x1s, route, ys, pos, ln_g, ln_b, *, alpha, tm):
    t = route.shape[0]
    d = ln_g.shape[1]
    n_blocks = d // LANES
    x_pitch = _slab_pitch(n_blocks)
    y_pitch = _slab_pitch(2 * n_blocks)
    n_tiles = t // tm
    return pl.pallas_call(
        functools.partial(_combine_kernel, tm=tm, n_tiles=n_tiles, n_blocks=n_blocks, alpha=alpha),
        grid_spec=pltpu.PrefetchScalarGridSpec(
            num_scalar_prefetch=1, grid=(n_tiles,),
            in_specs=[pl.BlockSpec((tm * x_pitch, LANES), lambda i, pos: (i, 0)),
                      pl.BlockSpec((tm, LANES), lambda i, pos: (i, 0)),
                      pl.BlockSpec((1, d), lambda i, pos: (0, 0)),
                      pl.BlockSpec((1, d), lambda i, pos: (0, 0)),
                      pl.BlockSpec(memory_space=pl.ANY)],
            out_specs=pl.BlockSpec((tm, d), lambda i, pos: (i, 0)),
            scratch_shapes=[pltpu.VMEM((2, tm * y_pitch, LANES), F32),
                            pltpu.SemaphoreType.DMA((2,))]),
        out_shape=jax.ShapeDtypeStruct((t, d), F32),
        compiler_params=_params(1),
        name="moe_combine_ln2",
    )(pos, x1s, route, ln_g, ln_b, ys)


def _moe(x1s, logits, w1, w3, w2, ln_g, ln_b, *, alpha, tm):
    t = logits.shape[0]
    tm = min(tm, t)
    key_shift = max(t - 1, 1).bit_length()
    assert N_CLASSES << key_shift < 1 << 24
    route, key, counts = _route(logits, tm=tm, key_shift=key_shift)
    pos, src, e_a, e_b, used, n_tiles = _plan(key.reshape(t), counts.reshape(LANES), tm=tm,
                                              key_shift=key_shift)
    ys = _experts(x1s, src, e_a, e_b, used, w1, w3, w2, tm=tm, n_tiles=n_tiles)
    return _combine(x1s, route, ys, pos, ln_g, ln_b, alpha=alpha, tm=tm)


def kernel(x, mem, w_in, b_gate, w_mem_kv, da_lambda, da_norm_g, ret_gn_g, ret_gn_b, w_branch, w_o,
           ln1_g, ln1_b, w_rg, b_rg, w_re, b_re, w1, w3, w2, ln2_g, ln2_b):
    batch, seq, d = x.shape
    mem_len = mem.shape[1]
    depth = w_in.shape[0]
    alpha = (2.0 * depth) ** 0.25
    t = batch * seq
    x2 = x.reshape(t, d)
    for l in range(depth):
        lambda_init = 0.8 - 0.6 * math.exp(-0.3 * l)
        proj = _matmul(x2, w_in[l].astype(BF16), tm=2048, tn=1024, name="in_proj")
        mem_kv = _matmul(mem.reshape(batch * mem_len, d), w_mem_kv[l].astype(BF16),
                         tm=1024, tn=1024, name="mem_kv_proj")
        y_da = _diff_attention(proj, da_lambda[l], da_norm_g[l], batch=batch, seq=seq, d=d,
                               lambda_init=lambda_init, tile=256)
        y_ret = _retention(proj, ret_gn_g[l], ret_gn_b[l], batch=batch, seq=seq, d=d)
        y_mem = _memory_attention(proj, mem_kv, batch=batch, seq=seq, mem_len=mem_len, d=d, tq=512)
        n_route = N_GROUPS + N_EXPERTS
        wr = jnp.pad(jnp.concatenate([w_rg[l], w_re[l]], axis=1), ((0, 0), (0, LANES - n_route)))
        br = jnp.pad(jnp.concatenate([b_rg[l], b_re[l]]), (0, LANES - n_route)).reshape(1, LANES)
        x1, logits = _merge(x2, y_da, y_ret, y_mem, proj, b_gate[l].reshape(N_BRANCH, d),
                            w_branch[l].astype(BF16), w_o[l].astype(BF16),
                            ln1_g[l].reshape(1, d), ln1_b[l].reshape(1, d), wr, br,
                            alpha=alpha, tm=512)
        x2 = _moe(x1, logits, w1[l].astype(BF16), w3[l].astype(BF16), w2[l].astype(BF16),
                  ln2_g[l].reshape(1, d), ln2_b[l].reshape(1, d), alpha=alpha, tm=256)
    return x2.reshape(batch, seq, d)
```

```python
import functools
import math

import jax
import jax.numpy as jnp
import numpy as np
from jax import lax
from jax.experimental import pallas as pl
from jax.experimental.pallas import tpu as pltpu

BF16 = jnp.bfloat16
F32 = jnp.float32

DA_HEADS = 8
RET_HEADS = 4
RET_CHUNK = 128
MEM_HEADS = 4
N_BRANCH = 3
N_GROUPS = 4
EXPERTS_PER_GROUP = 4
N_EXPERTS = N_GROUPS * EXPERTS_PER_GROUP
EPS = 1e-5
LANES = 128
VMEM_LIMIT = 56 * 1024 * 1024

_NT = (((1,), (1,)), ((), ()))


def _params(n_axes):
    return pltpu.CompilerParams(dimension_semantics=("arbitrary",) * n_axes,
                                vmem_limit_bytes=VMEM_LIMIT)


def _layer_norm(z, g, b):
    mu = jnp.mean(z, axis=-1, keepdims=True)
    zc = z - mu
    var = jnp.mean(zc * zc, axis=-1, keepdims=True)
    return zc * lax.rsqrt(var + EPS) * g + b


def _matmul_kernel(x_ref, w_ref, o_ref):
    o_ref[...] = jnp.dot(x_ref[...].astype(BF16), w_ref[...],
                         preferred_element_type=F32).astype(o_ref.dtype)


def _matmul(x, w, *, tm, tn, name):
    m, k = x.shape
    n = w.shape[1]
    tm, tn = min(tm, m), min(tn, n)
    return pl.pallas_call(
        _matmul_kernel,
        grid=(m // tm, n // tn),
        in_specs=[pl.BlockSpec((tm, k), lambda i, j: (i, 0)),
                  pl.BlockSpec((k, tn), lambda i, j: (0, j))],
        out_specs=pl.BlockSpec((tm, tn), lambda i, j: (i, j)),
        out_shape=jax.ShapeDtypeStruct((m, n), BF16),
        compiler_params=_params(2),
        name=name,
    )(x, w)


DA_ONES_ROWS = 16
DA_MASKED = -1e30
DA_CHUNK_ROWS = 32
LOG2_E = 1.4426950408889634


def _da_schedule(n_blk):
    full = [(qi, p) for qi in range(n_blk) for p in range(qi // 2)]
    order = full + [(qi, qi // 2) for qi in range(n_blk)]
    order.append(order[-1])
    table = np.asarray(order, np.int32)
    return len(full), table[:, 0], table[:, 1]


def _da_kernel(sq_ref, sp_ref, slope_ref, lam_ref, g_ref, q_ref, k_ref, v_ref, o_ref,
               ka0_ref, ka1_ref, qa0_ref, qa1_ref, vt_ref, s_ref, acc0_ref, acc1_ref, m0_ref, m1_ref,
               *, tile, dk, n_blk, n_full, lambda_init):
    h = pl.program_id(1)
    a_log2 = slope_ref[h] * LOG2_E
    w = 2 * dk
    ka_refs = (ka0_ref, ka1_ref)
    qa_refs = (qa0_ref, qa1_ref)
    acc_refs = (acc0_ref, acc1_ref)
    m_refs = (m0_ref, m1_ref)
    lane = lax.broadcasted_iota(jnp.int32, (tile, w), 1)
    pos = lax.broadcasted_iota(jnp.int32, (tile, w), 0).astype(F32)
    pieces = []
    rest = jnp.full((tile, w), a_log2, F32)
    for _ in range(3):
        piece = rest.astype(BF16).astype(F32)
        pieces.append(piece)
        rest = rest - piece
    in_comp, k_bias, q_bias = [], [], []
    for c in range(2):
        free = dk * (1 - c)
        in_comp.append((lane >= c * dk) & (lane < (c + 1) * dk))
        kb = jnp.where((lane >= free) & (lane < free + 3), pos, 0.0)
        qb = jnp.where((lane >= free + 3) & (lane < free + 6), -pos, 0.0)
        for i in range(3):
            kb = jnp.where(lane == free + 3 + i, pieces[i], kb)
            qb = jnp.where(lane == free + i, pieces[i], qb)
        k_bias.append(kb)
        q_bias.append(qb)

    def prepare(pp, _):
        for jj in range(2):
            blk = 2 * pp + jj
            rows = pl.ds(pl.multiple_of(blk * tile, tile), tile)
            kf = k_ref[rows, :].astype(F32)
            qf = q_ref[rows, :].astype(F32) * (dk ** -0.5 * LOG2_E)
            for c in range(2):
                ka_refs[c][rows, :] = jnp.where(in_comp[c], kf, k_bias[c]).astype(BF16)
                qa_refs[c][rows, :] = jnp.where(in_comp[c], qf, q_bias[c]).astype(BF16)
                acc_refs[c][blk] = jnp.zeros((w + DA_ONES_ROWS, tile), F32)
                m_refs[c][blk] = jnp.full((1, tile), -jnp.inf, F32)
            vt_ref[pp, 0:w, jj * tile:(jj + 1) * tile] = v_ref[rows, :].astype(F32).T.astype(BF16)
        vt_ref[pp, w:w + DA_ONES_ROWS, :] = jnp.ones((DA_ONES_ROWS, 2 * tile), BF16)
        return 0

    lax.fori_loop(0, n_blk // 2, prepare, 0)

    lv = lam_ref[...]
    lam = (jnp.exp(jnp.sum(lv[0:1] * lv[1:2], axis=1, keepdims=True))
           - jnp.exp(jnp.sum(lv[2:3] * lv[3:4], axis=1, keepdims=True)) + lambda_init)
    chunk = DA_CHUNK_ROWS
    n_chunks = tile // chunk
    rel0 = (lax.broadcasted_iota(jnp.int32, (chunk, tile), 0)
            - lax.broadcasted_iota(jnp.int32, (chunk, tile), 1))

    def issue_scores(t, slot, n_kv):
        qi, p = sq_ref[t], sp_ref[t]
        q_rows = pl.ds(pl.multiple_of(qi * tile, tile), tile)
        for jj in range(n_kv):
            k_rows = pl.ds(pl.multiple_of((2 * p + jj) * tile, tile), tile)
            for c in range(2):
                s_ref[slot, 2 * jj + c] = lax.dot_general(
                    ka_refs[c][k_rows, :], qa_refs[c][q_rows, :], _NT, preferred_element_type=F32)

    def consume(t, slot, diag):
        n_kv = 1 if diag == 0 else 2
        qi, p = sq_ref[t], sp_ref[t]
        offs = [a_log2 * ((2 * p + jj - qi) * tile).astype(F32) for jj in range(n_kv)]
        accs = []
        for c in range(2):
            tiles = [s_ref.at[slot, 2 * jj + c] for jj in range(n_kv)]
            m_old = m_refs[c][qi]
            m_new = m_old
            for jj in range(n_kv):
                col_max = None
                for r in range(n_chunks):
                    rows = slice(r * chunk, (r + 1) * chunk)
                    x = tiles[jj][rows, :]
                    if jj == diag:
                        x = jnp.where(rel0 <= -r * chunk, x, DA_MASKED)
                        tiles[jj][rows, :] = x
                    col_max = x if col_max is None else jnp.maximum(col_max, x)
                m_new = jnp.maximum(m_new, jnp.max(col_max, axis=0, keepdims=True) + offs[jj])
            pts = []
            for jj in range(n_kv):
                shift = m_new - offs[jj]
                for r in range(n_chunks):
                    x = tiles[jj][r * chunk:(r + 1) * chunk, :]
                    pts.append(jnp.exp2(x - shift).astype(BF16))
            vt = vt_ref[p] if n_kv == 2 else vt_ref[p, :, 0:tile]
            pv = jnp.dot(vt, jnp.concatenate(pts, axis=0), preferred_element_type=F32)
            acc = jnp.exp2(m_old - m_new) * acc_refs[c][qi] + pv
            if diag is None:
                acc_refs[c][qi] = acc
                m_refs[c][qi] = m_new
            else:
                accs.append(acc)
        if diag is not None:
            a0, a1 = accs
            ot = a0[0:w] / a0[w:w + 1] - lam * (a1[0:w] / a1[w:w + 1])
            o = ot.T
            o = o * lax.rsqrt(jnp.mean(o * o, axis=1, keepdims=True) + EPS) * g_ref[...]
            q_rows = pl.ds(pl.multiple_of(qi * tile, tile), tile)
            o_ref[q_rows, :] = (o * (1.0 - lambda_init)).astype(o_ref.dtype)

    def two_entries(i, base, last):
        t = base + 2 * i
        issue_scores(t + 1, 1, 2)
        consume(t, 0, 0 if last else None)
        issue_scores(t + 2, 0, 1 if last else 2)
        consume(t + 1, 1, 1 if last else None)
        return 0

    issue_scores(0, 0, 2 if n_full else 1)
    lax.fori_loop(0, n_full // 2, lambda i, _: two_entries(i, 0, False), 0)
    lax.fori_loop(0, n_blk // 2, lambda i, _: two_entries(i, n_full, True), 0)


def _diff_attention(proj, lam_vecs, norm_g, *, batch, seq, d, lambda_init, tile):
    dk = d // (2 * DA_HEADS)
    w = 2 * dk
    tile = min(tile, seq)
    assert tile <= 256 and seq % (2 * tile) == 0
    n_blk = seq // tile
    hpb = d // w
    slopes = 2.0 ** (-8.0 * (jnp.arange(DA_HEADS, dtype=F32) + 1.0) / DA_HEADS)
    n_full, sched_q, sched_p = _da_schedule(n_blk)
    kern = functools.partial(_da_kernel, tile=tile, dk=dk, n_blk=n_blk, n_full=n_full,
                             lambda_init=lambda_init)
    smem = pl.BlockSpec(memory_space=pltpu.SMEM)
    return pl.pallas_call(
        kern,
        grid=(batch, DA_HEADS),
        in_specs=[smem, smem, smem,
                  pl.BlockSpec((4, dk), lambda b, h: (0, 0)),
                  pl.BlockSpec((1, w), lambda b, h: (0, 0)),
                  pl.BlockSpec((seq, w), lambda b, h: (b, h)),
                  pl.BlockSpec((seq, w), lambda b, h: (b, hpb + h)),
                  pl.BlockSpec((seq, w), lambda b, h: (b, 2 * hpb + h))],
        out_specs=pl.BlockSpec((seq, w), lambda b, h: (b, h)),
        out_shape=jax.ShapeDtypeStruct((batch * seq, d), BF16),
        scratch_shapes=[pltpu.VMEM((seq, w), BF16), pltpu.VMEM((seq, w), BF16),
                        pltpu.VMEM((seq, w), BF16), pltpu.VMEM((seq, w), BF16),
                        pltpu.VMEM((n_blk // 2, w + DA_ONES_ROWS, 2 * tile), BF16),
                        pltpu.VMEM((2, 4, tile, tile), F32),
                        pltpu.VMEM((n_blk, w + DA_ONES_ROWS, tile), F32),
                        pltpu.VMEM((n_blk, w + DA_ONES_ROWS, tile), F32),
                        pltpu.VMEM((n_blk, 1, tile), F32),
                        pltpu.VMEM((n_blk, 1, tile), F32)],
        compiler_params=_params(2),
        name="diff_attention",
    )(jnp.asarray(sched_q), jnp.asarray(sched_p), slopes, lam_vecs, norm_g.reshape(1, w),
      proj, proj, proj)


def _ret_kernel(cd_ref, intra_ref, qd_ref, kd_ref, gng_ref, gnb_ref,
                q_ref, k_ref, v_ref, g_ref, o_ref, state_ref, *, n_chunks, dk, dv):
    c_len = RET_CHUNK
    heads = range(RET_HEADS)
    state_ref[...] = jnp.zeros_like(state_ref)

    def chunk(c, _):
        rows = pl.ds(pl.multiple_of(c * c_len, c_len), c_len)
        q = [q_ref[rows, h * dk:(h + 1) * dk] for h in heads]
        k = [k_ref[rows, h * dk:(h + 1) * dk] for h in heads]
        v = [v_ref[rows, h * dv:(h + 1) * dv] for h in heads]
        state = [state_ref[h] for h in heads]
        s = [lax.dot_general(q[h], k[h], _NT, preferred_element_type=F32) for h in heads]
        cross = [jnp.dot(q[h], state[h].astype(BF16), preferred_element_type=F32) for h in heads]
        kv = [jnp.dot((k[h].astype(F32) * kd_ref[h]).T.astype(BF16), v[h],
                      preferred_element_type=F32) for h in heads]
        o = [jnp.dot((s[h] * intra_ref[h]).astype(BF16), v[h], preferred_element_type=F32)
             + cross[h] * qd_ref[h] for h in heads]
        for h in heads:
            state_ref[h] = state[h] * cd_ref[h] + kv[h]
            cols = slice(h * dv, (h + 1) * dv)
            mu = jnp.mean(o[h], axis=1, keepdims=True)
            oc = o[h] - mu
            var = jnp.mean(oc * oc, axis=1, keepdims=True)
            y = oc * lax.rsqrt(var + EPS) * gng_ref[:, cols] + gnb_ref[:, cols]
            gate = g_ref[rows, cols].astype(F32)
            y = gate * (1.0 / (1.0 + jnp.exp(-gate))) * y
            o_ref[rows, cols] = y.astype(o_ref.dtype)
        return 0

    lax.fori_loop(0, n_chunks, chunk, 0)


def _retention(proj, gn_g, gn_b, *, batch, seq, d):
    dv = d // RET_HEADS
    dk = dv // 2
    c_len = RET_CHUNK
    n_chunks = seq // c_len
    log_g = jnp.log(1.0 - 2.0 ** (-5.0 - jnp.arange(RET_HEADS, dtype=F32)))
    idx = jnp.arange(c_len, dtype=F32)
    rel = idx[:, None] - idx[None, :]
    intra = jnp.where(rel[None] >= 0, jnp.exp(rel[None] * log_g[:, None, None]), 0.0) * dk ** -0.5
    q_decay = jnp.exp((idx[None, :] + 1.0) * log_g[:, None])[:, :, None]
    k_decay = (jnp.exp((c_len - 1.0 - idx[None, :]) * log_g[:, None]) * dk ** -0.5)[:, :, None]
    chunk_decay = jnp.exp(c_len * log_g)
    hk = RET_HEADS * dk
    q_blk = 3 * d // hk
    k_blk = q_blk + 1
    v_blk = (3 * d + 2 * hk) // d
    g_blk = v_blk + 1
    kern = functools.partial(_ret_kernel, n_chunks=n_chunks, dk=dk, dv=dv)
    const3 = lambda b: (0, 0, 0)
    return pl.pallas_call(
        kern,
        grid=(batch,),
        in_specs=[pl.BlockSpec(memory_space=pltpu.SMEM),
                  pl.BlockSpec((RET_HEADS, c_len, c_len), const3),
                  pl.BlockSpec((RET_HEADS, c_len, 1), const3),
                  pl.BlockSpec((RET_HEADS, c_len, 1), const3),
                  pl.BlockSpec((1, d), lambda b: (0, 0)),
                  pl.BlockSpec((1, d), lambda b: (0, 0)),
                  pl.BlockSpec((seq, hk), lambda b: (b, q_blk)),
                  pl.BlockSpec((seq, hk), lambda b: (b, k_blk)),
                  pl.BlockSpec((seq, d), lambda b: (b, v_blk)),
                  pl.BlockSpec((seq, d), lambda b: (b, g_blk))],
        out_specs=pl.BlockSpec((seq, d), lambda b: (b, 0)),
        out_shape=jax.ShapeDtypeStruct((batch * seq, d), BF16),
        scratch_shapes=[pltpu.VMEM((RET_HEADS, dk, dv), F32)],
        compiler_params=_params(1),
        name="retention",
    )(chunk_decay, intra, q_decay, k_decay, gn_g.reshape(1, d), gn_b.reshape(1, d),
      proj, proj, proj, proj)


def _mem_kernel(q_ref, kv_ref, o_ref, *, d, dh):
    for h in range(MEM_HEADS):
        q = q_ref[:, h * dh:(h + 1) * dh]
        k = kv_ref[:, h * dh:(h + 1) * dh]
        v = kv_ref[:, d + h * dh:d + (h + 1) * dh]
        s = lax.dot_general(q, k, _NT, preferred_element_type=F32) * dh ** -0.5
        p = jnp.exp(s - jnp.max(s, axis=1, keepdims=True))
        l = jnp.sum(p, axis=1, keepdims=True)
        o = jnp.dot(p.astype(BF16), v, preferred_element_type=F32) / l
        o_ref[:, h * dh:(h + 1) * dh] = o.astype(o_ref.dtype)


def _memory_attention(proj, mem_kv, *, batch, seq, mem_len, d, tq):
    dh = d // MEM_HEADS
    tq = min(tq, seq)
    nq = seq // tq
    q_blk = 6
    kern = functools.partial(_mem_kernel, d=d, dh=dh)
    return pl.pallas_call(
        kern,
        grid=(batch, nq),
        in_specs=[pl.BlockSpec((tq, d), lambda b, i: (b * nq + i, q_blk)),
                  pl.BlockSpec((mem_len, 2 * d), lambda b, i: (b, 0))],
        out_specs=pl.BlockSpec((tq, d), lambda b, i: (b * nq + i, 0)),
        out_shape=jax.ShapeDtypeStruct((batch * seq, d), BF16),
        compiler_params=_params(2),
        name="memory_attention",
    )(proj, mem_kv)


def _merge_kernel(x_ref, yda_ref, yret_ref, ymem_ref, g0_ref, g1_ref, g2_ref, bg_ref,
                  wb_ref, wo_ref, lng_ref, lnb_ref, wrh_ref, wrl_ref, br_ref, x1_ref, lg_ref,
                  *, alpha):
    ys = (yda_ref, yret_ref, ymem_ref)
    gs = (g0_ref, g1_ref, g2_ref)
    merged = None
    for n in range(N_BRANCH):
        gate = gs[n][...].astype(F32) + bg_ref[n:n + 1, :]
        gate = 1.0 / (1.0 + jnp.exp(-gate))
        term = gate * jnp.dot(ys[n][...], wb_ref[n], preferred_element_type=F32)
        merged = term if merged is None else merged + term
    out = jnp.dot(merged.astype(BF16), wo_ref[...], preferred_element_type=F32)
    x1 = _layer_norm(alpha * x_ref[...] + out, lng_ref[...], lnb_ref[...])
    tm, n_blocks = x1.shape[0], x1.shape[1] // LANES
    _to_slabs(x1_ref, x1, tm, _slab_pitch(n_blocks))
    _zero_slab_padding(x1_ref, tm, _slab_pitch(n_blocks), n_blocks)
    x_hi = x1.astype(BF16)
    x_lo = (x1 - x_hi.astype(F32)).astype(BF16)
    lg_ref[...] = (jnp.dot(x_hi, wrh_ref[...], preferred_element_type=F32)
                   + jnp.dot(x_lo, wrh_ref[...], preferred_element_type=F32)
                   + jnp.dot(x_hi, wrl_ref[...], preferred_element_type=F32) + br_ref[...])


def _merge(x2, y_da, y_ret, y_mem, proj, b_gate, wb, wo, ln_g, ln_b, wr, br, *, alpha, tm):
    t, d = x2.shape
    tm = min(tm, t)
    gate_blk = 7
    row = lambda i: (i, 0)
    const2 = lambda i: (0, 0)
    once = pl.Buffered(1)
    x_pitch = _slab_pitch(d // LANES)
    wr_hi = wr.astype(BF16)
    wr_lo = (wr - wr_hi.astype(F32)).astype(BF16)
    kern = functools.partial(_merge_kernel, alpha=alpha)
    return pl.pallas_call(
        kern,
        grid=(t // tm,),
        in_specs=[pl.BlockSpec((tm, d), row), pl.BlockSpec((tm, d), row),
                  pl.BlockSpec((tm, d), row), pl.BlockSpec((tm, d), row),
                  pl.BlockSpec((tm, d), lambda i: (i, gate_blk)),
                  pl.BlockSpec((tm, d), lambda i: (i, gate_blk + 1)),
                  pl.BlockSpec((tm, d), lambda i: (i, gate_blk + 2)),
                  pl.BlockSpec((N_BRANCH, d), const2),
                  pl.BlockSpec((N_BRANCH, d, d), lambda i: (0, 0, 0), pipeline_mode=once),
                  pl.BlockSpec((d, d), const2, pipeline_mode=once),
                  pl.BlockSpec((1, d), const2), pl.BlockSpec((1, d), const2),
                  pl.BlockSpec((d, LANES), const2), pl.BlockSpec((d, LANES), const2),
                  pl.BlockSpec((1, LANES), const2)],
        out_specs=[pl.BlockSpec((tm * x_pitch, LANES), row), pl.BlockSpec((tm, LANES), row)],
        out_shape=[jax.ShapeDtypeStruct((t * x_pitch, LANES), F32),
                   jax.ShapeDtypeStruct((t, LANES), F32)],
        compiler_params=_params(1),
        name="merge_ln1_router",
    )(x2, y_da, y_ret, y_mem, proj, proj, proj, b_gate, wb, wo, ln_g, ln_b, wr_hi, wr_lo, br)


PAIRS = ((0, 1), (0, 2), (0, 3), (1, 2), (1, 3), (2, 3))
N_CLASSES = N_GROUPS * len(PAIRS)
ROUTE_CLS, ROUTE_RANK, ROUTE_WA, ROUTE_WB = 0, 1, 2, 3
EXPERT_GATHER_AHEAD = 2
COMBINE_GATHER_AHEAD = 2


def _route_kernel(lg_ref, tril_ref, route_ref, key_ref, cnt_ref, carry_ref, *, key_shift):
    i = pl.program_id(0)

    @pl.when(i == 0)
    def _():
        carry_ref[...] = jnp.zeros_like(carry_ref)

    lg = lg_ref[...]
    tm = lg.shape[0]
    col = lax.broadcasted_iota(jnp.int32, lg.shape, 1).astype(F32)
    big = float(LANES)
    ninf = -jnp.inf

    def first_argmax(vals, vmax):
        return jnp.min(jnp.where(vals == vmax, col, big), axis=1, keepdims=True)

    gl = jnp.where(col < N_GROUPS, lg, ninf)
    gmax = jnp.max(gl, axis=1, keepdims=True)
    p_top = 1.0 / jnp.sum(jnp.exp(gl - gmax), axis=1, keepdims=True)
    g_idx = first_argmax(gl, gmax)
    lo = N_GROUPS + g_idx * EXPERTS_PER_GROUP
    el = jnp.where((col >= lo) & (col < lo + EXPERTS_PER_GROUP), lg, ninf)
    e1 = jnp.max(el, axis=1, keepdims=True)
    i1 = first_argmax(el, e1)
    el2 = jnp.where(col == i1, ninf, el)
    e2 = jnp.max(el2, axis=1, keepdims=True)
    i2 = first_argmax(el2, e2)
    r = jnp.exp(e2 - e1)
    w1 = p_top / (1.0 + r)
    w2 = p_top * r / (1.0 + r)

    a_loc = jnp.minimum(i1, i2) - lo
    b_loc = jnp.maximum(i1, i2) - lo
    pair = jnp.where(a_loc == 0.0, b_loc - 1.0, jnp.where(a_loc == 1.0, b_loc + 1.0, 5.0))
    cls = g_idx * float(len(PAIRS)) + pair
    top_is_a = i1 < i2
    w_a = jnp.where(top_is_a, w1, w2)
    w_b = jnp.where(top_is_a, w2, w1)

    onehot = jnp.where(col == cls, 1.0, 0.0)
    prefix = carry_ref[...] + jnp.dot(tril_ref[...], onehot.astype(BF16), preferred_element_type=F32)
    rank = jnp.sum(onehot * prefix, axis=1, keepdims=True) - 1.0
    carry_ref[...] = prefix[tm - 1:tm, :]
    cnt_ref[...] = prefix[tm - 1:tm, :].astype(jnp.int32)
    route_ref[...] = jnp.where(col == ROUTE_CLS, cls,
                               jnp.where(col == ROUTE_RANK, rank,
                                         jnp.where(col == ROUTE_WA, w_a,
                                                   jnp.where(col == ROUTE_WB, w_b, 0.0))))
    key = jnp.where(col == 0.0, cls * float(1 << key_shift) + rank, 0.0).T
    key_ref[...] = key[0:1, :].astype(jnp.int32)


def _route(logits, *, tm, key_shift):
    t = logits.shape[0]
    tm = min(tm, t)
    tril = jnp.tril(jnp.ones((tm, tm), BF16))
    return pl.pallas_call(
        functools.partial(_route_kernel, key_shift=key_shift),
        grid=(t // tm,),
        in_specs=[pl.BlockSpec((tm, LANES), lambda i: (i, 0)),
                  pl.BlockSpec((tm, tm), lambda i: (0, 0))],
        out_specs=[pl.BlockSpec((tm, LANES), lambda i: (i, 0)),
                   pl.BlockSpec((1, tm), lambda i: (0, i)),
                   pl.BlockSpec((1, LANES), lambda i: (0, 0))],
        out_shape=[jax.ShapeDtypeStruct((t, LANES), F32),
                   jax.ShapeDtypeStruct((1, t), jnp.int32),
                   jax.ShapeDtypeStruct((1, LANES), jnp.int32)],
        scratch_shapes=[pltpu.VMEM((1, LANES), F32)],
        compiler_params=_params(1),
        name="routing",
    )(logits, tril)


def _plan_kernel(key_ref, cnt_ref, pos_ref, src_ref, ea_ref, eb_ref, used_ref, starts_ref,
                 *, n_tokens, n_tiles, tm, key_shift):
    tm_shift = tm.bit_length() - 1
    row0 = jnp.int32(0)
    tile0 = jnp.int32(0)
    last_a = jnp.int32(0)
    last_b = jnp.int32(1)
    for c in range(N_CLASSES):
        e_a = c // len(PAIRS) * EXPERTS_PER_GROUP + PAIRS[c % len(PAIRS)][0]
        e_b = c // len(PAIRS) * EXPERTS_PER_GROUP + PAIRS[c % len(PAIRS)][1]
        cnt = cnt_ref[c]
        n_cls_tiles = lax.shift_right_logical(cnt + (tm - 1), tm_shift)
        starts_ref[c] = row0

        def mark(i, _, e_a=e_a, e_b=e_b):
            ea_ref[i] = e_a
            eb_ref[i] = e_b
            used_ref[i] = 1
            return 0

        lax.fori_loop(tile0, tile0 + n_cls_tiles, mark, 0)
        has = n_cls_tiles > 0
        last_a = jnp.where(has, e_a, last_a)
        last_b = jnp.where(has, e_b, last_b)
        row_end = row0 + n_cls_tiles * tm

        def pad(r, _):
            src_ref[r] = 0
            return 0

        lax.fori_loop(row0 + cnt, row_end, pad, 0)
        row0 = row_end
        tile0 = tile0 + n_cls_tiles

    def unused(i, _):
        ea_ref[i] = last_a
        eb_ref[i] = last_b
        used_ref[i] = 0
        return 0

    lax.fori_loop(tile0, n_tiles, unused, 0)
    lax.fori_loop(row0, n_tiles * tm, pad, 0)

    def put(t, _):
        key = key_ref[t]
        p = starts_ref[lax.shift_right_logical(key, key_shift)] + (key & ((1 << key_shift) - 1))
        pos_ref[t] = p
        src_ref[p] = t
        return 0

    lax.fori_loop(0, n_tokens, put, 0, unroll=8)


def _plan(key, counts, *, tm, key_shift):
    n_tokens = key.shape[0]
    assert tm & (tm - 1) == 0
    n_tiles = n_tokens // tm + N_CLASSES - 1 + EXPERT_GATHER_AHEAD
    smem = pl.BlockSpec(memory_space=pltpu.SMEM)
    tiles = jax.ShapeDtypeStruct((n_tiles,), jnp.int32)
    pos, src, e_a, e_b, used = pl.pallas_call(
        functools.partial(_plan_kernel, n_tokens=n_tokens, n_tiles=n_tiles, tm=tm,
                          key_shift=key_shift),
        in_specs=[smem, smem], out_specs=[smem] * 5,
        out_shape=[jax.ShapeDtypeStruct((n_tokens,), jnp.int32),
                   jax.ShapeDtypeStruct((n_tiles * tm,), jnp.int32), tiles, tiles, tiles],
        scratch_shapes=[pltpu.SMEM((N_CLASSES,), jnp.int32)],
        name="moe_plan",
    )(key, counts)
    return pos, src, e_a, e_b, used, n_tiles


def _slab_pitch(n_blocks):
    return n_blocks + 4 if n_blocks % 8 == 0 else n_blocks


def _to_slabs(ref, value, n_rows, pitch, offset=0):
    for j in range(value.shape[1] // LANES):
        ref[pl.ds(offset + j, n_rows, stride=pitch), :] = value[:, j * LANES:(j + 1) * LANES]


def _zero_slab_padding(ref, n_rows, pitch, n_blocks):
    for j in range(n_blocks, pitch):
        ref[pl.ds(j, n_rows, stride=pitch), :] = jnp.zeros((n_rows, LANES), ref.dtype)


def _from_slabs(ref, n_rows, n_blocks, pitch, offset=0):
    return jnp.concatenate([ref[pl.ds(offset + j, n_rows, stride=pitch), :]
                            for j in range(n_blocks)], axis=1)


def _gather_copy(tab_ref, buf_ref, sem, row, r, rec, pitch):
    return pltpu.make_async_copy(tab_ref.at[pl.ds(pl.multiple_of(row * pitch, 4), rec), :],
                                 buf_ref.at[pl.ds(pl.multiple_of(r * pitch, 4), rec), :], sem)


def _gather_start(idx_ref, tab_ref, buf_ref, sem, tile, tm, rec, pitch):
    def issue(r, _):
        _gather_copy(tab_ref, buf_ref, sem, idx_ref[tile * tm + r], r, rec, pitch).start()
        return 0

    lax.fori_loop(0, tm, issue, 0, unroll=8)


def _gather_wait(tab_ref, buf_ref, sem, tm, rec):
    pltpu.make_async_copy(tab_ref.at[pl.ds(0, tm * rec), :], buf_ref.at[pl.ds(0, tm * rec), :],
                          sem).wait()


def _experts_kernel(src_ref, ea_ref, eb_ref, used_ref, x_ref, w1a_ref, w3a_ref, w2a_ref,
                    w1b_ref, w3b_ref, w2b_ref, o_ref, buf_ref, sem_ref, *, tm, n_tiles, n_blocks):
    i = pl.program_id(0)
    ahead = EXPERT_GATHER_AHEAD
    slot = lax.rem(i, ahead + 1)
    fill_slot = lax.rem(i + ahead, ahead + 1)
    used = lambda tile: used_ref[tile] != 0
    in_flight = (i < ahead) | used(jnp.maximum(i - ahead, 0))
    x_pitch = _slab_pitch(n_blocks)
    y_pitch = _slab_pitch(2 * n_blocks)

    @pl.when(i == 0)
    def _():
        for j in range(ahead):
            _gather_start(src_ref, x_ref, buf_ref.at[j], sem_ref.at[j], j, tm, n_blocks, x_pitch)

    @pl.when(used(i))
    def _():
        _gather_wait(x_ref, buf_ref.at[slot], sem_ref.at[slot], tm, n_blocks)
        x = _from_slabs(buf_ref.at[slot], tm, n_blocks, x_pitch).astype(BF16)
        ups = [(jnp.dot(x, w1_ref[...], preferred_element_type=F32),
                jnp.dot(x, w3_ref[...], preferred_element_type=F32))
               for w1_ref, w3_ref in ((w1a_ref, w3a_ref), (w1b_ref, w3b_ref))]
        for r in range(tm):
            _gather_copy(x_ref, buf_ref.at[fill_slot], sem_ref.at[fill_slot],
                         src_ref[(i + ahead) * tm + r], r, n_blocks, x_pitch).start()
        for (a, b), w2_ref, offset in zip(ups, (w2a_ref, w2b_ref), (0, n_blocks)):
            hid = a * (1.0 / (1.0 + jnp.exp(-a))) * b
            y = jnp.dot(hid.astype(BF16), w2_ref[...], preferred_element_type=F32)
            _to_slabs(o_ref, y, tm, y_pitch, offset=offset)
        _zero_slab_padding(o_ref, tm, y_pitch, 2 * n_blocks)

    @pl.when(jnp.logical_not(used(i)))
    def _():
        @pl.when(in_flight)
        def _():
            _gather_wait(x_ref, buf_ref.at[slot], sem_ref.at[slot], tm, n_blocks)

        o_ref[...] = jnp.zeros_like(o_ref)


def _experts(x1s, src, e_a, e_b, used, w1, w3, w2, *, tm, n_tiles):
    d, f = w1.shape[1], w1.shape[2]
    n_blocks = d // LANES
    x_pitch = _slab_pitch(n_blocks)
    y_pitch = _slab_pitch(2 * n_blocks)
    up_a = pl.BlockSpec((None, d, f), lambda i, src, ea, eb, used: (ea[i], 0, 0))
    up_b = pl.BlockSpec((None, d, f), lambda i, src, ea, eb, used: (eb[i], 0, 0))
    down_a = pl.BlockSpec((None, f, d), lambda i, src, ea, eb, used: (ea[i], 0, 0))
    down_b = pl.BlockSpec((None, f, d), lambda i, src, ea, eb, used: (eb[i], 0, 0))
    return pl.pallas_call(
        functools.partial(_experts_kernel, tm=tm, n_tiles=n_tiles, n_blocks=n_blocks),
        grid_spec=pltpu.PrefetchScalarGridSpec(
            num_scalar_prefetch=4, grid=(n_tiles,),
            in_specs=[pl.BlockSpec(memory_space=pl.ANY),
                      up_a, up_a, down_a, up_b, up_b, down_b],
            out_specs=pl.BlockSpec((tm * y_pitch, LANES), lambda i, src, ea, eb, used: (i, 0)),
            scratch_shapes=[pltpu.VMEM((EXPERT_GATHER_AHEAD + 1, tm * x_pitch, LANES), F32),
                            pltpu.SemaphoreType.DMA((EXPERT_GATHER_AHEAD + 1,))]),
        out_shape=jax.ShapeDtypeStruct((n_tiles * tm * y_pitch, LANES), F32),
        compiler_params=_params(1),
        name="experts",
    )(src, e_a, e_b, used, x1s, w1, w3, w2, w1, w3, w2)


def _combine_kernel(pos_ref, x_ref, route_ref, lng_ref, lnb_ref, ys_ref, o_ref, buf_ref, sem_ref,
                    *, tm, n_tiles, n_blocks, alpha, ahead):
    i = pl.program_id(0)
    x_pitch = _slab_pitch(n_blocks)
    y_pitch = _slab_pitch(2 * n_blocks)
    rec = 2 * n_blocks
    n_buf = ahead + 1
    slot = lax.rem(i, n_buf)

    @pl.when(i == 0)
    def _():
        for j in range(ahead):
            _gather_start(pos_ref, ys_ref, buf_ref.at[j], sem_ref.at[j], j, tm, rec, y_pitch)

    _gather_wait(ys_ref, buf_ref.at[slot], sem_ref.at[slot], tm, rec)
    ys = buf_ref.at[slot]
    y_a = _from_slabs(ys, tm, n_blocks, y_pitch)
    y_b = _from_slabs(ys, tm, n_blocks, y_pitch, offset=n_blocks)
    fill = lax.rem(i + ahead, n_buf)
    nxt = jnp.minimum(i + ahead, n_tiles - 1)
    for r in range(tm):
        _gather_copy(ys_ref, buf_ref.at[fill], sem_ref.at[fill], pos_ref[nxt * tm + r], r, rec,
                     y_pitch).start()
    route = route_ref[...]
    moe = route[:, ROUTE_WA:ROUTE_WA + 1] * y_a + route[:, ROUTE_WB:ROUTE_WB + 1] * y_b
    x1 = _from_slabs(x_ref, tm, n_blocks, x_pitch)
    o_ref[...] = _layer_norm(alpha * x1 + moe, lng_ref[...], lnb_ref[...])

    @pl.when(i == n_tiles - 1)
    def _():
        for j in range(1, ahead + 1):
            extra = lax.rem(i + j, n_buf)
            _gather_wait(ys_ref, buf_ref.at[extra], sem_ref.at[extra], tm, rec)


def _combine(x1s, route, ys, pos, ln_g, ln_b, *, alpha, tm):
    t = route.shape[0]
    d = ln_g.shape[1]
    n_blocks = d // LANES
    x_pitch = _slab_pitch(n_blocks)
    y_pitch = _slab_pitch(2 * n_blocks)
    n_tiles = t // tm
    ahead = min(COMBINE_GATHER_AHEAD, n_tiles - 1)
    return pl.pallas_call(
        functools.partial(_combine_kernel, tm=tm, n_tiles=n_tiles, n_blocks=n_blocks, alpha=alpha,
                          ahead=ahead),
        grid_spec=pltpu.PrefetchScalarGridSpec(
            num_scalar_prefetch=1, grid=(n_tiles,),
            in_specs=[pl.BlockSpec((tm * x_pitch, LANES), lambda i, pos: (i, 0)),
                      pl.BlockSpec((tm, LANES), lambda i, pos: (i, 0)),
                      pl.BlockSpec((1, d), lambda i, pos: (0, 0)),
                      pl.BlockSpec((1, d), lambda i, pos: (0, 0)),
                      pl.BlockSpec(memory_space=pl.ANY)],
            out_specs=pl.BlockSpec((tm, d), lambda i, pos: (i, 0)),
            scratch_shapes=[pltpu.VMEM((ahead + 1, tm * y_pitch, LANES), F32),
                            pltpu.SemaphoreType.DMA((ahead + 1,))]),
        out_shape=jax.ShapeDtypeStruct((t, d), F32),
        compiler_params=_params(1),
        name="moe_combine_ln2",
    )(pos, x1s, route, ln_g, ln_b, ys)


def _moe(x1s, logits, w1, w3, w2, ln_g, ln_b, *, alpha, tm):
    t = logits.shape[0]
    tm = min(tm, t)
    key_shift = max(t - 1, 1).bit_length()
    assert N_CLASSES << key_shift < 1 << 24
    route, key, counts = _route(logits, tm=tm, key_shift=key_shift)
    pos, src, e_a, e_b, used, n_tiles = _plan(key.reshape(t), counts.reshape(LANES), tm=tm,
                                              key_shift=key_shift)
    ys = _experts(x1s, src, e_a, e_b, used, w1, w3, w2, tm=tm, n_tiles=n_tiles)
    return _combine(x1s, route, ys, pos, ln_g, ln_b, alpha=alpha, tm=tm)


def kernel(x, mem, w_in, b_gate, w_mem_kv, da_lambda, da_norm_g, ret_gn_g, ret_gn_b, w_branch, w_o,
           ln1_g, ln1_b, w_rg, b_rg, w_re, b_re, w1, w3, w2, ln2_g, ln2_b):
    batch, seq, d = x.shape
    mem_len = mem.shape[1]
    depth = w_in.shape[0]
    alpha = (2.0 * depth) ** 0.25
    t = batch * seq
    x2 = x.reshape(t, d)
    for l in range(depth):
        lambda_init = 0.8 - 0.6 * math.exp(-0.3 * l)
        proj = _matmul(x2, w_in[l].astype(BF16), tm=2048, tn=1024, name="in_proj")
        mem_kv = _matmul(mem.reshape(batch * mem_len, d), w_mem_kv[l].astype(BF16),
                         tm=1024, tn=1024, name="mem_kv_proj")
        y_da = _diff_attention(proj, da_lambda[l], da_norm_g[l], batch=batch, seq=seq, d=d,
                               lambda_init=lambda_init, tile=256)
        y_ret = _retention(proj, ret_gn_g[l], ret_gn_b[l], batch=batch, seq=seq, d=d)
        y_mem = _memory_attention(proj, mem_kv, batch=batch, seq=seq, mem_len=mem_len, d=d, tq=512)
        n_route = N_GROUPS + N_EXPERTS
        wr = jnp.pad(jnp.concatenate([w_rg[l], w_re[l]], axis=1), ((0, 0), (0, LANES - n_route)))
        br = jnp.pad(jnp.concatenate([b_rg[l], b_re[l]]), (0, LANES - n_route)).reshape(1, LANES)
        x1, logits = _merge(x2, y_da, y_ret, y_mem, proj, b_gate[l].reshape(N_BRANCH, d),
                            w_branch[l].astype(BF16), w_o[l].astype(BF16),
                            ln1_g[l].reshape(1, d), ln1_b[l].reshape(1, d), wr, br,
                            alpha=alpha, tm=512)
        x2 = _moe(x1, logits, w1[l].astype(BF16), w3[l].astype(BF16), w2[l].astype(BF16),
                  ln2_g[l].reshape(1, d), ln2_b[l].reshape(1, d), alpha=alpha, tm=256)
    return x2.reshape(batch, seq, d)
```

```python
import functools
import math

import jax
import jax.numpy as jnp
import numpy as np
from jax import lax
from jax.experimental import pallas as pl
from jax.experimental.pallas import tpu as pltpu

BF16 = jnp.bfloat16
F32 = jnp.float32

DA_HEADS = 8
RET_HEADS = 4
RET_CHUNK = 128
MEM_HEADS = 4
N_BRANCH = 3
N_GROUPS = 4
EXPERTS_PER_GROUP = 4
N_EXPERTS = N_GROUPS * EXPERTS_PER_GROUP
EPS = 1e-5
LANES = 128
VMEM_LIMIT = 56 * 1024 * 1024

_NT = (((1,), (1,)), ((), ()))


def _params(n_axes):
    return pltpu.CompilerParams(dimension_semantics=("arbitrary",) * n_axes,
                                vmem_limit_bytes=VMEM_LIMIT)


def _layer_norm(z, g, b):
    mu = jnp.mean(z, axis=-1, keepdims=True)
    zc = z - mu
    var = jnp.mean(zc * zc, axis=-1, keepdims=True)
    return zc * lax.rsqrt(var + EPS) * g + b


def _matmul_kernel(x_ref, w_ref, o_ref):
    o_ref[...] = jnp.dot(x_ref[...].astype(BF16), w_ref[...].astype(BF16),
                         preferred_element_type=F32).astype(o_ref.dtype)


def _matmul(x, w, *, tm, tn, name):
    m, k = x.shape
    n = w.shape[1]
    tm, tn = min(tm, m), min(tn, n)
    return pl.pallas_call(
        _matmul_kernel,
        grid=(m // tm, n // tn),
        in_specs=[pl.BlockSpec((tm, k), lambda i, j: (i, 0)),
                  pl.BlockSpec((k, tn), lambda i, j: (0, j))],
        out_specs=pl.BlockSpec((tm, tn), lambda i, j: (i, j)),
        out_shape=jax.ShapeDtypeStruct((m, n), BF16),
        compiler_params=_params(2),
        name=name,
    )(x, w)


DA_ONES_ROWS = 16
DA_MASKED = -1e30
DA_CHUNK_ROWS = 32
LOG2_E = 1.4426950408889634


def _da_schedule(n_blk):
    full = [(qi, p) for qi in range(n_blk) for p in range(qi // 2)]
    order = full + [(qi, qi // 2) for qi in range(n_blk)]
    order.append(order[-1])
    table = np.asarray(order, np.int32)
    return len(full), table[:, 0], table[:, 1]


def _da_kernel(sq_ref, sp_ref, slope_ref, lam_ref, g_ref, q_ref, k_ref, v_ref, o_ref,
               ka0_ref, ka1_ref, qa0_ref, qa1_ref, vt_ref, s_ref, acc0_ref, acc1_ref, m0_ref, m1_ref,
               *, tile, dk, n_blk, n_full, lambda_init):
    h = pl.program_id(1)
    a_log2 = slope_ref[h] * LOG2_E
    w = 2 * dk
    ka_refs = (ka0_ref, ka1_ref)
    qa_refs = (qa0_ref, qa1_ref)
    acc_refs = (acc0_ref, acc1_ref)
    m_refs = (m0_ref, m1_ref)
    lane = lax.broadcasted_iota(jnp.int32, (tile, w), 1)
    pos = lax.broadcasted_iota(jnp.int32, (tile, w), 0).astype(F32)
    pieces = []
    rest = jnp.full((tile, w), a_log2, F32)
    for _ in range(3):
        piece = rest.astype(BF16).astype(F32)
        pieces.append(piece)
        rest = rest - piece
    in_comp, k_bias, q_bias = [], [], []
    for c in range(2):
        free = dk * (1 - c)
        in_comp.append((lane >= c * dk) & (lane < (c + 1) * dk))
        kb = jnp.where((lane >= free) & (lane < free + 3), pos, 0.0)
        qb = jnp.where((lane >= free + 3) & (lane < free + 6), -pos, 0.0)
        for i in range(3):
            kb = jnp.where(lane == free + 3 + i, pieces[i], kb)
            qb = jnp.where(lane == free + i, pieces[i], qb)
        k_bias.append(kb)
        q_bias.append(qb)

    def prepare(pp, _):
        for jj in range(2):
            blk = 2 * pp + jj
            rows = pl.ds(pl.multiple_of(blk * tile, tile), tile)
            kf = k_ref[rows, :].astype(F32)
            qf = q_ref[rows, :].astype(F32) * (dk ** -0.5 * LOG2_E)
            for c in range(2):
                ka_refs[c][rows, :] = jnp.where(in_comp[c], kf, k_bias[c]).astype(BF16)
                qa_refs[c][rows, :] = jnp.where(in_comp[c], qf, q_bias[c]).astype(BF16)
                acc_refs[c][blk] = jnp.zeros((w + DA_ONES_ROWS, tile), F32)
                m_refs[c][blk] = jnp.full((1, tile), -jnp.inf, F32)
            vt_ref[pp, 0:w, jj * tile:(jj + 1) * tile] = v_ref[rows, :].astype(F32).T.astype(BF16)
        vt_ref[pp, w:w + DA_ONES_ROWS, :] = jnp.ones((DA_ONES_ROWS, 2 * tile), BF16)
        return 0

    lax.fori_loop(0, n_blk // 2, prepare, 0)

    lv = lam_ref[...]
    lam = (jnp.exp(jnp.sum(lv[0:1] * lv[1:2], axis=1, keepdims=True))
           - jnp.exp(jnp.sum(lv[2:3] * lv[3:4], axis=1, keepdims=True)) + lambda_init)
    out_gain = jnp.broadcast_to(g_ref[...] * (1.0 - lambda_init), (w, tile))
    chunk = DA_CHUNK_ROWS
    n_chunks = tile // chunk
    rel0 = (lax.broadcasted_iota(jnp.int32, (chunk, tile), 0)
            - lax.broadcasted_iota(jnp.int32, (chunk, tile), 1))

    def issue_scores(t, slot, n_kv):
        qi, p = sq_ref[t], sp_ref[t]
        q_rows = pl.ds(pl.multiple_of(qi * tile, tile), tile)
        for jj in range(n_kv):
            k_rows = pl.ds(pl.multiple_of((2 * p + jj) * tile, tile), tile)
            for c in range(2):
                s_ref[slot, 2 * jj + c] = lax.dot_general(
                    ka_refs[c][k_rows, :], qa_refs[c][q_rows, :], _NT, preferred_element_type=F32)

    def consume(t, slot, diag):
        n_kv = 1 if diag == 0 else 2
        qi, p = sq_ref[t], sp_ref[t]
        offs = [a_log2 * ((2 * p + jj - qi) * tile).astype(F32) for jj in range(n_kv)]
        accs = []
        for c in range(2):
            tiles = [s_ref.at[slot, 2 * jj + c] for jj in range(n_kv)]
            m_old = m_refs[c][qi]
            m_new = m_old
            for jj in range(n_kv):
                col_max = None
                for r in range(n_chunks):
                    rows = slice(r * chunk, (r + 1) * chunk)
                    x = tiles[jj][rows, :]
                    if jj == diag:
                        x = jnp.where(rel0 <= -r * chunk, x, DA_MASKED)
                        tiles[jj][rows, :] = x
                    col_max = x if col_max is None else jnp.maximum(col_max, x)
                m_new = jnp.maximum(m_new, jnp.max(col_max, axis=0, keepdims=True) + offs[jj])
            pts = []
            for jj in range(n_kv):
                shift = m_new - offs[jj]
                for r in range(n_chunks):
                    x = tiles[jj][r * chunk:(r + 1) * chunk, :]
                    pts.append(jnp.exp2(x - shift).astype(BF16))
            vt = vt_ref[p] if n_kv == 2 else vt_ref[p, :, 0:tile]
            pv = jnp.dot(vt, jnp.concatenate(pts, axis=0), preferred_element_type=F32)
            acc = jnp.exp2(m_old - m_new) * acc_refs[c][qi] + pv
            if diag is None:
                acc_refs[c][qi] = acc
                m_refs[c][qi] = m_new
            else:
                accs.append(acc)
        if diag is not None:
            a0, a1 = accs
            ot = a0[0:w] / a0[w:w + 1] - lam * (a1[0:w] / a1[w:w + 1])
            ot = ot * lax.rsqrt(jnp.mean(ot * ot, axis=0, keepdims=True) + EPS) * out_gain
            q_rows = pl.ds(pl.multiple_of(qi * tile, tile), tile)
            o_ref[q_rows, :] = ot.T.astype(o_ref.dtype)

    def two_entries(i, base, last):
        t = base + 2 * i
        issue_scores(t + 1, 1, 2)
        consume(t, 0, 0 if last else None)
        issue_scores(t + 2, 0, 1 if last else 2)
        consume(t + 1, 1, 1 if last else None)
        return 0

    issue_scores(0, 0, 2 if n_full else 1)
    lax.fori_loop(0, n_full // 2, lambda i, _: two_entries(i, 0, False), 0)
    lax.fori_loop(0, n_blk // 2, lambda i, _: two_entries(i, n_full, True), 0)


def _diff_attention(proj, lam_vecs, norm_g, *, batch, seq, d, lambda_init, tile):
    dk = d // (2 * DA_HEADS)
    w = 2 * dk
    tile = min(tile, seq)
    assert tile <= 256 and seq % (2 * tile) == 0
    n_blk = seq // tile
    hpb = d // w
    slopes = 2.0 ** (-8.0 * (jnp.arange(DA_HEADS, dtype=F32) + 1.0) / DA_HEADS)
    n_full, sched_q, sched_p = _da_schedule(n_blk)
    kern = functools.partial(_da_kernel, tile=tile, dk=dk, n_blk=n_blk, n_full=n_full,
                             lambda_init=lambda_init)
    smem = pl.BlockSpec(memory_space=pltpu.SMEM)
    return pl.pallas_call(
        kern,
        grid=(batch, DA_HEADS),
        in_specs=[smem, smem, smem,
                  pl.BlockSpec((4, dk), lambda b, h: (0, 0)),
                  pl.BlockSpec((w, 1), lambda b, h: (0, 0)),
                  pl.BlockSpec((seq, w), lambda b, h: (b, h)),
                  pl.BlockSpec((seq, w), lambda b, h: (b, hpb + h)),
                  pl.BlockSpec((seq, w), lambda b, h: (b, 2 * hpb + h))],
        out_specs=pl.BlockSpec((seq, w), lambda b, h: (b, h)),
        out_shape=jax.ShapeDtypeStruct((batch * seq, d), BF16),
        scratch_shapes=[pltpu.VMEM((seq, w), BF16), pltpu.VMEM((seq, w), BF16),
                        pltpu.VMEM((seq, w), BF16), pltpu.VMEM((seq, w), BF16),
                        pltpu.VMEM((n_blk // 2, w + DA_ONES_ROWS, 2 * tile), BF16),
                        pltpu.VMEM((2, 4, tile, tile), F32),
                        pltpu.VMEM((n_blk, w + DA_ONES_ROWS, tile), F32),
                        pltpu.VMEM((n_blk, w + DA_ONES_ROWS, tile), F32),
                        pltpu.VMEM((n_blk, 1, tile), F32),
                        pltpu.VMEM((n_blk, 1, tile), F32)],
        compiler_params=_params(2),
        name="diff_attention",
    )(jnp.asarray(sched_q), jnp.asarray(sched_p), slopes, lam_vecs, norm_g.reshape(w, 1),
      proj, proj, proj)


def _ret_kernel(cd_ref, intra_ref, qd_ref, kd_ref, gng_ref, gnb_ref,
                q_ref, k_ref, v_ref, g_ref, o_ref, state_ref, *, n_chunks, dk, dv):
    c_len = RET_CHUNK
    heads = range(RET_HEADS)
    state_ref[...] = jnp.zeros_like(state_ref)

    def chunk(c, _):
        rows = pl.ds(pl.multiple_of(c * c_len, c_len), c_len)
        q = [q_ref[rows, h * dk:(h + 1) * dk] for h in heads]
        k = [k_ref[rows, h * dk:(h + 1) * dk] for h in heads]
        v = [v_ref[rows, h * dv:(h + 1) * dv] for h in heads]
        state = [state_ref[h] for h in heads]
        s = [lax.dot_general(q[h], k[h], _NT, preferred_element_type=F32) for h in heads]
        cross = [jnp.dot(q[h], state[h].astype(BF16), preferred_element_type=F32) for h in heads]
        kv = [jnp.dot((k[h].astype(F32) * kd_ref[h]).T.astype(BF16), v[h],
                      preferred_element_type=F32) for h in heads]
        o = [jnp.dot((s[h] * intra_ref[h]).astype(BF16), v[h], preferred_element_type=F32)
             + cross[h] * qd_ref[h] for h in heads]
        for h in heads:
            state_ref[h] = state[h] * cd_ref[h] + kv[h]
            cols = slice(h * dv, (h + 1) * dv)
            mu = jnp.mean(o[h], axis=1, keepdims=True)
            oc = o[h] - mu
            var = jnp.mean(oc * oc, axis=1, keepdims=True)
            y = oc * lax.rsqrt(var + EPS) * gng_ref[:, cols] + gnb_ref[:, cols]
            gate = g_ref[rows, cols].astype(F32)
            y = gate * (1.0 / (1.0 + jnp.exp(-gate))) * y
            o_ref[rows, cols] = y.astype(o_ref.dtype)
        return 0

    lax.fori_loop(0, n_chunks, chunk, 0)


def _retention(proj, gn_g, gn_b, *, batch, seq, d):
    dv = d // RET_HEADS
    dk = dv // 2
    c_len = RET_CHUNK
    n_chunks = seq // c_len
    log_g = jnp.log(1.0 - 2.0 ** (-5.0 - jnp.arange(RET_HEADS, dtype=F32)))
    idx = jnp.arange(c_len, dtype=F32)
    rel = idx[:, None] - idx[None, :]
    intra = jnp.where(rel[None] >= 0, jnp.exp(rel[None] * log_g[:, None, None]), 0.0) * dk ** -0.5
    q_decay = jnp.exp((idx[None, :] + 1.0) * log_g[:, None])[:, :, None]
    k_decay = (jnp.exp((c_len - 1.0 - idx[None, :]) * log_g[:, None]) * dk ** -0.5)[:, :, None]
    chunk_decay = jnp.exp(c_len * log_g)
    hk = RET_HEADS * dk
    q_blk = 3 * d // hk
    k_blk = q_blk + 1
    v_blk = (3 * d + 2 * hk) // d
    g_blk = v_blk + 1
    kern = functools.partial(_ret_kernel, n_chunks=n_chunks, dk=dk, dv=dv)
    const3 = lambda b: (0, 0, 0)
    return pl.pallas_call(
        kern,
        grid=(batch,),
        in_specs=[pl.BlockSpec(memory_space=pltpu.SMEM),
                  pl.BlockSpec((RET_HEADS, c_len, c_len), const3),
                  pl.BlockSpec((RET_HEADS, c_len, 1), const3),
                  pl.BlockSpec((RET_HEADS, c_len, 1), const3),
                  pl.BlockSpec((1, d), lambda b: (0, 0)),
                  pl.BlockSpec((1, d), lambda b: (0, 0)),
                  pl.BlockSpec((seq, hk), lambda b: (b, q_blk)),
                  pl.BlockSpec((seq, hk), lambda b: (b, k_blk)),
                  pl.BlockSpec((seq, d), lambda b: (b, v_blk)),
                  pl.BlockSpec((seq, d), lambda b: (b, g_blk))],
        out_specs=pl.BlockSpec((seq, d), lambda b: (b, 0)),
        out_shape=jax.ShapeDtypeStruct((batch * seq, d), BF16),
        scratch_shapes=[pltpu.VMEM((RET_HEADS, dk, dv), F32)],
        compiler_params=_params(1),
        name="retention",
    )(chunk_decay, intra, q_decay, k_decay, gn_g.reshape(1, d), gn_b.reshape(1, d),
      proj, proj, proj, proj)


def _mem_kernel(q_ref, kv_ref, o_ref, *, d, dh):
    for h in range(MEM_HEADS):
        q = q_ref[:, h * dh:(h + 1) * dh]
        k = kv_ref[:, h * dh:(h + 1) * dh]
        v = kv_ref[:, d + h * dh:d + (h + 1) * dh]
        s = lax.dot_general(q, k, _NT, preferred_element_type=F32) * dh ** -0.5
        p = jnp.exp(s - jnp.max(s, axis=1, keepdims=True))
        l = jnp.sum(p, axis=1, keepdims=True)
        o = jnp.dot(p.astype(BF16), v, preferred_element_type=F32) / l
        o_ref[:, h * dh:(h + 1) * dh] = o.astype(o_ref.dtype)


def _memory_attention(proj, mem_kv, *, batch, seq, mem_len, d, tq):
    dh = d // MEM_HEADS
    tq = min(tq, seq)
    nq = seq // tq
    q_blk = 6
    kern = functools.partial(_mem_kernel, d=d, dh=dh)
    return pl.pallas_call(
        kern,
        grid=(batch, nq),
        in_specs=[pl.BlockSpec((tq, d), lambda b, i: (b * nq + i, q_blk)),
                  pl.BlockSpec((mem_len, 2 * d), lambda b, i: (b, 0))],
        out_specs=pl.BlockSpec((tq, d), lambda b, i: (b * nq + i, 0)),
        out_shape=jax.ShapeDtypeStruct((batch * seq, d), BF16),
        compiler_params=_params(2),
        name="memory_attention",
    )(proj, mem_kv)


def _merge_kernel(x_ref, yda_ref, yret_ref, ymem_ref, g0_ref, g1_ref, g2_ref, bg_ref,
                  wb_ref, wo_ref, lng_ref, lnb_ref, wrh_ref, wrl_ref, br_ref, x1_ref, lg_ref,
                  *, alpha):
    ys = (yda_ref, yret_ref, ymem_ref)
    gs = (g0_ref, g1_ref, g2_ref)
    merged = None
    for n in range(N_BRANCH):
        gate = gs[n][...].astype(F32) + bg_ref[n:n + 1, :]
        gate = 1.0 / (1.0 + jnp.exp(-gate))
        term = gate * jnp.dot(ys[n][...], wb_ref[n], preferred_element_type=F32)
        merged = term if merged is None else merged + term
    out = jnp.dot(merged.astype(BF16), wo_ref[...], preferred_element_type=F32)
    x1 = _layer_norm(alpha * x_ref[...] + out, lng_ref[...], lnb_ref[...])
    tm, n_blocks = x1.shape[0], x1.shape[1] // LANES
    _to_slabs(x1_ref, x1, tm, _slab_pitch(n_blocks))
    _zero_slab_padding(x1_ref, tm, _slab_pitch(n_blocks), n_blocks)
    x_hi = x1.astype(BF16)
    x_lo = (x1 - x_hi.astype(F32)).astype(BF16)
    lg_ref[...] = (jnp.dot(x_hi, wrh_ref[...], preferred_element_type=F32)
                   + jnp.dot(x_lo, wrh_ref[...], preferred_element_type=F32)
                   + jnp.dot(x_hi, wrl_ref[...], preferred_element_type=F32) + br_ref[...])


def _merge(x2, y_da, y_ret, y_mem, proj, b_gate, wb, wo, ln_g, ln_b, wr, br, *, alpha, tm):
    t, d = x2.shape
    tm = min(tm, t)
    gate_blk = 7
    row = lambda i: (i, 0)
    const2 = lambda i: (0, 0)
    once = pl.Buffered(1)
    x_pitch = _slab_pitch(d // LANES)
    wr_hi = wr.astype(BF16)
    wr_lo = (wr - wr_hi.astype(F32)).astype(BF16)
    kern = functools.partial(_merge_kernel, alpha=alpha)
    return pl.pallas_call(
        kern,
        grid=(t // tm,),
        in_specs=[pl.BlockSpec((tm, d), row), pl.BlockSpec((tm, d), row),
                  pl.BlockSpec((tm, d), row), pl.BlockSpec((tm, d), row),
                  pl.BlockSpec((tm, d), lambda i: (i, gate_blk)),
                  pl.BlockSpec((tm, d), lambda i: (i, gate_blk + 1)),
                  pl.BlockSpec((tm, d), lambda i: (i, gate_blk + 2)),
                  pl.BlockSpec((N_BRANCH, d), const2),
                  pl.BlockSpec((N_BRANCH, d, d), lambda i: (0, 0, 0), pipeline_mode=once),
                  pl.BlockSpec((d, d), const2, pipeline_mode=once),
                  pl.BlockSpec((1, d), const2), pl.BlockSpec((1, d), const2),
                  pl.BlockSpec((d, LANES), const2), pl.BlockSpec((d, LANES), const2),
                  pl.BlockSpec((1, LANES), const2)],
        out_specs=[pl.BlockSpec((tm * x_pitch, LANES), row), pl.BlockSpec((tm, LANES), row)],
        out_shape=[jax.ShapeDtypeStruct((t * x_pitch, LANES), F32),
                   jax.ShapeDtypeStruct((t, LANES), F32)],
        compiler_params=_params(1),
        name="merge_ln1_router",
    )(x2, y_da, y_ret, y_mem, proj, proj, proj, b_gate, wb, wo, ln_g, ln_b, wr_hi, wr_lo, br)


PAIRS = ((0, 1), (0, 2), (0, 3), (1, 2), (1, 3), (2, 3))
N_CLASSES = N_GROUPS * len(PAIRS)
ROUTE_CLS, ROUTE_RANK, ROUTE_WA, ROUTE_WB = 0, 1, 2, 3
EXPERT_ROW0 = 8
CLASS_ROWS = 32
EXPERT_GATHER_AHEAD = 2
COMBINE_GATHER_AHEAD = 2


def _route_kernel(lg_ref, triu_ref, route_ref, key_ref, cnt_ref, carry_ref, *, key_shift):
    i = pl.program_id(0)

    @pl.when(i == 0)
    def _():
        carry_ref[...] = jnp.zeros_like(carry_ref)

    lgt = lg_ref[...].T
    tm = lgt.shape[1]
    row = lambda r: lgt[r:r + 1, :]

    def first_max(vals):
        best = vals[0]
        for v in vals[1:]:
            best = jnp.maximum(best, v)
        idx = jnp.full_like(best, float(len(vals) - 1))
        for k in range(len(vals) - 2, -1, -1):
            idx = jnp.where(vals[k] == best, float(k), idx)
        return best, idx

    groups = [row(g) for g in range(N_GROUPS)]
    gmax, g_idx = first_max(groups)
    denom = jnp.exp(groups[0] - gmax)
    for g in groups[1:]:
        denom = denom + jnp.exp(g - gmax)
    p_top = 1.0 / denom
    sel = []
    for k in range(EXPERTS_PER_GROUP):
        v = row(EXPERT_ROW0 + (N_GROUPS - 1) * EXPERTS_PER_GROUP + k)
        for g in range(N_GROUPS - 2, -1, -1):
            v = jnp.where(g_idx == float(g), row(EXPERT_ROW0 + g * EXPERTS_PER_GROUP + k), v)
        sel.append(v)
    e1, i1 = first_max(sel)
    rest = [jnp.where(i1 == float(k), -jnp.inf, sel[k]) for k in range(EXPERTS_PER_GROUP)]
    e2, i2 = first_max(rest)
    r = jnp.exp(e2 - e1)
    w1 = p_top / (1.0 + r)
    w2 = p_top * r / (1.0 + r)

    a_loc = jnp.minimum(i1, i2)
    b_loc = jnp.maximum(i1, i2)
    pair = jnp.where(a_loc == 0.0, b_loc - 1.0, jnp.where(a_loc == 1.0, b_loc + 1.0, 5.0))
    cls = g_idx * float(len(PAIRS)) + pair
    top_is_a = i1 < i2
    w_a = jnp.where(top_is_a, w1, w2)
    w_b = jnp.where(top_is_a, w2, w1)

    cls_row = lax.broadcasted_iota(jnp.int32, (CLASS_ROWS, tm), 0).astype(F32)
    onehot = jnp.where(cls_row == cls, 1.0, 0.0)
    prefix = carry_ref[...] + jnp.dot(onehot.astype(BF16), triu_ref[...],
                                      preferred_element_type=F32)
    rank = jnp.sum(onehot * prefix, axis=0, keepdims=True) - 1.0
    total = carry_ref[...] + jnp.sum(onehot, axis=1, keepdims=True)
    carry_ref[...] = total
    cnt_ref[...] = total.astype(jnp.int32)
    key_ref[...] = (cls * float(1 << key_shift) + rank).astype(jnp.int32)
    record = jnp.concatenate([cls, rank, w_a, w_b, jnp.zeros((LANES - 4, tm), F32)], axis=0)
    route_ref[...] = record.T


def _route(logits, *, tm, key_shift):
    t = logits.shape[0]
    tm = min(tm, t)
    triu = jnp.triu(jnp.ones((tm, tm), BF16))
    return pl.pallas_call(
        functools.partial(_route_kernel, key_shift=key_shift),
        grid=(t // tm,),
        in_specs=[pl.BlockSpec((tm, LANES), lambda i: (i, 0)),
                  pl.BlockSpec((tm, tm), lambda i: (0, 0))],
        out_specs=[pl.BlockSpec((tm, LANES), lambda i: (i, 0)),
                   pl.BlockSpec((1, tm), lambda i: (0, i)),
                   pl.BlockSpec((CLASS_ROWS, 1), lambda i: (0, 0))],
        out_shape=[jax.ShapeDtypeStruct((t, LANES), F32),
                   jax.ShapeDtypeStruct((1, t), jnp.int32),
                   jax.ShapeDtypeStruct((CLASS_ROWS, 1), jnp.int32)],
        scratch_shapes=[pltpu.VMEM((CLASS_ROWS, 1), F32)],
        compiler_params=_params(1),
        name="routing",
    )(logits, triu)


def _class_starts(counts, tm):
    padded = (counts + (tm - 1)) // tm * tm
    return jnp.cumsum(padded) - padded


def _positions_kernel(starts_ref, key_ref, pos_ref, *, key_shift):
    key = key_ref[...]
    cls = lax.shift_right_logical(key, key_shift)
    pos = key & ((1 << key_shift) - 1)
    for c in range(N_CLASSES):
        pos = jnp.where(cls == c, pos + starts_ref[c], pos)
    pos_ref[...] = pos


def _positions(key, starts, *, key_shift):
    return pl.pallas_call(
        functools.partial(_positions_kernel, key_shift=key_shift),
        in_specs=[pl.BlockSpec(memory_space=pltpu.SMEM), pl.BlockSpec(memory_space=pltpu.VMEM)],
        out_specs=pl.BlockSpec(memory_space=pltpu.VMEM),
        out_shape=jax.ShapeDtypeStruct(key.shape, jnp.int32),
        name="moe_positions",
    )(starts, key)


def _plan_kernel(pos_ref, cnt_ref, src_ref, ea_ref, eb_ref, used_ref, *, n_tokens, n_tiles, tm):
    tm_shift = tm.bit_length() - 1
    row0 = jnp.int32(0)
    tile0 = jnp.int32(0)
    last_a = jnp.int32(0)
    last_b = jnp.int32(1)
    for c in range(N_CLASSES):
        e_a = c // len(PAIRS) * EXPERTS_PER_GROUP + PAIRS[c % len(PAIRS)][0]
        e_b = c // len(PAIRS) * EXPERTS_PER_GROUP + PAIRS[c % len(PAIRS)][1]
        cnt = cnt_ref[c]
        n_cls_tiles = lax.shift_right_logical(cnt + (tm - 1), tm_shift)

        def mark(i, _, e_a=e_a, e_b=e_b):
            ea_ref[i] = e_a
            eb_ref[i] = e_b
            used_ref[i] = 1
            return 0

        lax.fori_loop(tile0, tile0 + n_cls_tiles, mark, 0)
        has = n_cls_tiles > 0
        last_a = jnp.where(has, e_a, last_a)
        last_b = jnp.where(has, e_b, last_b)
        row_end = row0 + n_cls_tiles * tm

        def pad(r, _):
            src_ref[r] = 0
            return 0

        lax.fori_loop(row0 + cnt, row_end, pad, 0)
        row0 = row_end
        tile0 = tile0 + n_cls_tiles

    def unused(i, _):
        ea_ref[i] = last_a
        eb_ref[i] = last_b
        used_ref[i] = 0
        return 0

    lax.fori_loop(tile0, n_tiles, unused, 0)
    lax.fori_loop(row0, n_tiles * tm, pad, 0)

    def put(t, _):
        src_ref[pos_ref[t]] = t
        return 0

    lax.fori_loop(0, n_tokens, put, 0, unroll=8)


def _plan(pos, counts, *, tm):
    n_tokens = pos.shape[0]
    assert tm & (tm - 1) == 0
    n_tiles = n_tokens // tm + N_CLASSES - 1 + EXPERT_GATHER_AHEAD
    smem = pl.BlockSpec(memory_space=pltpu.SMEM)
    tiles = jax.ShapeDtypeStruct((n_tiles,), jnp.int32)
    src, e_a, e_b, used = pl.pallas_call(
        functools.partial(_plan_kernel, n_tokens=n_tokens, n_tiles=n_tiles, tm=tm),
        in_specs=[smem, smem], out_specs=[smem] * 4,
        out_shape=[jax.ShapeDtypeStruct((n_tiles * tm,), jnp.int32), tiles, tiles, tiles],
        name="moe_plan",
    )(pos, counts)
    return src, e_a, e_b, used, n_tiles


def _slab_pitch(n_blocks):
    return n_blocks + 4 if n_blocks % 8 == 0 else n_blocks


def _to_slabs(ref, value, n_rows, pitch, offset=0):
    for j in range(value.shape[1] // LANES):
        ref[pl.ds(offset + j, n_rows, stride=pitch), :] = value[:, j * LANES:(j + 1) * LANES]


def _zero_slab_padding(ref, n_rows, pitch, n_blocks):
    for j in range(n_blocks, pitch):
        ref[pl.ds(j, n_rows, stride=pitch), :] = jnp.zeros((n_rows, LANES), ref.dtype)


def _from_slabs(ref, n_rows, n_blocks, pitch, offset=0):
    return jnp.concatenate([ref[pl.ds(offset + j, n_rows, stride=pitch), :]
                            for j in range(n_blocks)], axis=1)


def _gather_copy(tab_ref, buf_ref, sem, row, r, rec, pitch):
    return pltpu.make_async_copy(tab_ref.at[pl.ds(pl.multiple_of(row * pitch, 4), rec), :],
                                 buf_ref.at[pl.ds(pl.multiple_of(r * pitch, 4), rec), :], sem)


def _gather_start(idx_ref, tab_ref, buf_ref, sem, tile, tm, rec, pitch):
    def issue(r, _):
        _gather_copy(tab_ref, buf_ref, sem, idx_ref[tile * tm + r], r, rec, pitch).start()
        return 0

    lax.fori_loop(0, tm, issue, 0, unroll=8)


def _gather_wait(tab_ref, buf_ref, sem, tm, rec):
    pltpu.make_async_copy(tab_ref.at[pl.ds(0, tm * rec), :], buf_ref.at[pl.ds(0, tm * rec), :],
                          sem).wait()


def _experts_kernel(src_ref, ea_ref, eb_ref, used_ref, x_ref, w1a_ref, w3a_ref, w2a_ref,
                    w1b_ref, w3b_ref, w2b_ref, o_ref, buf_ref, sem_ref, *, tm, n_tiles, n_blocks):
    i = pl.program_id(0)
    ahead = EXPERT_GATHER_AHEAD
    slot = lax.rem(i, ahead + 1)
    fill_slot = lax.rem(i + ahead, ahead + 1)
    used = lambda tile: used_ref[tile] != 0
    in_flight = (i < ahead) | used(jnp.maximum(i - ahead, 0))
    x_pitch = _slab_pitch(n_blocks)
    y_pitch = _slab_pitch(2 * n_blocks)

    @pl.when(i == 0)
    def _():
        for j in range(ahead):
            _gather_start(src_ref, x_ref, buf_ref.at[j], sem_ref.at[j], j, tm, n_blocks, x_pitch)

    @pl.when(used(i))
    def _():
        _gather_wait(x_ref, buf_ref.at[slot], sem_ref.at[slot], tm, n_blocks)
        x = _from_slabs(buf_ref.at[slot], tm, n_blocks, x_pitch).astype(BF16)
        ups = [(jnp.dot(x, w1_ref[...], preferred_element_type=F32),
                jnp.dot(x, w3_ref[...], preferred_element_type=F32))
               for w1_ref, w3_ref in ((w1a_ref, w3a_ref), (w1b_ref, w3b_ref))]
        for r in range(tm):
            _gather_copy(x_ref, buf_ref.at[fill_slot], sem_ref.at[fill_slot],
                         src_ref[(i + ahead) * tm + r], r, n_blocks, x_pitch).start()
        for (a, b), w2_ref, offset in zip(ups, (w2a_ref, w2b_ref), (0, n_blocks)):
            hid = a * (1.0 / (1.0 + jnp.exp(-a))) * b
            y = jnp.dot(hid.astype(BF16), w2_ref[...], preferred_element_type=F32)
            _to_slabs(o_ref, y, tm, y_pitch, offset=offset)
        _zero_slab_padding(o_ref, tm, y_pitch, 2 * n_blocks)

    @pl.when(jnp.logical_not(used(i)))
    def _():
        @pl.when(in_flight)
        def _():
            _gather_wait(x_ref, buf_ref.at[slot], sem_ref.at[slot], tm, n_blocks)

        o_ref[...] = jnp.zeros_like(o_ref)


def _experts(x1s, src, e_a, e_b, used, w1, w3, w2, *, tm, n_tiles):
    d, f = w1.shape[1], w1.shape[2]
    n_blocks = d // LANES
    x_pitch = _slab_pitch(n_blocks)
    y_pitch = _slab_pitch(2 * n_blocks)
    up_a = pl.BlockSpec((None, d, f), lambda i, src, ea, eb, used: (ea[i], 0, 0))
    up_b = pl.BlockSpec((None, d, f), lambda i, src, ea, eb, used: (eb[i], 0, 0))
    down_a = pl.BlockSpec((None, f, d), lambda i, src, ea, eb, used: (ea[i], 0, 0))
    down_b = pl.BlockSpec((None, f, d), lambda i, src, ea, eb, used: (eb[i], 0, 0))
    return pl.pallas_call(
        functools.partial(_experts_kernel, tm=tm, n_tiles=n_tiles, n_blocks=n_blocks),
        grid_spec=pltpu.PrefetchScalarGridSpec(
            num_scalar_prefetch=4, grid=(n_tiles,),
            in_specs=[pl.BlockSpec(memory_space=pl.ANY),
                      up_a, up_a, down_a, up_b, up_b, down_b],
            out_specs=pl.BlockSpec((tm * y_pitch, LANES), lambda i, src, ea, eb, used: (i, 0)),
            scratch_shapes=[pltpu.VMEM((EXPERT_GATHER_AHEAD + 1, tm * x_pitch, LANES), F32),
                            pltpu.SemaphoreType.DMA((EXPERT_GATHER_AHEAD + 1,))]),
        out_shape=jax.ShapeDtypeStruct((n_tiles * tm * y_pitch, LANES), F32),
        compiler_params=_params(1),
        name="experts",
    )(src, e_a, e_b, used, x1s, w1, w3, w2, w1, w3, w2)


def _combine_kernel(pos_ref, x_ref, route_ref, lng_ref, lnb_ref, ys_ref, o_ref, buf_ref, sem_ref,
                    *, tm, n_tiles, n_blocks, alpha, ahead):
    i = pl.program_id(0)
    x_pitch = _slab_pitch(n_blocks)
    y_pitch = _slab_pitch(2 * n_blocks)
    rec = 2 * n_blocks
    n_buf = ahead + 1
    slot = lax.rem(i, n_buf)

    @pl.when(i == 0)
    def _():
        for j in range(ahead):
            _gather_start(pos_ref, ys_ref, buf_ref.at[j], sem_ref.at[j], j, tm, rec, y_pitch)

    _gather_wait(ys_ref, buf_ref.at[slot], sem_ref.at[slot], tm, rec)
    ys = buf_ref.at[slot]
    y_a = _from_slabs(ys, tm, n_blocks, y_pitch)
    y_b = _from_slabs(ys, tm, n_blocks, y_pitch, offset=n_blocks)
    fill = lax.rem(i + ahead, n_buf)
    nxt = jnp.minimum(i + ahead, n_tiles - 1)
    for r in range(tm):
        _gather_copy(ys_ref, buf_ref.at[fill], sem_ref.at[fill], pos_ref[nxt * tm + r], r, rec,
                     y_pitch).start()
    route = route_ref[...]
    moe = route[:, ROUTE_WA:ROUTE_WA + 1] * y_a + route[:, ROUTE_WB:ROUTE_WB + 1] * y_b
    x1 = _from_slabs(x_ref, tm, n_blocks, x_pitch)
    o_ref[...] = _layer_norm(alpha * x1 + moe, lng_ref[...], lnb_ref[...])

    @pl.when(i == n_tiles - 1)
    def _():
        for j in range(1, ahead + 1):
            extra = lax.rem(i + j, n_buf)
            _gather_wait(ys_ref, buf_ref.at[extra], sem_ref.at[extra], tm, rec)


def _combine(x1s, route, ys, pos, ln_g, ln_b, *, alpha, tm):
    t = route.shape[0]
    d = ln_g.shape[1]
    n_blocks = d // LANES
    x_pitch = _slab_pitch(n_blocks)
    y_pitch = _slab_pitch(2 * n_blocks)
    n_tiles = t // tm
    ahead = min(COMBINE_GATHER_AHEAD, n_tiles - 1)
    return pl.pallas_call(
        functools.partial(_combine_kernel, tm=tm, n_tiles=n_tiles, n_blocks=n_blocks, alpha=alpha,
                          ahead=ahead),
        grid_spec=pltpu.PrefetchScalarGridSpec(
            num_scalar_prefetch=1, grid=(n_tiles,),
            in_specs=[pl.BlockSpec((tm * x_pitch, LANES), lambda i, pos: (i, 0)),
                      pl.BlockSpec((tm, LANES), lambda i, pos: (i, 0)),
                      pl.BlockSpec((1, d), lambda i, pos: (0, 0)),
                      pl.BlockSpec((1, d), lambda i, pos: (0, 0)),
                      pl.BlockSpec(memory_space=pl.ANY)],
            out_specs=pl.BlockSpec((tm, d), lambda i, pos: (i, 0)),
            scratch_shapes=[pltpu.VMEM((ahead + 1, tm * y_pitch, LANES), F32),
                            pltpu.SemaphoreType.DMA((ahead + 1,))]),
        out_shape=jax.ShapeDtypeStruct((t, d), F32),
        compiler_params=_params(1),
        name="moe_combine_ln2",
    )(pos, x1s, route, ln_g, ln_b, ys)


def _moe(x1s, logits, w1, w3, w2, ln_g, ln_b, *, alpha, tm):
    t = logits.shape[0]
    tm = min(tm, t)
    key_shift = max(t - 1, 1).bit_length()
    assert N_CLASSES << key_shift < 1 << 24
    route, key, counts = _route(logits, tm=tm, key_shift=key_shift)
    counts = counts.reshape(CLASS_ROWS)[:N_CLASSES]
    pos = _positions(key.reshape(8, t // 8), _class_starts(counts, tm), key_shift=key_shift)
    pos = pos.reshape(t)
    src, e_a, e_b, used, n_tiles = _plan(pos, counts, tm=tm)
    ys = _experts(x1s, src, e_a, e_b, used, w1, w3, w2, tm=tm, n_tiles=n_tiles)
    return _combine(x1s, route, ys, pos, ln_g, ln_b, alpha=alpha, tm=tm)


def kernel(x, mem, w_in, b_gate, w_mem_kv, da_lambda, da_norm_g, ret_gn_g, ret_gn_b, w_branch, w_o,
           ln1_g, ln1_b, w_rg, b_rg, w_re, b_re, w1, w3, w2, ln2_g, ln2_b):
    batch, seq, d = x.shape
    mem_len = mem.shape[1]
    depth = w_in.shape[0]
    alpha = (2.0 * depth) ** 0.25
    t = batch * seq
    x2 = x.reshape(t, d)
    for l in range(depth):
        lambda_init = 0.8 - 0.6 * math.exp(-0.3 * l)
        proj = _matmul(x2, w_in[l], tm=2048, tn=1024, name="in_proj")
        mem_kv = _matmul(mem.reshape(batch * mem_len, d), w_mem_kv[l],
                         tm=1024, tn=1024, name="mem_kv_proj")
        y_da = _diff_attention(proj, da_lambda[l], da_norm_g[l], batch=batch, seq=seq, d=d,
                               lambda_init=lambda_init, tile=256)
        y_ret = _retention(proj, ret_gn_g[l], ret_gn_b[l], batch=batch, seq=seq, d=d)
        y_mem = _memory_attention(proj, mem_kv, batch=batch, seq=seq, mem_len=mem_len, d=d, tq=512)
        gap = EXPERT_ROW0 - N_GROUPS
        tail = LANES - EXPERT_ROW0 - N_EXPERTS
        wr = jnp.concatenate([w_rg[l], jnp.zeros((d, gap), F32), w_re[l],
                              jnp.zeros((d, tail), F32)], axis=1)
        br = jnp.concatenate([b_rg[l], jnp.zeros((gap,), F32), b_re[l],
                              jnp.zeros((tail,), F32)]).reshape(1, LANES)
        x1, logits = _merge(x2, y_da, y_ret, y_mem, proj, b_gate[l].reshape(N_BRANCH, d),
                            w_branch[l].astype(BF16), w_o[l].astype(BF16),
                            ln1_g[l].reshape(1, d), ln1_b[l].reshape(1, d), wr, br,
                            alpha=alpha, tm=512)
        x2 = _moe(x1, logits, w1[l].astype(BF16), w3[l].astype(BF16), w2[l].astype(BF16),
                  ln2_g[l].reshape(1, d), ln2_b[l].reshape(1, d), alpha=alpha, tm=256)
    return x2.reshape(batch, seq, d)
```

```python
import functools
import math

import jax
import jax.numpy as jnp
import numpy as np
from jax import lax
from jax.experimental import pallas as pl
from jax.experimental.pallas import tpu as pltpu

BF16 = jnp.bfloat16
F32 = jnp.float32

DA_HEADS = 8
RET_HEADS = 4
RET_CHUNK = 128
MEM_HEADS = 4
N_BRANCH = 3
N_GROUPS = 4
EXPERTS_PER_GROUP = 4
N_EXPERTS = N_GROUPS * EXPERTS_PER_GROUP
EPS = 1e-5
LANES = 128
VMEM_LIMIT = 56 * 1024 * 1024

_NT = (((1,), (1,)), ((), ()))


def _params(n_axes):
    return pltpu.CompilerParams(dimension_semantics=("arbitrary",) * n_axes,
                                vmem_limit_bytes=VMEM_LIMIT)


def _layer_norm(z, g, b):
    mu = jnp.mean(z, axis=-1, keepdims=True)
    zc = z - mu
    var = jnp.mean(zc * zc, axis=-1, keepdims=True)
    return zc * lax.rsqrt(var + EPS) * g + b


def _matmul_kernel(x_ref, w_ref, o_ref):
    o_ref[...] = jnp.dot(x_ref[...].astype(BF16), w_ref[...].astype(BF16),
                         preferred_element_type=F32).astype(o_ref.dtype)


def _matmul(x, w, *, tm, tn, name):
    m, k = x.shape
    n = w.shape[1]
    tm, tn = min(tm, m), min(tn, n)
    return pl.pallas_call(
        _matmul_kernel,
        grid=(m // tm, n // tn),
        in_specs=[pl.BlockSpec((tm, k), lambda i, j: (i, 0)),
                  pl.BlockSpec((k, tn), lambda i, j: (0, j))],
        out_specs=pl.BlockSpec((tm, tn), lambda i, j: (i, j)),
        out_shape=jax.ShapeDtypeStruct((m, n), BF16),
        compiler_params=_params(2),
        name=name,
    )(x, w)


DA_ONES_ROWS = 16
DA_MASKED = -1e30
DA_CHUNK_ROWS = 32
LOG2_E = 1.4426950408889634


def _da_schedule(n_blk):
    full = [(qi, p) for qi in range(n_blk) for p in range(qi // 2)]
    order = full + [(qi, qi // 2) for qi in range(n_blk)]
    order.append(order[-1])
    table = np.asarray(order, np.int32)
    return len(full), table[:, 0], table[:, 1]


def _da_kernel(sq_ref, sp_ref, slope_ref, lam_ref, g_ref, q_ref, k_ref, v_ref, o_ref,
               ka0_ref, ka1_ref, qa0_ref, qa1_ref, vt_ref, s_ref, acc0_ref, acc1_ref, m0_ref, m1_ref,
               *, tile, dk, n_blk, n_full, lambda_init):
    h = pl.program_id(1)
    a_log2 = slope_ref[h] * LOG2_E
    w = 2 * dk
    ka_refs = (ka0_ref, ka1_ref)
    qa_refs = (qa0_ref, qa1_ref)
    acc_refs = (acc0_ref, acc1_ref)
    m_refs = (m0_ref, m1_ref)
    lane = lax.broadcasted_iota(jnp.int32, (tile, w), 1)
    pos = lax.broadcasted_iota(jnp.int32, (tile, w), 0).astype(F32)
    pieces = []
    rest = jnp.full((tile, w), a_log2, F32)
    for _ in range(3):
        piece = rest.astype(BF16).astype(F32)
        pieces.append(piece)
        rest = rest - piece
    in_comp, k_bias, q_bias = [], [], []
    for c in range(2):
        free = dk * (1 - c)
        in_comp.append((lane >= c * dk) & (lane < (c + 1) * dk))
        kb = jnp.where((lane >= free) & (lane < free + 3), pos, 0.0)
        qb = jnp.where((lane >= free + 3) & (lane < free + 6), -pos, 0.0)
        for i in range(3):
            kb = jnp.where(lane == free + 3 + i, pieces[i], kb)
            qb = jnp.where(lane == free + i, pieces[i], qb)
        k_bias.append(kb)
        q_bias.append(qb)

    def prepare(pp, _):
        for jj in range(2):
            blk = 2 * pp + jj
            rows = pl.ds(pl.multiple_of(blk * tile, tile), tile)
            kf = k_ref[rows, :].astype(F32)
            qf = q_ref[rows, :].astype(F32) * (dk ** -0.5 * LOG2_E)
            for c in range(2):
                ka_refs[c][rows, :] = jnp.where(in_comp[c], kf, k_bias[c]).astype(BF16)
                qa_refs[c][rows, :] = jnp.where(in_comp[c], qf, q_bias[c]).astype(BF16)
                acc_refs[c][blk] = jnp.zeros((w + DA_ONES_ROWS, tile), F32)
                m_refs[c][blk] = jnp.full((1, tile), -jnp.inf, F32)
            vt_ref[pp, 0:w, jj * tile:(jj + 1) * tile] = v_ref[rows, :].astype(F32).T.astype(BF16)
        vt_ref[pp, w:w + DA_ONES_ROWS, :] = jnp.ones((DA_ONES_ROWS, 2 * tile), BF16)
        return 0

    lax.fori_loop(0, n_blk // 2, prepare, 0)

    lv = lam_ref[...]
    lam = (jnp.exp(jnp.sum(lv[0:1] * lv[1:2], axis=1, keepdims=True))
           - jnp.exp(jnp.sum(lv[2:3] * lv[3:4], axis=1, keepdims=True)) + lambda_init)
    out_gain = jnp.broadcast_to(g_ref[...] * (1.0 - lambda_init), (w, tile))
    chunk = DA_CHUNK_ROWS
    n_chunks = tile // chunk
    rel0 = (lax.broadcasted_iota(jnp.int32, (chunk, tile), 0)
            - lax.broadcasted_iota(jnp.int32, (chunk, tile), 1))

    def issue_scores(t, slot, n_kv):
        qi, p = sq_ref[t], sp_ref[t]
        q_rows = pl.ds(pl.multiple_of(qi * tile, tile), tile)
        for jj in range(n_kv):
            k_rows = pl.ds(pl.multiple_of((2 * p + jj) * tile, tile), tile)
            for c in range(2):
                s_ref[slot, 2 * jj + c] = lax.dot_general(
                    ka_refs[c][k_rows, :], qa_refs[c][q_rows, :], _NT, preferred_element_type=F32)

    def consume(t, slot, diag):
        n_kv = 1 if diag == 0 else 2
        qi, p = sq_ref[t], sp_ref[t]
        offs = [a_log2 * ((2 * p + jj - qi) * tile).astype(F32) for jj in range(n_kv)]
        accs = []
        for c in range(2):
            tiles = [s_ref.at[slot, 2 * jj + c] for jj in range(n_kv)]
            m_old = m_refs[c][qi]
            m_new = m_old
            for jj in range(n_kv):
                col_max = None
                for r in range(n_chunks):
                    rows = slice(r * chunk, (r + 1) * chunk)
                    x = tiles[jj][rows, :]
                    if jj == diag:
                        x = jnp.where(rel0 <= -r * chunk, x, DA_MASKED)
                        tiles[jj][rows, :] = x
                    col_max = x if col_max is None else jnp.maximum(col_max, x)
                m_new = jnp.maximum(m_new, jnp.max(col_max, axis=0, keepdims=True) + offs[jj])
            pts = []
            for jj in range(n_kv):
                shift = m_new - offs[jj]
                for r in range(n_chunks):
                    x = tiles[jj][r * chunk:(r + 1) * chunk, :]
                    pts.append(jnp.exp2(x - shift).astype(BF16))
            vt = vt_ref[p] if n_kv == 2 else vt_ref[p, :, 0:tile]
            pv = jnp.dot(vt, jnp.concatenate(pts, axis=0), preferred_element_type=F32)
            acc = jnp.exp2(m_old - m_new) * acc_refs[c][qi] + pv
            if diag is None:
                acc_refs[c][qi] = acc
                m_refs[c][qi] = m_new
            else:
                accs.append(acc)
        if diag is not None:
            a0, a1 = accs
            ot = a0[0:w] / a0[w:w + 1] - lam * (a1[0:w] / a1[w:w + 1])
            ot = ot * lax.rsqrt(jnp.mean(ot * ot, axis=0, keepdims=True) + EPS) * out_gain
            q_rows = pl.ds(pl.multiple_of(qi * tile, tile), tile)
            o_ref[q_rows, :] = ot.T.astype(o_ref.dtype)

    def two_entries(i, base, last):
        t = base + 2 * i
        issue_scores(t + 1, 1, 2)
        consume(t, 0, 0 if last else None)
        issue_scores(t + 2, 0, 1 if last else 2)
        consume(t + 1, 1, 1 if last else None)
        return 0

    issue_scores(0, 0, 2 if n_full else 1)
    lax.fori_loop(0, n_full // 2, lambda i, _: two_entries(i, 0, False), 0)
    lax.fori_loop(0, n_blk // 2, lambda i, _: two_entries(i, n_full, True), 0)


def _diff_attention(proj, lam_vecs, norm_g, *, batch, seq, d, lambda_init, tile):
    dk = d // (2 * DA_HEADS)
    w = 2 * dk
    tile = min(tile, seq)
    assert tile <= 256 and seq % (2 * tile) == 0
    n_blk = seq // tile
    hpb = d // w
    slopes = 2.0 ** (-8.0 * (jnp.arange(DA_HEADS, dtype=F32) + 1.0) / DA_HEADS)
    n_full, sched_q, sched_p = _da_schedule(n_blk)
    kern = functools.partial(_da_kernel, tile=tile, dk=dk, n_blk=n_blk, n_full=n_full,
                             lambda_init=lambda_init)
    smem = pl.BlockSpec(memory_space=pltpu.SMEM)
    return pl.pallas_call(
        kern,
        grid=(batch, DA_HEADS),
        in_specs=[smem, smem, smem,
                  pl.BlockSpec((4, dk), lambda b, h: (0, 0)),
                  pl.BlockSpec((w, 1), lambda b, h: (0, 0)),
                  pl.BlockSpec((seq, w), lambda b, h: (b, h)),
                  pl.BlockSpec((seq, w), lambda b, h: (b, hpb + h)),
                  pl.BlockSpec((seq, w), lambda b, h: (b, 2 * hpb + h))],
        out_specs=pl.BlockSpec((seq, w), lambda b, h: (b, h)),
        out_shape=jax.ShapeDtypeStruct((batch * seq, d), BF16),
        scratch_shapes=[pltpu.VMEM((seq, w), BF16), pltpu.VMEM((seq, w), BF16),
                        pltpu.VMEM((seq, w), BF16), pltpu.VMEM((seq, w), BF16),
                        pltpu.VMEM((n_blk // 2, w + DA_ONES_ROWS, 2 * tile), BF16),
                        pltpu.VMEM((2, 4, tile, tile), F32),
                        pltpu.VMEM((n_blk, w + DA_ONES_ROWS, tile), F32),
                        pltpu.VMEM((n_blk, w + DA_ONES_ROWS, tile), F32),
                        pltpu.VMEM((n_blk, 1, tile), F32),
                        pltpu.VMEM((n_blk, 1, tile), F32)],
        compiler_params=_params(2),
        name="diff_attention",
    )(jnp.asarray(sched_q), jnp.asarray(sched_p), slopes, lam_vecs, norm_g.reshape(w, 1),
      proj, proj, proj)


def _ret_kernel(cd_ref, intra_ref, qd_ref, kd_ref, gng_ref, gnb_ref,
                q_ref, k_ref, v_ref, g_ref, o_ref, state_ref, *, n_chunks, dk, dv):
    c_len = RET_CHUNK
    heads = range(RET_HEADS)
    state_ref[...] = jnp.zeros_like(state_ref)

    def chunk(c, _):
        rows = pl.ds(pl.multiple_of(c * c_len, c_len), c_len)
        q = [q_ref[rows, h * dk:(h + 1) * dk] for h in heads]
        k = [k_ref[rows, h * dk:(h + 1) * dk] for h in heads]
        v = [v_ref[rows, h * dv:(h + 1) * dv] for h in heads]
        state = [state_ref[h] for h in heads]
        s = [lax.dot_general(q[h], k[h], _NT, preferred_element_type=F32) for h in heads]
        cross = [jnp.dot(q[h], state[h].astype(BF16), preferred_element_type=F32) for h in heads]
        kv = [jnp.dot((k[h].astype(F32) * kd_ref[h]).T.astype(BF16), v[h],
                      preferred_element_type=F32) for h in heads]
        o = [jnp.dot((s[h] * intra_ref[h]).astype(BF16), v[h], preferred_element_type=F32)
             + cross[h] * qd_ref[h] for h in heads]
        for h in heads:
            state_ref[h] = state[h] * cd_ref[h] + kv[h]
            cols = slice(h * dv, (h + 1) * dv)
            mu = jnp.mean(o[h], axis=1, keepdims=True)
            oc = o[h] - mu
            var = jnp.mean(oc * oc, axis=1, keepdims=True)
            y = oc * lax.rsqrt(var + EPS) * gng_ref[:, cols] + gnb_ref[:, cols]
            gate = g_ref[rows, cols].astype(F32)
            y = gate * (1.0 / (1.0 + jnp.exp(-gate))) * y
            o_ref[rows, cols] = y.astype(o_ref.dtype)
        return 0

    lax.fori_loop(0, n_chunks, chunk, 0)


def _retention(proj, gn_g, gn_b, *, batch, seq, d):
    dv = d // RET_HEADS
    dk = dv // 2
    c_len = RET_CHUNK
    n_chunks = seq // c_len
    log_g = jnp.log(1.0 - 2.0 ** (-5.0 - jnp.arange(RET_HEADS, dtype=F32)))
    idx = jnp.arange(c_len, dtype=F32)
    rel = idx[:, None] - idx[None, :]
    intra = jnp.where(rel[None] >= 0, jnp.exp(rel[None] * log_g[:, None, None]), 0.0) * dk ** -0.5
    q_decay = jnp.exp((idx[None, :] + 1.0) * log_g[:, None])[:, :, None]
    k_decay = (jnp.exp((c_len - 1.0 - idx[None, :]) * log_g[:, None]) * dk ** -0.5)[:, :, None]
    chunk_decay = jnp.exp(c_len * log_g)
    hk = RET_HEADS * dk
    q_blk = 3 * d // hk
    k_blk = q_blk + 1
    v_blk = (3 * d + 2 * hk) // d
    g_blk = v_blk + 1
    kern = functools.partial(_ret_kernel, n_chunks=n_chunks, dk=dk, dv=dv)
    const3 = lambda b: (0, 0, 0)
    return pl.pallas_call(
        kern,
        grid=(batch,),
        in_specs=[pl.BlockSpec(memory_space=pltpu.SMEM),
                  pl.BlockSpec((RET_HEADS, c_len, c_len), const3),
                  pl.BlockSpec((RET_HEADS, c_len, 1), const3),
                  pl.BlockSpec((RET_HEADS, c_len, 1), const3),
                  pl.BlockSpec((1, d), lambda b: (0, 0)),
                  pl.BlockSpec((1, d), lambda b: (0, 0)),
                  pl.BlockSpec((seq, hk), lambda b: (b, q_blk)),
                  pl.BlockSpec((seq, hk), lambda b: (b, k_blk)),
                  pl.BlockSpec((seq, d), lambda b: (b, v_blk)),
                  pl.BlockSpec((seq, d), lambda b: (b, g_blk))],
        out_specs=pl.BlockSpec((seq, d), lambda b: (b, 0)),
        out_shape=jax.ShapeDtypeStruct((batch * seq, d), BF16),
        scratch_shapes=[pltpu.VMEM((RET_HEADS, dk, dv), F32)],
        compiler_params=_params(1),
        name="retention",
    )(chunk_decay, intra, q_decay, k_decay, gn_g.reshape(1, d), gn_b.reshape(1, d),
      proj, proj, proj, proj)


def _mem_kernel(q_ref, kv_ref, o_ref, *, d, dh):
    for h in range(MEM_HEADS):
        q = q_ref[:, h * dh:(h + 1) * dh]
        k = kv_ref[:, h * dh:(h + 1) * dh]
        v = kv_ref[:, d + h * dh:d + (h + 1) * dh]
        s = lax.dot_general(q, k, _NT, preferred_element_type=F32) * dh ** -0.5
        p = jnp.exp(s - jnp.max(s, axis=1, keepdims=True))
        l = jnp.sum(p, axis=1, keepdims=True)
        o = jnp.dot(p.astype(BF16), v, preferred_element_type=F32) / l
        o_ref[:, h * dh:(h + 1) * dh] = o.astype(o_ref.dtype)


def _memory_attention(proj, mem_kv, *, batch, seq, mem_len, d, tq):
    dh = d // MEM_HEADS
    tq = min(tq, seq)
    nq = seq // tq
    q_blk = 6
    kern = functools.partial(_mem_kernel, d=d, dh=dh)
    return pl.pallas_call(
        kern,
        grid=(batch, nq),
        in_specs=[pl.BlockSpec((tq, d), lambda b, i: (b * nq + i, q_blk)),
                  pl.BlockSpec((mem_len, 2 * d), lambda b, i: (b, 0))],
        out_specs=pl.BlockSpec((tq, d), lambda b, i: (b * nq + i, 0)),
        out_shape=jax.ShapeDtypeStruct((batch * seq, d), BF16),
        compiler_params=_params(2),
        name="memory_attention",
    )(proj, mem_kv)


def _merge_kernel(x_ref, yda_ref, yret_ref, ymem_ref, g0_ref, g1_ref, g2_ref, bg_ref,
                  wb_ref, wo_ref, lng_ref, lnb_ref, wrh_ref, wrl_ref, br_ref, x1_ref, lg_ref,
                  *, alpha):
    ys = (yda_ref, yret_ref, ymem_ref)
    gs = (g0_ref, g1_ref, g2_ref)
    merged = None
    for n in range(N_BRANCH):
        gate = gs[n][...].astype(F32) + bg_ref[n:n + 1, :]
        gate = 1.0 / (1.0 + jnp.exp(-gate))
        term = gate * jnp.dot(ys[n][...], wb_ref[n], preferred_element_type=F32)
        merged = term if merged is None else merged + term
    out = jnp.dot(merged.astype(BF16), wo_ref[...], preferred_element_type=F32)
    x1 = _layer_norm(alpha * x_ref[...] + out, lng_ref[...], lnb_ref[...])
    tm, n_blocks = x1.shape[0], x1.shape[1] // LANES
    _to_slabs(x1_ref, x1, tm, _slab_pitch(n_blocks))
    _zero_slab_padding(x1_ref, tm, _slab_pitch(n_blocks), n_blocks)
    x_hi = x1.astype(BF16)
    x_lo = (x1 - x_hi.astype(F32)).astype(BF16)
    lg_ref[...] = (jnp.dot(x_hi, wrh_ref[...], preferred_element_type=F32)
                   + jnp.dot(x_lo, wrh_ref[...], preferred_element_type=F32)
                   + jnp.dot(x_hi, wrl_ref[...], preferred_element_type=F32) + br_ref[...])


def _merge(x2, y_da, y_ret, y_mem, proj, b_gate, wb, wo, ln_g, ln_b, wr, br, *, alpha, tm):
    t, d = x2.shape
    tm = min(tm, t)
    gate_blk = 7
    row = lambda i: (i, 0)
    const2 = lambda i: (0, 0)
    once = pl.Buffered(1)
    x_pitch = _slab_pitch(d // LANES)
    wr_hi = wr.astype(BF16)
    wr_lo = (wr - wr_hi.astype(F32)).astype(BF16)
    kern = functools.partial(_merge_kernel, alpha=alpha)
    return pl.pallas_call(
        kern,
        grid=(t // tm,),
        in_specs=[pl.BlockSpec((tm, d), row), pl.BlockSpec((tm, d), row),
                  pl.BlockSpec((tm, d), row), pl.BlockSpec((tm, d), row),
                  pl.BlockSpec((tm, d), lambda i: (i, gate_blk)),
                  pl.BlockSpec((tm, d), lambda i: (i, gate_blk + 1)),
                  pl.BlockSpec((tm, d), lambda i: (i, gate_blk + 2)),
                  pl.BlockSpec((N_BRANCH, d), const2),
                  pl.BlockSpec((N_BRANCH, d, d), lambda i: (0, 0, 0), pipeline_mode=once),
                  pl.BlockSpec((d, d), const2, pipeline_mode=once),
                  pl.BlockSpec((1, d), const2), pl.BlockSpec((1, d), const2),
                  pl.BlockSpec((d, LANES), const2), pl.BlockSpec((d, LANES), const2),
                  pl.BlockSpec((1, LANES), const2)],
        out_specs=[pl.BlockSpec((tm * x_pitch, LANES), row), pl.BlockSpec((tm, LANES), row)],
        out_shape=[jax.ShapeDtypeStruct((t * x_pitch, LANES), F32),
                   jax.ShapeDtypeStruct((t, LANES), F32)],
        compiler_params=_params(1),
        name="merge_ln1_router",
    )(x2, y_da, y_ret, y_mem, proj, proj, proj, b_gate, wb, wo, ln_g, ln_b, wr_hi, wr_lo, br)


PAIRS = ((0, 1), (0, 2), (0, 3), (1, 2), (1, 3), (2, 3))
N_CLASSES = N_GROUPS * len(PAIRS)
ROUTE_CLS, ROUTE_RANK, ROUTE_WA, ROUTE_WB = 0, 1, 2, 3
EXPERT_ROW0 = 8
CLASS_ROWS = 32
EXPERT_GATHER_AHEAD = 2
COMBINE_GATHER_AHEAD = 2
ROUTE_TILE = 512
COMBINE_TILE = 512


def _route_kernel(lg_ref, triu_ref, route_ref, key_ref, cnt_ref, carry_ref, *, key_shift):
    i = pl.program_id(0)

    @pl.when(i == 0)
    def _():
        carry_ref[...] = jnp.zeros_like(carry_ref)

    lgt = lg_ref[...].T
    tm = lgt.shape[1]
    row = lambda r: lgt[r:r + 1, :]

    def first_max(vals):
        best = vals[0]
        for v in vals[1:]:
            best = jnp.maximum(best, v)
        idx = jnp.full_like(best, float(len(vals) - 1))
        for k in range(len(vals) - 2, -1, -1):
            idx = jnp.where(vals[k] == best, float(k), idx)
        return best, idx

    groups = [row(g) for g in range(N_GROUPS)]
    gmax, g_idx = first_max(groups)
    denom = jnp.exp(groups[0] - gmax)
    for g in groups[1:]:
        denom = denom + jnp.exp(g - gmax)
    p_top = 1.0 / denom
    sel = []
    for k in range(EXPERTS_PER_GROUP):
        v = row(EXPERT_ROW0 + (N_GROUPS - 1) * EXPERTS_PER_GROUP + k)
        for g in range(N_GROUPS - 2, -1, -1):
            v = jnp.where(g_idx == float(g), row(EXPERT_ROW0 + g * EXPERTS_PER_GROUP + k), v)
        sel.append(v)
    e1, i1 = first_max(sel)
    rest = [jnp.where(i1 == float(k), -jnp.inf, sel[k]) for k in range(EXPERTS_PER_GROUP)]
    e2, i2 = first_max(rest)
    r = jnp.exp(e2 - e1)
    w1 = p_top / (1.0 + r)
    w2 = p_top * r / (1.0 + r)

    a_loc = jnp.minimum(i1, i2)
    b_loc = jnp.maximum(i1, i2)
    pair = jnp.where(a_loc == 0.0, b_loc - 1.0, jnp.where(a_loc == 1.0, b_loc + 1.0, 5.0))
    cls = g_idx * float(len(PAIRS)) + pair
    top_is_a = i1 < i2
    w_a = jnp.where(top_is_a, w1, w2)
    w_b = jnp.where(top_is_a, w2, w1)

    cls_row = lax.broadcasted_iota(jnp.int32, (CLASS_ROWS, tm), 0).astype(F32)
    onehot = jnp.where(cls_row == cls, 1.0, 0.0)
    prefix = carry_ref[...] + jnp.dot(onehot.astype(BF16), triu_ref[...],
                                      preferred_element_type=F32)
    rank = jnp.sum(onehot * prefix, axis=0, keepdims=True) - 1.0
    total = carry_ref[...] + jnp.sum(onehot, axis=1, keepdims=True)
    carry_ref[...] = total
    cnt_ref[...] = total.astype(jnp.int32)
    key_ref[...] = (cls * float(1 << key_shift) + rank).astype(jnp.int32)
    record = jnp.concatenate([cls, rank, w_a, w_b, jnp.zeros((LANES - 4, tm), F32)], axis=0)
    route_ref[...] = record.T


def _route(logits, *, tm, key_shift):
    t = logits.shape[0]
    tm = min(tm, t)
    triu = jnp.triu(jnp.ones((tm, tm), BF16))
    return pl.pallas_call(
        functools.partial(_route_kernel, key_shift=key_shift),
        grid=(t // tm,),
        in_specs=[pl.BlockSpec((tm, LANES), lambda i: (i, 0)),
                  pl.BlockSpec((tm, tm), lambda i: (0, 0))],
        out_specs=[pl.BlockSpec((tm, LANES), lambda i: (i, 0)),
                   pl.BlockSpec((1, tm), lambda i: (0, i)),
                   pl.BlockSpec((CLASS_ROWS, 1), lambda i: (0, 0))],
        out_shape=[jax.ShapeDtypeStruct((t, LANES), F32),
                   jax.ShapeDtypeStruct((1, t), jnp.int32),
                   jax.ShapeDtypeStruct((CLASS_ROWS, 1), jnp.int32)],
        scratch_shapes=[pltpu.VMEM((CLASS_ROWS, 1), F32)],
        compiler_params=_params(1),
        name="routing",
    )(logits, triu)


def _class_starts(counts, tm):
    padded = (counts + (tm - 1)) // tm * tm
    return jnp.cumsum(padded) - padded


def _positions_kernel(starts_ref, key_ref, pos_ref, *, key_shift):
    key = key_ref[...]
    cls = lax.shift_right_logical(key, key_shift)
    pos = key & ((1 << key_shift) - 1)
    for c in range(N_CLASSES):
        pos = jnp.where(cls == c, pos + starts_ref[c], pos)
    pos_ref[...] = pos


def _positions(key, starts, *, key_shift):
    return pl.pallas_call(
        functools.partial(_positions_kernel, key_shift=key_shift),
        in_specs=[pl.BlockSpec(memory_space=pltpu.SMEM), pl.BlockSpec(memory_space=pltpu.VMEM)],
        out_specs=pl.BlockSpec(memory_space=pltpu.VMEM),
        out_shape=jax.ShapeDtypeStruct(key.shape, jnp.int32),
        name="moe_positions",
    )(starts, key)


def _plan_kernel(pos_ref, cnt_ref, src_ref, ea_ref, eb_ref, used_ref, *, n_tokens, n_tiles, tm):
    tm_shift = tm.bit_length() - 1
    row0 = jnp.int32(0)
    tile0 = jnp.int32(0)
    last_a = jnp.int32(0)
    last_b = jnp.int32(1)
    for c in range(N_CLASSES):
        e_a = c // len(PAIRS) * EXPERTS_PER_GROUP + PAIRS[c % len(PAIRS)][0]
        e_b = c // len(PAIRS) * EXPERTS_PER_GROUP + PAIRS[c % len(PAIRS)][1]
        cnt = cnt_ref[c]
        n_cls_tiles = lax.shift_right_logical(cnt + (tm - 1), tm_shift)

        def mark(i, _, e_a=e_a, e_b=e_b):
            ea_ref[i] = e_a
            eb_ref[i] = e_b
            used_ref[i] = 1
            return 0

        lax.fori_loop(tile0, tile0 + n_cls_tiles, mark, 0)
        has = n_cls_tiles > 0
        last_a = jnp.where(has, e_a, last_a)
        last_b = jnp.where(has, e_b, last_b)
        row_end = row0 + n_cls_tiles * tm

        def pad(r, _):
            src_ref[r] = 0
            return 0

        lax.fori_loop(row0 + cnt, row_end, pad, 0)
        row0 = row_end
        tile0 = tile0 + n_cls_tiles

    def unused(i, _):
        ea_ref[i] = last_a
        eb_ref[i] = last_b
        used_ref[i] = 0
        return 0

    lax.fori_loop(tile0, n_tiles, unused, 0)
    lax.fori_loop(row0, n_tiles * tm, pad, 0)

    def put(t, _):
        src_ref[pos_ref[t]] = t
        return 0

    lax.fori_loop(0, n_tokens, put, 0, unroll=8)


def _plan(pos, counts, *, tm):
    n_tokens = pos.shape[0]
    assert tm & (tm - 1) == 0
    n_tiles = n_tokens // tm + N_CLASSES - 1 + EXPERT_GATHER_AHEAD
    smem = pl.BlockSpec(memory_space=pltpu.SMEM)
    tiles = jax.ShapeDtypeStruct((n_tiles,), jnp.int32)
    src, e_a, e_b, used = pl.pallas_call(
        functools.partial(_plan_kernel, n_tokens=n_tokens, n_tiles=n_tiles, tm=tm),
        in_specs=[smem, smem], out_specs=[smem] * 4,
        out_shape=[jax.ShapeDtypeStruct((n_tiles * tm,), jnp.int32), tiles, tiles, tiles],
        name="moe_plan",
    )(pos, counts)
    return src, e_a, e_b, used, n_tiles


def _slab_pitch(n_blocks):
    return n_blocks + 4 if n_blocks % 8 == 0 else n_blocks


def _to_slabs(ref, value, n_rows, pitch, offset=0):
    for j in range(value.shape[1] // LANES):
        ref[pl.ds(offset + j, n_rows, stride=pitch), :] = value[:, j * LANES:(j + 1) * LANES]


def _zero_slab_padding(ref, n_rows, pitch, n_blocks):
    for j in range(n_blocks, pitch):
        ref[pl.ds(j, n_rows, stride=pitch), :] = jnp.zeros((n_rows, LANES), ref.dtype)


def _from_slabs(ref, n_rows, n_blocks, pitch, offset=0):
    return jnp.concatenate([ref[pl.ds(offset + j, n_rows, stride=pitch), :]
                            for j in range(n_blocks)], axis=1)


def _gather_copy(tab_ref, buf_ref, sem, row, r, rec, pitch):
    return pltpu.make_async_copy(tab_ref.at[pl.ds(pl.multiple_of(row * pitch, 4), rec), :],
                                 buf_ref.at[pl.ds(pl.multiple_of(r * pitch, 4), rec), :], sem)


def _gather_start(idx_ref, tab_ref, buf_ref, sem, tile, tm, rec, pitch):
    def issue(r, _):
        _gather_copy(tab_ref, buf_ref, sem, idx_ref[tile * tm + r], r, rec, pitch).start()
        return 0

    lax.fori_loop(0, tm, issue, 0, unroll=8)


def _gather_wait(tab_ref, buf_ref, sem, tm, rec):
    pltpu.make_async_copy(tab_ref.at[pl.ds(0, tm * rec), :], buf_ref.at[pl.ds(0, tm * rec), :],
                          sem).wait()


def _experts_kernel(src_ref, ea_ref, eb_ref, used_ref, x_ref, w1a_ref, w3a_ref, w2a_ref,
                    w1b_ref, w3b_ref, w2b_ref, o_ref, buf_ref, sem_ref, *, tm, n_tiles, n_blocks):
    i = pl.program_id(0)
    ahead = EXPERT_GATHER_AHEAD
    slot = lax.rem(i, ahead + 1)
    fill_slot = lax.rem(i + ahead, ahead + 1)
    used = lambda tile: used_ref[tile] != 0
    in_flight = (i < ahead) | used(jnp.maximum(i - ahead, 0))
    x_pitch = _slab_pitch(n_blocks)
    y_pitch = _slab_pitch(2 * n_blocks)

    @pl.when(i == 0)
    def _():
        for j in range(ahead):
            _gather_start(src_ref, x_ref, buf_ref.at[j], sem_ref.at[j], j, tm, n_blocks, x_pitch)

    @pl.when(used(i))
    def _():
        _gather_wait(x_ref, buf_ref.at[slot], sem_ref.at[slot], tm, n_blocks)
        x = _from_slabs(buf_ref.at[slot], tm, n_blocks, x_pitch).astype(BF16)
        ups = [(jnp.dot(x, w1_ref[...], preferred_element_type=F32),
                jnp.dot(x, w3_ref[...], preferred_element_type=F32))
               for w1_ref, w3_ref in ((w1a_ref, w3a_ref), (w1b_ref, w3b_ref))]
        for r in range(tm):
            _gather_copy(x_ref, buf_ref.at[fill_slot], sem_ref.at[fill_slot],
                         src_ref[(i + ahead) * tm + r], r, n_blocks, x_pitch).start()
        for (a, b), w2_ref, offset in zip(ups, (w2a_ref, w2b_ref), (0, n_blocks)):
            hid = a * (1.0 / (1.0 + jnp.exp(-a))) * b
            y = jnp.dot(hid.astype(BF16), w2_ref[...], preferred_element_type=F32)
            _to_slabs(o_ref, y, tm, y_pitch, offset=offset)
        _zero_slab_padding(o_ref, tm, y_pitch, 2 * n_blocks)

    @pl.when(jnp.logical_not(used(i)))
    def _():
        @pl.when(in_flight)
        def _():
            _gather_wait(x_ref, buf_ref.at[slot], sem_ref.at[slot], tm, n_blocks)

        o_ref[...] = jnp.zeros_like(o_ref)


def _experts(x1s, src, e_a, e_b, used, w1, w3, w2, *, tm, n_tiles):
    d, f = w1.shape[1], w1.shape[2]
    n_blocks = d // LANES
    x_pitch = _slab_pitch(n_blocks)
    y_pitch = _slab_pitch(2 * n_blocks)
    up_a = pl.BlockSpec((None, d, f), lambda i, src, ea, eb, used: (ea[i], 0, 0))
    up_b = pl.BlockSpec((None, d, f), lambda i, src, ea, eb, used: (eb[i], 0, 0))
    down_a = pl.BlockSpec((None, f, d), lambda i, src, ea, eb, used: (ea[i], 0, 0))
    down_b = pl.BlockSpec((None, f, d), lambda i, src, ea, eb, used: (eb[i], 0, 0))
    return pl.pallas_call(
        functools.partial(_experts_kernel, tm=tm, n_tiles=n_tiles, n_blocks=n_blocks),
        grid_spec=pltpu.PrefetchScalarGridSpec(
            num_scalar_prefetch=4, grid=(n_tiles,),
            in_specs=[pl.BlockSpec(memory_space=pl.ANY),
                      up_a, up_a, down_a, up_b, up_b, down_b],
            out_specs=pl.BlockSpec((tm * y_pitch, LANES), lambda i, src, ea, eb, used: (i, 0)),
            scratch_shapes=[pltpu.VMEM((EXPERT_GATHER_AHEAD + 1, tm * x_pitch, LANES), F32),
                            pltpu.SemaphoreType.DMA((EXPERT_GATHER_AHEAD + 1,))]),
        out_shape=jax.ShapeDtypeStruct((n_tiles * tm * y_pitch, LANES), F32),
        compiler_params=_params(1),
        name="experts",
    )(src, e_a, e_b, used, x1s, w1, w3, w2, w1, w3, w2)


def _combine_kernel(pos_ref, x_ref, route_ref, lng_ref, lnb_ref, ys_ref, o_ref, buf_ref, sem_ref,
                    *, tm, n_tiles, n_blocks, alpha, ahead):
    i = pl.program_id(0)
    x_pitch = _slab_pitch(n_blocks)
    y_pitch = _slab_pitch(2 * n_blocks)
    rec = 2 * n_blocks
    n_buf = ahead + 1
    slot = lax.rem(i, n_buf)

    @pl.when(i == 0)
    def _():
        for j in range(ahead):
            _gather_start(pos_ref, ys_ref, buf_ref.at[j], sem_ref.at[j], j, tm, rec, y_pitch)

    _gather_wait(ys_ref, buf_ref.at[slot], sem_ref.at[slot], tm, rec)
    ys = buf_ref.at[slot]
    y_a = _from_slabs(ys, tm, n_blocks, y_pitch)
    y_b = _from_slabs(ys, tm, n_blocks, y_pitch, offset=n_blocks)
    fill = lax.rem(i + ahead, n_buf)
    nxt = jnp.minimum(i + ahead, n_tiles - 1)
    for r in range(tm):
        _gather_copy(ys_ref, buf_ref.at[fill], sem_ref.at[fill], pos_ref[nxt * tm + r], r, rec,
                     y_pitch).start()
    route = route_ref[...]
    moe = route[:, ROUTE_WA:ROUTE_WA + 1] * y_a + route[:, ROUTE_WB:ROUTE_WB + 1] * y_b
    x1 = _from_slabs(x_ref, tm, n_blocks, x_pitch)
    o_ref[...] = _layer_norm(alpha * x1 + moe, lng_ref[...], lnb_ref[...])

    @pl.when(i == n_tiles - 1)
    def _():
        for j in range(1, ahead + 1):
            extra = lax.rem(i + j, n_buf)
            _gather_wait(ys_ref, buf_ref.at[extra], sem_ref.at[extra], tm, rec)


def _combine(x1s, route, ys, pos, ln_g, ln_b, *, alpha, tm):
    t = route.shape[0]
    d = ln_g.shape[1]
    n_blocks = d // LANES
    x_pitch = _slab_pitch(n_blocks)
    y_pitch = _slab_pitch(2 * n_blocks)
    n_tiles = t // tm
    ahead = min(COMBINE_GATHER_AHEAD, n_tiles - 1)
    return pl.pallas_call(
        functools.partial(_combine_kernel, tm=tm, n_tiles=n_tiles, n_blocks=n_blocks, alpha=alpha,
                          ahead=ahead),
        grid_spec=pltpu.PrefetchScalarGridSpec(
            num_scalar_prefetch=1, grid=(n_tiles,),
            in_specs=[pl.BlockSpec((tm * x_pitch, LANES), lambda i, pos: (i, 0)),
                      pl.BlockSpec((tm, LANES), lambda i, pos: (i, 0)),
                      pl.BlockSpec((1, d), lambda i, pos: (0, 0)),
                      pl.BlockSpec((1, d), lambda i, pos: (0, 0)),
                      pl.BlockSpec(memory_space=pl.ANY)],
            out_specs=pl.BlockSpec((tm, d), lambda i, pos: (i, 0)),
            scratch_shapes=[pltpu.VMEM((ahead + 1, tm * y_pitch, LANES), F32),
                            pltpu.SemaphoreType.DMA((ahead + 1,))]),
        out_shape=jax.ShapeDtypeStruct((t, d), F32),
        compiler_params=_params(1),
        name="moe_combine_ln2",
    )(pos, x1s, route, ln_g, ln_b, ys)


def _moe(x1s, logits, w1, w3, w2, ln_g, ln_b, *, alpha, tm):
    t = logits.shape[0]
    tm = min(tm, t)
    key_shift = max(t - 1, 1).bit_length()
    assert N_CLASSES << key_shift < 1 << 24
    route, key, counts = _route(logits, tm=ROUTE_TILE, key_shift=key_shift)
    counts = counts.reshape(CLASS_ROWS)[:N_CLASSES]
    pos = _positions(key.reshape(8, t // 8), _class_starts(counts, tm), key_shift=key_shift)
    pos = pos.reshape(t)
    src, e_a, e_b, used, n_tiles = _plan(pos, counts, tm=tm)
    ys = _experts(x1s, src, e_a, e_b, used, w1, w3, w2, tm=tm, n_tiles=n_tiles)
    return _combine(x1s, route, ys, pos, ln_g, ln_b, alpha=alpha, tm=min(COMBINE_TILE, t))


def kernel(x, mem, w_in, b_gate, w_mem_kv, da_lambda, da_norm_g, ret_gn_g, ret_gn_b, w_branch, w_o,
           ln1_g, ln1_b, w_rg, b_rg, w_re, b_re, w1, w3, w2, ln2_g, ln2_b):
    batch, seq, d = x.shape
    mem_len = mem.shape[1]
    depth = w_in.shape[0]
    alpha = (2.0 * depth) ** 0.25
    t = batch * seq
    x2 = x.reshape(t, d)
    for l in range(depth):
        lambda_init = 0.8 - 0.6 * math.exp(-0.3 * l)
        proj = _matmul(x2, w_in[l], tm=2048, tn=1024, name="in_proj")
        mem_kv = _matmul(mem.reshape(batch * mem_len, d), w_mem_kv[l],
                         tm=1024, tn=1024, name="mem_kv_proj")
        y_da = _diff_attention(proj, da_lambda[l], da_norm_g[l], batch=batch, seq=seq, d=d,
                               lambda_init=lambda_init, tile=256)
        y_ret = _retention(proj, ret_gn_g[l], ret_gn_b[l], batch=batch, seq=seq, d=d)
        y_mem = _memory_attention(proj, mem_kv, batch=batch, seq=seq, mem_len=mem_len, d=d, tq=512)
        gap = EXPERT_ROW0 - N_GROUPS
        tail = LANES - EXPERT_ROW0 - N_EXPERTS
        wr = jnp.concatenate([w_rg[l], jnp.zeros((d, gap), F32), w_re[l],
                              jnp.zeros((d, tail), F32)], axis=1)
        br = jnp.concatenate([b_rg[l], jnp.zeros((gap,), F32), b_re[l],
                              jnp.zeros((tail,), F32)]).reshape(1, LANES)
        x1, logits = _merge(x2, y_da, y_ret, y_mem, proj, b_gate[l].reshape(N_BRANCH, d),
                            w_branch[l].astype(BF16), w_o[l].astype(BF16),
                            ln1_g[l].reshape(1, d), ln1_b[l].reshape(1, d), wr, br,
                            alpha=alpha, tm=512)
        x2 = _moe(x1, logits, w1[l].astype(BF16), w3[l].astype(BF16), w2[l].astype(BF16),
                  ln2_g[l].reshape(1, d), ln2_b[l].reshape(1, d), alpha=alpha, tm=256)
    return x2.reshape(batch, seq, d)
```

```python
import functools
import math

import jax
import jax.numpy as jnp
import numpy as np
from jax import lax
from jax.experimental import pallas as pl
from jax.experimental.pallas import tpu as pltpu

BF16 = jnp.bfloat16
F32 = jnp.float32

DA_HEADS = 8
RET_HEADS = 4
RET_CHUNK = 128
MEM_HEADS = 4
N_BRANCH = 3
N_GROUPS = 4
EXPERTS_PER_GROUP = 4
N_EXPERTS = N_GROUPS * EXPERTS_PER_GROUP
EPS = 1e-5
LANES = 128
VMEM_LIMIT = 56 * 1024 * 1024

_NT = (((1,), (1,)), ((), ()))


def _params(n_axes):
    return pltpu.CompilerParams(dimension_semantics=("arbitrary",) * n_axes,
                                vmem_limit_bytes=VMEM_LIMIT)


def _layer_norm(z, g, b):
    mu = jnp.mean(z, axis=-1, keepdims=True)
    zc = z - mu
    var = jnp.mean(zc * zc, axis=-1, keepdims=True)
    return zc * lax.rsqrt(var + EPS) * g + b


def _matmul_kernel(x_ref, w_ref, o_ref):
    o_ref[...] = jnp.dot(x_ref[...].astype(BF16), w_ref[...].astype(BF16),
                         preferred_element_type=F32).astype(o_ref.dtype)


def _matmul(x, w, *, tm, tn, name):
    m, k = x.shape
    n = w.shape[1]
    tm, tn = min(tm, m), min(tn, n)
    return pl.pallas_call(
        _matmul_kernel,
        grid=(m // tm, n // tn),
        in_specs=[pl.BlockSpec((tm, k), lambda i, j: (i, 0)),
                  pl.BlockSpec((k, tn), lambda i, j: (0, j))],
        out_specs=pl.BlockSpec((tm, tn), lambda i, j: (i, j)),
        out_shape=jax.ShapeDtypeStruct((m, n), BF16),
        compiler_params=_params(2),
        name=name,
    )(x, w)


DA_ONES_ROWS = 16
DA_MASKED = -1e30
DA_CHUNK_ROWS = 32
LOG2_E = 1.4426950408889634


def _da_schedule(n_blk):
    full = [(qi, p) for qi in range(n_blk) for p in range(qi // 2)]
    order = full + [(qi, qi // 2) for qi in range(n_blk)]
    order.append(order[-1])
    table = np.asarray(order, np.int32)
    return len(full), table[:, 0], table[:, 1]


def _da_kernel(sq_ref, sp_ref, slope_ref, lam_ref, g_ref, q_ref, k_ref, v_ref, o_ref,
               ka0_ref, ka1_ref, qa0_ref, qa1_ref, vt_ref, s_ref, acc0_ref, acc1_ref, m0_ref, m1_ref,
               *, tile, dk, n_blk, n_full, lambda_init):
    h = pl.program_id(1)
    a_log2 = slope_ref[h] * LOG2_E
    w = 2 * dk
    ka_refs = (ka0_ref, ka1_ref)
    qa_refs = (qa0_ref, qa1_ref)
    acc_refs = (acc0_ref, acc1_ref)
    m_refs = (m0_ref, m1_ref)
    lane = lax.broadcasted_iota(jnp.int32, (tile, w), 1)
    pos = lax.broadcasted_iota(jnp.int32, (tile, w), 0).astype(F32)
    pieces = []
    rest = jnp.full((tile, w), a_log2, F32)
    for _ in range(3):
        piece = rest.astype(BF16).astype(F32)
        pieces.append(piece)
        rest = rest - piece
    in_comp, k_bias, q_bias = [], [], []
    for c in range(2):
        free = dk * (1 - c)
        in_comp.append((lane >= c * dk) & (lane < (c + 1) * dk))
        kb = jnp.where((lane >= free) & (lane < free + 3), pos, 0.0)
        qb = jnp.where((lane >= free + 3) & (lane < free + 6), -pos, 0.0)
        for i in range(3):
            kb = jnp.where(lane == free + 3 + i, pieces[i], kb)
            qb = jnp.where(lane == free + i, pieces[i], qb)
        k_bias.append(kb)
        q_bias.append(qb)

    def prepare(pp, _):
        for jj in range(2):
            blk = 2 * pp + jj
            rows = pl.ds(pl.multiple_of(blk * tile, tile), tile)
            kf = k_ref[rows, :].astype(F32)
            qf = q_ref[rows, :].astype(F32) * (dk ** -0.5 * LOG2_E)
            for c in range(2):
                ka_refs[c][rows, :] = jnp.where(in_comp[c], kf, k_bias[c]).astype(BF16)
                qa_refs[c][rows, :] = jnp.where(in_comp[c], qf, q_bias[c]).astype(BF16)
                acc_refs[c][blk] = jnp.zeros((w + DA_ONES_ROWS, tile), F32)
                m_refs[c][blk] = jnp.full((1, tile), -jnp.inf, F32)
            vt_ref[pp, 0:w, jj * tile:(jj + 1) * tile] = v_ref[rows, :].astype(F32).T.astype(BF16)
        vt_ref[pp, w:w + DA_ONES_ROWS, :] = jnp.ones((DA_ONES_ROWS, 2 * tile), BF16)
        return 0

    lax.fori_loop(0, n_blk // 2, prepare, 0)

    lv = lam_ref[...]
    lam = (jnp.exp(jnp.sum(lv[0:1] * lv[1:2], axis=1, keepdims=True))
           - jnp.exp(jnp.sum(lv[2:3] * lv[3:4], axis=1, keepdims=True)) + lambda_init)
    out_gain = jnp.broadcast_to(g_ref[...] * (1.0 - lambda_init), (w, tile))
    chunk = DA_CHUNK_ROWS
    n_chunks = tile // chunk
    rel0 = (lax.broadcasted_iota(jnp.int32, (chunk, tile), 0)
            - lax.broadcasted_iota(jnp.int32, (chunk, tile), 1))

    def issue_scores(t, slot, n_kv):
        qi, p = sq_ref[t], sp_ref[t]
        q_rows = pl.ds(pl.multiple_of(qi * tile, tile), tile)
        for jj in range(n_kv):
            k_rows = pl.ds(pl.multiple_of((2 * p + jj) * tile, tile), tile)
            for c in range(2):
                s_ref[slot, 2 * jj + c] = lax.dot_general(
                    ka_refs[c][k_rows, :], qa_refs[c][q_rows, :], _NT, preferred_element_type=F32)

    def consume(t, slot, diag):
        n_kv = 1 if diag == 0 else 2
        qi, p = sq_ref[t], sp_ref[t]
        offs = [a_log2 * ((2 * p + jj - qi) * tile).astype(F32) for jj in range(n_kv)]
        accs = []
        for c in range(2):
            tiles = [s_ref.at[slot, 2 * jj + c] for jj in range(n_kv)]
            m_old = m_refs[c][qi]
            m_new = m_old
            for jj in range(n_kv):
                col_max = None
                for r in range(n_chunks):
                    rows = slice(r * chunk, (r + 1) * chunk)
                    x = tiles[jj][rows, :]
                    if jj == diag:
                        x = jnp.where(rel0 <= -r * chunk, x, DA_MASKED)
                        tiles[jj][rows, :] = x
                    col_max = x if col_max is None else jnp.maximum(col_max, x)
                m_new = jnp.maximum(m_new, jnp.max(col_max, axis=0, keepdims=True) + offs[jj])
            pts = []
            for jj in range(n_kv):
                shift = m_new - offs[jj]
                for r in range(n_chunks):
                    x = tiles[jj][r * chunk:(r + 1) * chunk, :]
                    pts.append(jnp.exp2(x - shift).astype(BF16))
            vt = vt_ref[p] if n_kv == 2 else vt_ref[p, :, 0:tile]
            pv = jnp.dot(vt, jnp.concatenate(pts, axis=0), preferred_element_type=F32)
            acc = jnp.exp2(m_old - m_new) * acc_refs[c][qi] + pv
            if diag is None:
                acc_refs[c][qi] = acc
                m_refs[c][qi] = m_new
            else:
                accs.append(acc)
        if diag is not None:
            a0, a1 = accs
            ot = a0[0:w] / a0[w:w + 1] - lam * (a1[0:w] / a1[w:w + 1])
            ot = ot * lax.rsqrt(jnp.mean(ot * ot, axis=0, keepdims=True) + EPS) * out_gain
            q_rows = pl.ds(pl.multiple_of(qi * tile, tile), tile)
            o_ref[q_rows, :] = ot.T.astype(o_ref.dtype)

    def two_entries(i, base, last):
        t = base + 2 * i
        issue_scores(t + 1, 1, 2)
        consume(t, 0, 0 if last else None)
        issue_scores(t + 2, 0, 1 if last else 2)
        consume(t + 1, 1, 1 if last else None)
        return 0

    issue_scores(0, 0, 2 if n_full else 1)
    lax.fori_loop(0, n_full // 2, lambda i, _: two_entries(i, 0, False), 0)
    lax.fori_loop(0, n_blk // 2, lambda i, _: two_entries(i, n_full, True), 0)


def _diff_attention(proj, lam_vecs, norm_g, *, batch, seq, d, lambda_init, tile):
    dk = d // (2 * DA_HEADS)
    w = 2 * dk
    tile = min(tile, seq)
    assert tile <= 256 and seq % (2 * tile) == 0
    n_blk = seq // tile
    hpb = d // w
    slopes = 2.0 ** (-8.0 * (jnp.arange(DA_HEADS, dtype=F32) + 1.0) / DA_HEADS)
    n_full, sched_q, sched_p = _da_schedule(n_blk)
    kern = functools.partial(_da_kernel, tile=tile, dk=dk, n_blk=n_blk, n_full=n_full,
                             lambda_init=lambda_init)
    smem = pl.BlockSpec(memory_space=pltpu.SMEM)
    return pl.pallas_call(
        kern,
        grid=(batch, DA_HEADS),
        in_specs=[smem, smem, smem,
                  pl.BlockSpec((4, dk), lambda b, h: (0, 0)),
                  pl.BlockSpec((w, 1), lambda b, h: (0, 0)),
                  pl.BlockSpec((seq, w), lambda b, h: (b, h)),
                  pl.BlockSpec((seq, w), lambda b, h: (b, hpb + h)),
                  pl.BlockSpec((seq, w), lambda b, h: (b, 2 * hpb + h))],
        out_specs=pl.BlockSpec((seq, w), lambda b, h: (b, h)),
        out_shape=jax.ShapeDtypeStruct((batch * seq, d), BF16),
        scratch_shapes=[pltpu.VMEM((seq, w), BF16), pltpu.VMEM((seq, w), BF16),
                        pltpu.VMEM((seq, w), BF16), pltpu.VMEM((seq, w), BF16),
                        pltpu.VMEM((n_blk // 2, w + DA_ONES_ROWS, 2 * tile), BF16),
                        pltpu.VMEM((2, 4, tile, tile), F32),
                        pltpu.VMEM((n_blk, w + DA_ONES_ROWS, tile), F32),
                        pltpu.VMEM((n_blk, w + DA_ONES_ROWS, tile), F32),
                        pltpu.VMEM((n_blk, 1, tile), F32),
                        pltpu.VMEM((n_blk, 1, tile), F32)],
        compiler_params=_params(2),
        name="diff_attention",
    )(jnp.asarray(sched_q), jnp.asarray(sched_p), slopes, lam_vecs, norm_g.reshape(w, 1),
      proj, proj, proj)


def _ret_kernel(cd_ref, intra_ref, qd_ref, kd_ref, gng_ref, gnb_ref,
                q_ref, k_ref, v_ref, g_ref, o_ref, state_ref, *, n_chunks, dk, dv):
    c_len = RET_CHUNK
    heads = range(RET_HEADS)
    state_ref[...] = jnp.zeros_like(state_ref)

    def chunk(c, _):
        rows = pl.ds(pl.multiple_of(c * c_len, c_len), c_len)
        q = [q_ref[rows, h * dk:(h + 1) * dk] for h in heads]
        k = [k_ref[rows, h * dk:(h + 1) * dk] for h in heads]
        v = [v_ref[rows, h * dv:(h + 1) * dv] for h in heads]
        state = [state_ref[h] for h in heads]
        s = [lax.dot_general(q[h], k[h], _NT, preferred_element_type=F32) for h in heads]
        cross = [jnp.dot(q[h], state[h].astype(BF16), preferred_element_type=F32) for h in heads]
        kv = [jnp.dot((k[h].astype(F32) * kd_ref[h]).T.astype(BF16), v[h],
                      preferred_element_type=F32) for h in heads]
        o = [jnp.dot((s[h] * intra_ref[h]).astype(BF16), v[h], preferred_element_type=F32)
             + cross[h] * qd_ref[h] for h in heads]
        for h in heads:
            state_ref[h] = state[h] * cd_ref[h] + kv[h]
            cols = slice(h * dv, (h + 1) * dv)
            mu = jnp.mean(o[h], axis=1, keepdims=True)
            oc = o[h] - mu
            var = jnp.mean(oc * oc, axis=1, keepdims=True)
            y = oc * lax.rsqrt(var + EPS) * gng_ref[:, cols] + gnb_ref[:, cols]
            gate = g_ref[rows, cols].astype(F32)
            y = gate * (1.0 / (1.0 + jnp.exp(-gate))) * y
            o_ref[rows, cols] = y.astype(o_ref.dtype)
        return 0

    lax.fori_loop(0, n_chunks, chunk, 0)


def _retention(proj, gn_g, gn_b, *, batch, seq, d):
    dv = d // RET_HEADS
    dk = dv // 2
    c_len = RET_CHUNK
    n_chunks = seq // c_len
    log_g = jnp.log(1.0 - 2.0 ** (-5.0 - jnp.arange(RET_HEADS, dtype=F32)))
    idx = jnp.arange(c_len, dtype=F32)
    rel = idx[:, None] - idx[None, :]
    intra = jnp.where(rel[None] >= 0, jnp.exp(rel[None] * log_g[:, None, None]), 0.0) * dk ** -0.5
    q_decay = jnp.exp((idx[None, :] + 1.0) * log_g[:, None])[:, :, None]
    k_decay = (jnp.exp((c_len - 1.0 - idx[None, :]) * log_g[:, None]) * dk ** -0.5)[:, :, None]
    chunk_decay = jnp.exp(c_len * log_g)
    hk = RET_HEADS * dk
    q_blk = 3 * d // hk
    k_blk = q_blk + 1
    v_blk = (3 * d + 2 * hk) // d
    g_blk = v_blk + 1
    kern = functools.partial(_ret_kernel, n_chunks=n_chunks, dk=dk, dv=dv)
    const3 = lambda b: (0, 0, 0)
    return pl.pallas_call(
        kern,
        grid=(batch,),
        in_specs=[pl.BlockSpec(memory_space=pltpu.SMEM),
                  pl.BlockSpec((RET_HEADS, c_len, c_len), const3),
                  pl.BlockSpec((RET_HEADS, c_len, 1), const3),
                  pl.BlockSpec((RET_HEADS, c_len, 1), const3),
                  pl.BlockSpec((1, d), lambda b: (0, 0)),
                  pl.BlockSpec((1, d), lambda b: (0, 0)),
                  pl.BlockSpec((seq, hk), lambda b: (b, q_blk)),
                  pl.BlockSpec((seq, hk), lambda b: (b, k_blk)),
                  pl.BlockSpec((seq, d), lambda b: (b, v_blk)),
                  pl.BlockSpec((seq, d), lambda b: (b, g_blk))],
        out_specs=pl.BlockSpec((seq, d), lambda b: (b, 0)),
        out_shape=jax.ShapeDtypeStruct((batch * seq, d), BF16),
        scratch_shapes=[pltpu.VMEM((RET_HEADS, dk, dv), F32)],
        compiler_params=_params(1),
        name="retention",
    )(chunk_decay, intra, q_decay, k_decay, gn_g.reshape(1, d), gn_b.reshape(1, d),
      proj, proj, proj, proj)


def _mem_kernel(q_ref, kv_ref, o_ref, *, d, dh):
    for h in range(MEM_HEADS):
        q = q_ref[:, h * dh:(h + 1) * dh]
        k = kv_ref[:, h * dh:(h + 1) * dh]
        v = kv_ref[:, d + h * dh:d + (h + 1) * dh]
        s = lax.dot_general(q, k, _NT, preferred_element_type=F32) * dh ** -0.5
        p = jnp.exp(s - jnp.max(s, axis=1, keepdims=True))
        l = jnp.sum(p, axis=1, keepdims=True)
        o = jnp.dot(p.astype(BF16), v, preferred_element_type=F32) / l
        o_ref[:, h * dh:(h + 1) * dh] = o.astype(o_ref.dtype)


def _memory_attention(proj, mem_kv, *, batch, seq, mem_len, d, tq):
    dh = d // MEM_HEADS
    tq = min(tq, seq)
    nq = seq // tq
    q_blk = 6
    kern = functools.partial(_mem_kernel, d=d, dh=dh)
    return pl.pallas_call(
        kern,
        grid=(batch, nq),
        in_specs=[pl.BlockSpec((tq, d), lambda b, i: (b * nq + i, q_blk)),
                  pl.BlockSpec((mem_len, 2 * d), lambda b, i: (b, 0))],
        out_specs=pl.BlockSpec((tq, d), lambda b, i: (b * nq + i, 0)),
        out_shape=jax.ShapeDtypeStruct((batch * seq, d), BF16),
        compiler_params=_params(2),
        name="memory_attention",
    )(proj, mem_kv)


def _merge_kernel(x_ref, yda_ref, yret_ref, ymem_ref, g0_ref, g1_ref, g2_ref, bg_ref,
                  wb_ref, wo_ref, lng_ref, lnb_ref, wrh_ref, wrl_ref, br_ref, triu_ref,
                  x1_ref, key_ref, cnt_ref, carry_ref, *, alpha, key_shift):
    @pl.when(pl.program_id(0) == 0)
    def _():
        carry_ref[...] = jnp.zeros_like(carry_ref)

    ys = (yda_ref, yret_ref, ymem_ref)
    gs = (g0_ref, g1_ref, g2_ref)
    merged = None
    for n in range(N_BRANCH):
        gate = gs[n][...].astype(F32) + bg_ref[n:n + 1, :]
        gate = 1.0 / (1.0 + jnp.exp(-gate))
        term = gate * jnp.dot(ys[n][...], wb_ref[n], preferred_element_type=F32)
        merged = term if merged is None else merged + term
    out = jnp.dot(merged.astype(BF16), wo_ref[...], preferred_element_type=F32)
    x1 = _layer_norm(alpha * x_ref[...] + out, lng_ref[...], lnb_ref[...])
    tm, n_blocks = x1.shape[0], x1.shape[1] // LANES
    pitch = _slab_pitch(n_blocks)
    _to_slabs(x1_ref, x1, tm, pitch)
    x_hi = x1.astype(BF16)
    x_lo = (x1 - x_hi.astype(F32)).astype(BF16)
    logits = (jnp.dot(x_hi, wrh_ref[...], preferred_element_type=F32)
              + jnp.dot(x_lo, wrh_ref[...], preferred_element_type=F32)
              + jnp.dot(x_hi, wrl_ref[...], preferred_element_type=F32) + br_ref[...])
    record, key, total = _route_tokens(logits.T, triu_ref[...], carry_ref[...], key_shift)
    carry_ref[...] = total
    cnt_ref[...] = total.astype(jnp.int32)
    key_ref[...] = key
    x1_ref[pl.ds(n_blocks, tm, stride=pitch), :] = record
    _zero_slab_padding(x1_ref, tm, pitch, n_blocks + 1)


def _merge(x2, y_da, y_ret, y_mem, proj, b_gate, wb, wo, ln_g, ln_b, wr, br, *, alpha, tm,
           key_shift):
    t, d = x2.shape
    tm = min(tm, t)
    triu = jnp.triu(jnp.ones((tm, tm), BF16))
    gate_blk = 7
    row = lambda i: (i, 0)
    const2 = lambda i: (0, 0)
    once = pl.Buffered(1)
    x_pitch = _slab_pitch(d // LANES)
    wr_hi = wr.astype(BF16)
    wr_lo = (wr - wr_hi.astype(F32)).astype(BF16)
    kern = functools.partial(_merge_kernel, alpha=alpha, key_shift=key_shift)
    return pl.pallas_call(
        kern,
        grid=(t // tm,),
        in_specs=[pl.BlockSpec((tm, d), row), pl.BlockSpec((tm, d), row),
                  pl.BlockSpec((tm, d), row), pl.BlockSpec((tm, d), row),
                  pl.BlockSpec((tm, d), lambda i: (i, gate_blk)),
                  pl.BlockSpec((tm, d), lambda i: (i, gate_blk + 1)),
                  pl.BlockSpec((tm, d), lambda i: (i, gate_blk + 2)),
                  pl.BlockSpec((N_BRANCH, d), const2),
                  pl.BlockSpec((N_BRANCH, d, d), lambda i: (0, 0, 0), pipeline_mode=once),
                  pl.BlockSpec((d, d), const2, pipeline_mode=once),
                  pl.BlockSpec((1, d), const2), pl.BlockSpec((1, d), const2),
                  pl.BlockSpec((d, LANES), const2), pl.BlockSpec((d, LANES), const2),
                  pl.BlockSpec((1, LANES), const2),
                  pl.BlockSpec((tm, tm), const2, pipeline_mode=once)],
        out_specs=[pl.BlockSpec((tm * x_pitch, LANES), row),
                   pl.BlockSpec((1, tm), lambda i: (0, i)),
                   pl.BlockSpec((CLASS_ROWS, 1), const2)],
        out_shape=[jax.ShapeDtypeStruct((t * x_pitch, LANES), F32),
                   jax.ShapeDtypeStruct((1, t), jnp.int32),
                   jax.ShapeDtypeStruct((CLASS_ROWS, 1), jnp.int32)],
        scratch_shapes=[pltpu.VMEM((CLASS_ROWS, 1), F32)],
        compiler_params=_params(1),
        name="merge_ln1_router",
    )(x2, y_da, y_ret, y_mem, proj, proj, proj, b_gate, wb, wo, ln_g, ln_b, wr_hi, wr_lo, br,
      triu)


PAIRS = ((0, 1), (0, 2), (0, 3), (1, 2), (1, 3), (2, 3))
N_CLASSES = N_GROUPS * len(PAIRS)
ROUTE_CLS, ROUTE_RANK, ROUTE_WA, ROUTE_WB = 0, 1, 2, 3
EXPERT_ROW0 = 8
CLASS_ROWS = 32
EXPERT_GATHER_AHEAD = 2
COMBINE_GATHER_AHEAD = 2
COMBINE_TILE = 512


def _route_tokens(lgt, triu, carry, key_shift):
    tm = lgt.shape[1]
    row = lambda r: lgt[r:r + 1, :]

    def first_max(vals):
        best = vals[0]
        for v in vals[1:]:
            best = jnp.maximum(best, v)
        idx = jnp.full_like(best, float(len(vals) - 1))
        for k in range(len(vals) - 2, -1, -1):
            idx = jnp.where(vals[k] == best, float(k), idx)
        return best, idx

    groups = [row(g) for g in range(N_GROUPS)]
    gmax, g_idx = first_max(groups)
    denom = jnp.exp(groups[0] - gmax)
    for g in groups[1:]:
        denom = denom + jnp.exp(g - gmax)
    p_top = 1.0 / denom
    sel = []
    for k in range(EXPERTS_PER_GROUP):
        v = row(EXPERT_ROW0 + (N_GROUPS - 1) * EXPERTS_PER_GROUP + k)
        for g in range(N_GROUPS - 2, -1, -1):
            v = jnp.where(g_idx == float(g), row(EXPERT_ROW0 + g * EXPERTS_PER_GROUP + k), v)
        sel.append(v)
    e1, i1 = first_max(sel)
    rest = [jnp.where(i1 == float(k), -jnp.inf, sel[k]) for k in range(EXPERTS_PER_GROUP)]
    e2, i2 = first_max(rest)
    r = jnp.exp(e2 - e1)
    w1 = p_top / (1.0 + r)
    w2 = p_top * r / (1.0 + r)

    a_loc = jnp.minimum(i1, i2)
    b_loc = jnp.maximum(i1, i2)
    pair = jnp.where(a_loc == 0.0, b_loc - 1.0, jnp.where(a_loc == 1.0, b_loc + 1.0, 5.0))
    cls = g_idx * float(len(PAIRS)) + pair
    top_is_a = i1 < i2
    w_a = jnp.where(top_is_a, w1, w2)
    w_b = jnp.where(top_is_a, w2, w1)

    cls_row = lax.broadcasted_iota(jnp.int32, (CLASS_ROWS, tm), 0).astype(F32)
    onehot = jnp.where(cls_row == cls, 1.0, 0.0)
    prefix = carry + jnp.dot(onehot.astype(BF16), triu, preferred_element_type=F32)
    rank = jnp.sum(onehot * prefix, axis=0, keepdims=True) - 1.0
    total = carry + jnp.sum(onehot, axis=1, keepdims=True)
    key = (cls * float(1 << key_shift) + rank).astype(jnp.int32)
    record = jnp.concatenate([cls, rank, w_a, w_b, jnp.zeros((LANES - 4, tm), F32)], axis=0)
    return record.T, key, total


def _class_starts(counts, tm):
    padded = (counts + (tm - 1)) // tm * tm
    return jnp.cumsum(padded) - padded


def _positions_kernel(starts_ref, key_ref, pos_ref, *, key_shift):
    key = key_ref[...]
    cls = lax.shift_right_logical(key, key_shift)
    pos = key & ((1 << key_shift) - 1)
    for c in range(N_CLASSES):
        pos = jnp.where(cls == c, pos + starts_ref[c], pos)
    pos_ref[...] = pos


def _positions(key, starts, *, key_shift):
    return pl.pallas_call(
        functools.partial(_positions_kernel, key_shift=key_shift),
        in_specs=[pl.BlockSpec(memory_space=pltpu.SMEM), pl.BlockSpec(memory_space=pltpu.VMEM)],
        out_specs=pl.BlockSpec(memory_space=pltpu.VMEM),
        out_shape=jax.ShapeDtypeStruct(key.shape, jnp.int32),
        name="moe_positions",
    )(starts, key)


def _plan_kernel(pos_ref, cnt_ref, src_ref, ea_ref, eb_ref, used_ref, *, n_tokens, n_tiles, tm):
    tm_shift = tm.bit_length() - 1
    row0 = jnp.int32(0)
    tile0 = jnp.int32(0)
    last_a = jnp.int32(0)
    last_b = jnp.int32(1)
    for c in range(N_CLASSES):
        e_a = c // len(PAIRS) * EXPERTS_PER_GROUP + PAIRS[c % len(PAIRS)][0]
        e_b = c // len(PAIRS) * EXPERTS_PER_GROUP + PAIRS[c % len(PAIRS)][1]
        cnt = cnt_ref[c]
        n_cls_tiles = lax.shift_right_logical(cnt + (tm - 1), tm_shift)

        def mark(i, _, e_a=e_a, e_b=e_b):
            ea_ref[i] = e_a
            eb_ref[i] = e_b
            used_ref[i] = 1
            return 0

        lax.fori_loop(tile0, tile0 + n_cls_tiles, mark, 0)
        has = n_cls_tiles > 0
        last_a = jnp.where(has, e_a, last_a)
        last_b = jnp.where(has, e_b, last_b)
        row_end = row0 + n_cls_tiles * tm

        def pad(r, _):
            src_ref[r] = 0
            return 0

        lax.fori_loop(row0 + cnt, row_end, pad, 0)
        row0 = row_end
        tile0 = tile0 + n_cls_tiles

    def unused(i, _):
        ea_ref[i] = last_a
        eb_ref[i] = last_b
        used_ref[i] = 0
        return 0

    lax.fori_loop(tile0, n_tiles, unused, 0)
    lax.fori_loop(row0, n_tiles * tm, pad, 0)

    def put(t, _):
        src_ref[pos_ref[t]] = t
        return 0

    lax.fori_loop(0, n_tokens, put, 0, unroll=8)


def _plan(pos, counts, *, tm):
    n_tokens = pos.shape[0]
    assert tm & (tm - 1) == 0
    n_tiles = n_tokens // tm + N_CLASSES - 1 + EXPERT_GATHER_AHEAD
    smem = pl.BlockSpec(memory_space=pltpu.SMEM)
    tiles = jax.ShapeDtypeStruct((n_tiles,), jnp.int32)
    src, e_a, e_b, used = pl.pallas_call(
        functools.partial(_plan_kernel, n_tokens=n_tokens, n_tiles=n_tiles, tm=tm),
        in_specs=[smem, smem], out_specs=[smem] * 4,
        out_shape=[jax.ShapeDtypeStruct((n_tiles * tm,), jnp.int32), tiles, tiles, tiles],
        name="moe_plan",
    )(pos, counts)
    return src, e_a, e_b, used, n_tiles


def _slab_pitch(n_blocks):
    return n_blocks + 4 if n_blocks % 8 == 0 else n_blocks


def _to_slabs(ref, value, n_rows, pitch, offset=0):
    for j in range(value.shape[1] // LANES):
        ref[pl.ds(offset + j, n_rows, stride=pitch), :] = value[:, j * LANES:(j + 1) * LANES]


def _zero_slab_padding(ref, n_rows, pitch, n_blocks):
    for j in range(n_blocks, pitch):
        ref[pl.ds(j, n_rows, stride=pitch), :] = jnp.zeros((n_rows, LANES), ref.dtype)


def _from_slabs(ref, n_rows, n_blocks, pitch, offset=0):
    return jnp.concatenate([ref[pl.ds(offset + j, n_rows, stride=pitch), :]
                            for j in range(n_blocks)], axis=1)


def _gather_copy(tab_ref, buf_ref, sem, row, r, rec, pitch):
    return pltpu.make_async_copy(tab_ref.at[pl.ds(pl.multiple_of(row * pitch, 4), rec), :],
                                 buf_ref.at[pl.ds(pl.multiple_of(r * pitch, 4), rec), :], sem)


def _gather_start(idx_ref, tab_ref, buf_ref, sem, tile, tm, rec, pitch):
    def issue(r, _):
        _gather_copy(tab_ref, buf_ref, sem, idx_ref[tile * tm + r], r, rec, pitch).start()
        return 0

    lax.fori_loop(0, tm, issue, 0, unroll=8)


def _gather_wait(tab_ref, buf_ref, sem, tm, rec):
    pltpu.make_async_copy(tab_ref.at[pl.ds(0, tm * rec), :], buf_ref.at[pl.ds(0, tm * rec), :],
                          sem).wait()


def _experts_kernel(src_ref, ea_ref, eb_ref, used_ref, x_ref, w1a_ref, w3a_ref, w2a_ref,
                    w1b_ref, w3b_ref, w2b_ref, o_ref, buf_ref, sem_ref, *, tm, n_tiles, n_blocks):
    i = pl.program_id(0)
    ahead = EXPERT_GATHER_AHEAD
    slot = lax.rem(i, ahead + 1)
    fill_slot = lax.rem(i + ahead, ahead + 1)
    used = lambda tile: used_ref[tile] != 0
    in_flight = (i < ahead) | used(jnp.maximum(i - ahead, 0))
    pitch = _slab_pitch(n_blocks)
    rec = n_blocks + 1

    @pl.when(i == 0)
    def _():
        for j in range(ahead):
            _gather_start(src_ref, x_ref, buf_ref.at[j], sem_ref.at[j], j, tm, rec, pitch)

    @pl.when(used(i))
    def _():
        _gather_wait(x_ref, buf_ref.at[slot], sem_ref.at[slot], tm, rec)
        x = _from_slabs(buf_ref.at[slot], tm, n_blocks, pitch).astype(BF16)
        record = buf_ref[slot, pl.ds(n_blocks, tm, stride=pitch), :]
        ups = [(jnp.dot(x, w1_ref[...], preferred_element_type=F32),
                jnp.dot(x, w3_ref[...], preferred_element_type=F32))
               for w1_ref, w3_ref in ((w1a_ref, w3a_ref), (w1b_ref, w3b_ref))]
        for r in range(tm):
            _gather_copy(x_ref, buf_ref.at[fill_slot], sem_ref.at[fill_slot],
                         src_ref[(i + ahead) * tm + r], r, rec, pitch).start()
        y = None
        for (a, b), w2_ref, lane in zip(ups, (w2a_ref, w2b_ref), (ROUTE_WA, ROUTE_WB)):
            hid = a * (1.0 / (1.0 + jnp.exp(-a))) * b
            term = record[:, lane:lane + 1] * jnp.dot(hid.astype(BF16), w2_ref[...],
                                                      preferred_element_type=F32)
            y = term if y is None else y + term
        _to_slabs(o_ref, y, tm, pitch)
        _zero_slab_padding(o_ref, tm, pitch, n_blocks)

    @pl.when(jnp.logical_not(used(i)))
    def _():
        @pl.when(in_flight)
        def _():
            _gather_wait(x_ref, buf_ref.at[slot], sem_ref.at[slot], tm, rec)

        o_ref[...] = jnp.zeros_like(o_ref)


def _experts(x1s, src, e_a, e_b, used, w1, w3, w2, *, tm, n_tiles):
    d, f = w1.shape[1], w1.shape[2]
    n_blocks = d // LANES
    pitch = _slab_pitch(n_blocks)
    up_a = pl.BlockSpec((None, d, f), lambda i, src, ea, eb, used: (ea[i], 0, 0))
    up_b = pl.BlockSpec((None, d, f), lambda i, src, ea, eb, used: (eb[i], 0, 0))
    down_a = pl.BlockSpec((None, f, d), lambda i, src, ea, eb, used: (ea[i], 0, 0))
    down_b = pl.BlockSpec((None, f, d), lambda i, src, ea, eb, used: (eb[i], 0, 0))
    return pl.pallas_call(
        functools.partial(_experts_kernel, tm=tm, n_tiles=n_tiles, n_blocks=n_blocks),
        grid_spec=pltpu.PrefetchScalarGridSpec(
            num_scalar_prefetch=4, grid=(n_tiles,),
            in_specs=[pl.BlockSpec(memory_space=pl.ANY),
                      up_a, up_a, down_a, up_b, up_b, down_b],
            out_specs=pl.BlockSpec((tm * pitch, LANES), lambda i, src, ea, eb, used: (i, 0)),
            scratch_shapes=[pltpu.VMEM((EXPERT_GATHER_AHEAD + 1, tm * pitch, LANES), F32),
                            pltpu.SemaphoreType.DMA((EXPERT_GATHER_AHEAD + 1,))]),
        out_shape=jax.ShapeDtypeStruct((n_tiles * tm * pitch, LANES), F32),
        compiler_params=_params(1),
        name="experts",
    )(src, e_a, e_b, used, x1s, w1, w3, w2, w1, w3, w2)


def _combine_kernel(pos_ref, x_ref, lng_ref, lnb_ref, ys_ref, o_ref, buf_ref, sem_ref,
                    *, tm, n_tiles, n_blocks, alpha, ahead):
    i = pl.program_id(0)
    pitch = _slab_pitch(n_blocks)
    rec = n_blocks
    n_buf = ahead + 1
    slot = lax.rem(i, n_buf)

    @pl.when(i == 0)
    def _():
        for j in range(max(ahead, 1)):
            _gather_start(pos_ref, ys_ref, buf_ref.at[j], sem_ref.at[j], j, tm, rec, pitch)

    _gather_wait(ys_ref, buf_ref.at[slot], sem_ref.at[slot], tm, rec)
    moe = _from_slabs(buf_ref.at[slot], tm, n_blocks, pitch)
    fill = lax.rem(i + ahead, n_buf)
    nxt = jnp.minimum(i + ahead, n_tiles - 1)
    for r in range(tm if ahead else 0):
        _gather_copy(ys_ref, buf_ref.at[fill], sem_ref.at[fill], pos_ref[nxt * tm + r], r, rec,
                     pitch).start()
    x1 = _from_slabs(x_ref, tm, n_blocks, pitch)
    o_ref[...] = _layer_norm(alpha * x1 + moe, lng_ref[...], lnb_ref[...])

    @pl.when(i == n_tiles - 1)
    def _():
        for j in range(1, ahead + 1):
            extra = lax.rem(i + j, n_buf)
            _gather_wait(ys_ref, buf_ref.at[extra], sem_ref.at[extra], tm, rec)


def _combine(x1s, ys, pos, ln_g, ln_b, *, alpha, tm):
    t = pos.shape[0]
    d = ln_g.shape[1]
    n_blocks = d // LANES
    pitch = _slab_pitch(n_blocks)
    n_tiles = t // tm
    ahead = min(COMBINE_GATHER_AHEAD, n_tiles - 1)
    return pl.pallas_call(
        functools.partial(_combine_kernel, tm=tm, n_tiles=n_tiles, n_blocks=n_blocks, alpha=alpha,
                          ahead=ahead),
        grid_spec=pltpu.PrefetchScalarGridSpec(
            num_scalar_prefetch=1, grid=(n_tiles,),
            in_specs=[pl.BlockSpec((tm * pitch, LANES), lambda i, pos: (i, 0)),
                      pl.BlockSpec((1, d), lambda i, pos: (0, 0)),
                      pl.BlockSpec((1, d), lambda i, pos: (0, 0)),
                      pl.BlockSpec(memory_space=pl.ANY)],
            out_specs=pl.BlockSpec((tm, d), lambda i, pos: (i, 0)),
            scratch_shapes=[pltpu.VMEM((ahead + 1, tm * pitch, LANES), F32),
                            pltpu.SemaphoreType.DMA((ahead + 1,))]),
        out_shape=jax.ShapeDtypeStruct((t, d), F32),
        compiler_params=_params(1),
        name="moe_combine_ln2",
    )(pos, x1s, ln_g, ln_b, ys)


def _moe(x1s, key, counts, w1, w3, w2, ln_g, ln_b, *, alpha, tm, key_shift):
    t = key.shape[1]
    tm = min(tm, t)
    counts = counts.reshape(CLASS_ROWS)[:N_CLASSES]
    pos = _positions(key.reshape(8, t // 8), _class_starts(counts, tm), key_shift=key_shift)
    pos = pos.reshape(t)
    src, e_a, e_b, used, n_tiles = _plan(pos, counts, tm=tm)
    ys = _experts(x1s, src, e_a, e_b, used, w1, w3, w2, tm=tm, n_tiles=n_tiles)
    return _combine(x1s, ys, pos, ln_g, ln_b, alpha=alpha, tm=min(COMBINE_TILE, t))


def kernel(x, mem, w_in, b_gate, w_mem_kv, da_lambda, da_norm_g, ret_gn_g, ret_gn_b, w_branch, w_o,
           ln1_g, ln1_b, w_rg, b_rg, w_re, b_re, w1, w3, w2, ln2_g, ln2_b):
    batch, seq, d = x.shape
    mem_len = mem.shape[1]
    depth = w_in.shape[0]
    alpha = (2.0 * depth) ** 0.25
    t = batch * seq
    x2 = x.reshape(t, d)
    for l in range(depth):
        lambda_init = 0.8 - 0.6 * math.exp(-0.3 * l)
        proj = _matmul(x2, w_in[l], tm=2048, tn=1024, name="in_proj")
        mem_kv = _matmul(mem.reshape(batch * mem_len, d), w_mem_kv[l],
                         tm=1024, tn=1024, name="mem_kv_proj")
        y_da = _diff_attention(proj, da_lambda[l], da_norm_g[l], batch=batch, seq=seq, d=d,
                               lambda_init=lambda_init, tile=256)
        y_ret = _retention(proj, ret_gn_g[l], ret_gn_b[l], batch=batch, seq=seq, d=d)
        y_mem = _memory_attention(proj, mem_kv, batch=batch, seq=seq, mem_len=mem_len, d=d, tq=512)
        gap = EXPERT_ROW0 - N_GROUPS
        tail = LANES - EXPERT_ROW0 - N_EXPERTS
        wr = jnp.concatenate([w_rg[l], jnp.zeros((d, gap), F32), w_re[l],
                              jnp.zeros((d, tail), F32)], axis=1)
        br = jnp.concatenate([b_rg[l], jnp.zeros((gap,), F32), b_re[l],
                              jnp.zeros((tail,), F32)]).reshape(1, LANES)
        key_shift = max(t - 1, 1).bit_length()
        assert N_CLASSES << key_shift < 1 << 24
        x1s, key, counts = _merge(x2, y_da, y_ret, y_mem, proj, b_gate[l].reshape(N_BRANCH, d),
                                  w_branch[l].astype(BF16), w_o[l].astype(BF16),
                                  ln1_g[l].reshape(1, d), ln1_b[l].reshape(1, d), wr, br,
                                  alpha=alpha, tm=512, key_shift=key_shift)
        x2 = _moe(x1s, key, counts, w1[l].astype(BF16), w3[l].astype(BF16), w2[l].astype(BF16),
                  ln2_g[l].reshape(1, d), ln2_b[l].reshape(1, d), alpha=alpha, tm=256,
                  key_shift=key_shift)
    return x2.reshape(batch, seq, d)
```

```python
import functools
import math

import jax
import jax.numpy as jnp
import numpy as np
from jax import lax
from jax.experimental import pallas as pl
from jax.experimental.pallas import tpu as pltpu

BF16 = jnp.bfloat16
F32 = jnp.float32

DA_HEADS = 8
RET_HEADS = 4
RET_CHUNK = 128
MEM_HEADS = 4
N_BRANCH = 3
N_GROUPS = 4
EXPERTS_PER_GROUP = 4
N_EXPERTS = N_GROUPS * EXPERTS_PER_GROUP
EPS = 1e-5
LANES = 128
VMEM_LIMIT = 56 * 1024 * 1024

_NT = (((1,), (1,)), ((), ()))


def _params(n_axes):
    return pltpu.CompilerParams(dimension_semantics=("arbitrary",) * n_axes,
                                vmem_limit_bytes=VMEM_LIMIT)


def _layer_norm(z, g, b):
    mu = jnp.mean(z, axis=-1, keepdims=True)
    zc = z - mu
    var = jnp.mean(zc * zc, axis=-1, keepdims=True)
    return zc * lax.rsqrt(var + EPS) * g + b


def _matmul_kernel(x_ref, w_ref, o_ref):
    o_ref[...] = jnp.dot(x_ref[...].astype(BF16), w_ref[...].astype(BF16),
                         preferred_element_type=F32).astype(o_ref.dtype)


def _matmul(x, w, *, tm, tn, name):
    m, k = x.shape
    n = w.shape[1]
    tm, tn = min(tm, m), min(tn, n)
    return pl.pallas_call(
        _matmul_kernel,
        grid=(m // tm, n // tn),
        in_specs=[pl.BlockSpec((tm, k), lambda i, j: (i, 0)),
                  pl.BlockSpec((k, tn), lambda i, j: (0, j))],
        out_specs=pl.BlockSpec((tm, tn), lambda i, j: (i, j)),
        out_shape=jax.ShapeDtypeStruct((m, n), BF16),
        compiler_params=_params(2),
        name=name,
    )(x, w)


DA_ONES_ROWS = 16
DA_MASKED = -1e30
DA_CHUNK_ROWS = 32
LOG2_E = 1.4426950408889634


def _da_schedule(n_blk):
    full = [(qi, p) for qi in range(n_blk) for p in range(qi // 2)]
    order = full + [(qi, qi // 2) for qi in range(n_blk)]
    order.append(order[-1])
    table = np.asarray(order, np.int32)
    return len(full), table[:, 0], table[:, 1]


def _da_kernel(sq_ref, sp_ref, slope_ref, lam_ref, g_ref, q_ref, k_ref, v_ref, o_ref,
               ka0_ref, ka1_ref, qa0_ref, qa1_ref, vt_ref, s_ref, acc0_ref, acc1_ref, m0_ref, m1_ref,
               *, tile, dk, n_blk, n_full, lambda_init):
    h = pl.program_id(1)
    a_log2 = slope_ref[h] * LOG2_E
    w = 2 * dk
    ka_refs = (ka0_ref, ka1_ref)
    qa_refs = (qa0_ref, qa1_ref)
    acc_refs = (acc0_ref, acc1_ref)
    m_refs = (m0_ref, m1_ref)
    lane = lax.broadcasted_iota(jnp.int32, (tile, w), 1)
    pos = lax.broadcasted_iota(jnp.int32, (tile, w), 0).astype(F32)
    pieces = []
    rest = jnp.full((tile, w), a_log2, F32)
    for _ in range(3):
        piece = rest.astype(BF16).astype(F32)
        pieces.append(piece)
        rest = rest - piece
    in_comp, k_bias, q_bias = [], [], []
    for c in range(2):
        free = dk * (1 - c)
        in_comp.append((lane >= c * dk) & (lane < (c + 1) * dk))
        kb = jnp.where((lane >= free) & (lane < free + 3), pos, 0.0)
        qb = jnp.where((lane >= free + 3) & (lane < free + 6), -pos, 0.0)
        for i in range(3):
            kb = jnp.where(lane == free + 3 + i, pieces[i], kb)
            qb = jnp.where(lane == free + i, pieces[i], qb)
        k_bias.append(kb)
        q_bias.append(qb)

    def prepare(pp, _):
        for jj in range(2):
            blk = 2 * pp + jj
            rows = pl.ds(pl.multiple_of(blk * tile, tile), tile)
            kf = k_ref[rows, :].astype(F32)
            qf = q_ref[rows, :].astype(F32) * (dk ** -0.5 * LOG2_E)
            for c in range(2):
                ka_refs[c][rows, :] = jnp.where(in_comp[c], kf, k_bias[c]).astype(BF16)
                qa_refs[c][rows, :] = jnp.where(in_comp[c], qf, q_bias[c]).astype(BF16)
                acc_refs[c][blk] = jnp.zeros((w + DA_ONES_ROWS, tile), F32)
                m_refs[c][blk] = jnp.full((1, tile), -jnp.inf, F32)
            vt_ref[pp, 0:w, jj * tile:(jj + 1) * tile] = v_ref[rows, :].astype(F32).T.astype(BF16)
        vt_ref[pp, w:w + DA_ONES_ROWS, :] = jnp.ones((DA_ONES_ROWS, 2 * tile), BF16)
        return 0

    lax.fori_loop(0, n_blk // 2, prepare, 0)

    lv = lam_ref[...]
    lam = (jnp.exp(jnp.sum(lv[0:1] * lv[1:2], axis=1, keepdims=True))
           - jnp.exp(jnp.sum(lv[2:3] * lv[3:4], axis=1, keepdims=True)) + lambda_init)
    out_gain = jnp.broadcast_to(g_ref[...] * (1.0 - lambda_init), (w, tile))
    chunk = DA_CHUNK_ROWS
    n_chunks = tile // chunk
    rel0 = (lax.broadcasted_iota(jnp.int32, (chunk, tile), 0)
            - lax.broadcasted_iota(jnp.int32, (chunk, tile), 1))

    def issue_scores(t, slot, n_kv):
        qi, p = sq_ref[t], sp_ref[t]
        q_rows = pl.ds(pl.multiple_of(qi * tile, tile), tile)
        for jj in range(n_kv):
            k_rows = pl.ds(pl.multiple_of((2 * p + jj) * tile, tile), tile)
            for c in range(2):
                s_ref[slot, 2 * jj + c] = lax.dot_general(
                    ka_refs[c][k_rows, :], qa_refs[c][q_rows, :], _NT, preferred_element_type=F32)

    def consume(t, slot, diag):
        n_kv = 1 if diag == 0 else 2
        qi, p = sq_ref[t], sp_ref[t]
        offs = [a_log2 * ((2 * p + jj - qi) * tile).astype(F32) for jj in range(n_kv)]
        accs = []
        for c in range(2):
            tiles = [s_ref.at[slot, 2 * jj + c] for jj in range(n_kv)]
            m_old = m_refs[c][qi]
            m_new = m_old
            for jj in range(n_kv):
                col_max = None
                for r in range(n_chunks):
                    rows = slice(r * chunk, (r + 1) * chunk)
                    x = tiles[jj][rows, :]
                    if jj == diag:
                        x = jnp.where(rel0 <= -r * chunk, x, DA_MASKED)
                        tiles[jj][rows, :] = x
                    col_max = x if col_max is None else jnp.maximum(col_max, x)
                m_new = jnp.maximum(m_new, jnp.max(col_max, axis=0, keepdims=True) + offs[jj])
            pts = []
            for jj in range(n_kv):
                shift = m_new - offs[jj]
                for r in range(n_chunks):
                    x = tiles[jj][r * chunk:(r + 1) * chunk, :]
                    pts.append(jnp.exp2(x - shift).astype(BF16))
            vt = vt_ref[p] if n_kv == 2 else vt_ref[p, :, 0:tile]
            pv = jnp.dot(vt, jnp.concatenate(pts, axis=0), preferred_element_type=F32)
            acc = jnp.exp2(m_old - m_new) * acc_refs[c][qi] + pv
            if diag is None:
                acc_refs[c][qi] = acc
                m_refs[c][qi] = m_new
            else:
                accs.append(acc)
        if diag is not None:
            a0, a1 = accs
            ot = a0[0:w] / a0[w:w + 1] - lam * (a1[0:w] / a1[w:w + 1])
            ot = ot * lax.rsqrt(jnp.mean(ot * ot, axis=0, keepdims=True) + EPS) * out_gain
            q_rows = pl.ds(pl.multiple_of(qi * tile, tile), tile)
            o_ref[q_rows, :] = ot.T.astype(o_ref.dtype)

    def two_entries(i, base, last):
        t = base + 2 * i
        issue_scores(t + 1, 1, 2)
        consume(t, 0, 0 if last else None)
        issue_scores(t + 2, 0, 1 if last else 2)
        consume(t + 1, 1, 1 if last else None)
        return 0

    issue_scores(0, 0, 2 if n_full else 1)
    lax.fori_loop(0, n_full // 2, lambda i, _: two_entries(i, 0, False), 0)
    lax.fori_loop(0, n_blk // 2, lambda i, _: two_entries(i, n_full, True), 0)


def _diff_attention(proj, lam_vecs, norm_g, *, batch, seq, d, lambda_init, tile):
    dk = d // (2 * DA_HEADS)
    w = 2 * dk
    tile = min(tile, seq)
    assert tile <= 256 and seq % (2 * tile) == 0
    n_blk = seq // tile
    hpb = d // w
    slopes = 2.0 ** (-8.0 * (jnp.arange(DA_HEADS, dtype=F32) + 1.0) / DA_HEADS)
    n_full, sched_q, sched_p = _da_schedule(n_blk)
    kern = functools.partial(_da_kernel, tile=tile, dk=dk, n_blk=n_blk, n_full=n_full,
                             lambda_init=lambda_init)
    smem = pl.BlockSpec(memory_space=pltpu.SMEM)
    return pl.pallas_call(
        kern,
        grid=(batch, DA_HEADS),
        in_specs=[smem, smem, smem,
                  pl.BlockSpec((4, dk), lambda b, h: (0, 0)),
                  pl.BlockSpec((w, 1), lambda b, h: (0, 0)),
                  pl.BlockSpec((seq, w), lambda b, h: (b, h)),
                  pl.BlockSpec((seq, w), lambda b, h: (b, hpb + h)),
                  pl.BlockSpec((seq, w), lambda b, h: (b, 2 * hpb + h))],
        out_specs=pl.BlockSpec((seq, w), lambda b, h: (b, h)),
        out_shape=jax.ShapeDtypeStruct((batch * seq, d), BF16),
        scratch_shapes=[pltpu.VMEM((seq, w), BF16), pltpu.VMEM((seq, w), BF16),
                        pltpu.VMEM((seq, w), BF16), pltpu.VMEM((seq, w), BF16),
                        pltpu.VMEM((n_blk // 2, w + DA_ONES_ROWS, 2 * tile), BF16),
                        pltpu.VMEM((2, 4, tile, tile), F32),
                        pltpu.VMEM((n_blk, w + DA_ONES_ROWS, tile), F32),
                        pltpu.VMEM((n_blk, w + DA_ONES_ROWS, tile), F32),
                        pltpu.VMEM((n_blk, 1, tile), F32),
                        pltpu.VMEM((n_blk, 1, tile), F32)],
        compiler_params=_params(2),
        name="diff_attention",
    )(jnp.asarray(sched_q), jnp.asarray(sched_p), slopes, lam_vecs, norm_g.reshape(w, 1),
      proj, proj, proj)


def _ret_kernel(cd_ref, intra_ref, qd_ref, kd_ref, gng_ref, gnb_ref,
                q_ref, k_ref, v_ref, g_ref, o_ref, state_ref, *, n_chunks, dk, dv):
    c_len = RET_CHUNK
    heads = range(RET_HEADS)
    state_ref[...] = jnp.zeros_like(state_ref)

    def chunk(c, _):
        rows = pl.ds(pl.multiple_of(c * c_len, c_len), c_len)
        q = [q_ref[rows, h * dk:(h + 1) * dk] for h in heads]
        k = [k_ref[rows, h * dk:(h + 1) * dk] for h in heads]
        v = [v_ref[rows, h * dv:(h + 1) * dv] for h in heads]
        state = [state_ref[h] for h in heads]
        s = [lax.dot_general(q[h], k[h], _NT, preferred_element_type=F32) for h in heads]
        cross = [jnp.dot(q[h], state[h].astype(BF16), preferred_element_type=F32) for h in heads]
        kv = [jnp.dot((k[h].astype(F32) * kd_ref[h]).T.astype(BF16), v[h],
                      preferred_element_type=F32) for h in heads]
        o = [jnp.dot((s[h] * intra_ref[h]).astype(BF16), v[h], preferred_element_type=F32)
             + cross[h] * qd_ref[h] for h in heads]
        for h in heads:
            state_ref[h] = state[h] * cd_ref[h] + kv[h]
            cols = slice(h * dv, (h + 1) * dv)
            mu = jnp.mean(o[h], axis=1, keepdims=True)
            oc = o[h] - mu
            var = jnp.mean(oc * oc, axis=1, keepdims=True)
            y = oc * lax.rsqrt(var + EPS) * gng_ref[:, cols] + gnb_ref[:, cols]
            gate = g_ref[rows, cols].astype(F32)
            y = gate * (1.0 / (1.0 + jnp.exp(-gate))) * y
            o_ref[rows, cols] = y.astype(o_ref.dtype)
        return 0

    lax.fori_loop(0, n_chunks, chunk, 0, unroll=2)


def _retention(proj, gn_g, gn_b, *, batch, seq, d):
    dv = d // RET_HEADS
    dk = dv // 2
    c_len = RET_CHUNK
    n_chunks = seq // c_len
    log_g = jnp.log(1.0 - 2.0 ** (-5.0 - jnp.arange(RET_HEADS, dtype=F32)))
    idx = jnp.arange(c_len, dtype=F32)
    rel = idx[:, None] - idx[None, :]
    intra = jnp.where(rel[None] >= 0, jnp.exp(rel[None] * log_g[:, None, None]), 0.0) * dk ** -0.5
    q_decay = jnp.exp((idx[None, :] + 1.0) * log_g[:, None])[:, :, None]
    k_decay = (jnp.exp((c_len - 1.0 - idx[None, :]) * log_g[:, None]) * dk ** -0.5)[:, :, None]
    chunk_decay = jnp.exp(c_len * log_g)
    hk = RET_HEADS * dk
    q_blk = 3 * d // hk
    k_blk = q_blk + 1
    v_blk = (3 * d + 2 * hk) // d
    g_blk = v_blk + 1
    kern = functools.partial(_ret_kernel, n_chunks=n_chunks, dk=dk, dv=dv)
    const3 = lambda b: (0, 0, 0)
    return pl.pallas_call(
        kern,
        grid=(batch,),
        in_specs=[pl.BlockSpec(memory_space=pltpu.SMEM),
                  pl.BlockSpec((RET_HEADS, c_len, c_len), const3),
                  pl.BlockSpec((RET_HEADS, c_len, 1), const3),
                  pl.BlockSpec((RET_HEADS, c_len, 1), const3),
                  pl.BlockSpec((1, d), lambda b: (0, 0)),
                  pl.BlockSpec((1, d), lambda b: (0, 0)),
                  pl.BlockSpec((seq, hk), lambda b: (b, q_blk)),
                  pl.BlockSpec((seq, hk), lambda b: (b, k_blk)),
                  pl.BlockSpec((seq, d), lambda b: (b, v_blk)),
                  pl.BlockSpec((seq, d), lambda b: (b, g_blk))],
        out_specs=pl.BlockSpec((seq, d), lambda b: (b, 0)),
        out_shape=jax.ShapeDtypeStruct((batch * seq, d), BF16),
        scratch_shapes=[pltpu.VMEM((RET_HEADS, dk, dv), F32)],
        compiler_params=_params(1),
        name="retention",
    )(chunk_decay, intra, q_decay, k_decay, gn_g.reshape(1, d), gn_b.reshape(1, d),
      proj, proj, proj, proj)


def _mem_kernel(q_ref, kv_ref, o_ref, *, d, dh):
    for h in range(MEM_HEADS):
        q = q_ref[:, h * dh:(h + 1) * dh]
        k = kv_ref[:, h * dh:(h + 1) * dh]
        v = kv_ref[:, d + h * dh:d + (h + 1) * dh]
        s = lax.dot_general(q, k, _NT, preferred_element_type=F32) * dh ** -0.5
        p = jnp.exp(s - jnp.max(s, axis=1, keepdims=True))
        l = jnp.sum(p, axis=1, keepdims=True)
        o = jnp.dot(p.astype(BF16), v, preferred_element_type=F32) / l
        o_ref[:, h * dh:(h + 1) * dh] = o.astype(o_ref.dtype)


def _memory_attention(proj, mem_kv, *, batch, seq, mem_len, d, tq):
    dh = d // MEM_HEADS
    tq = min(tq, seq)
    nq = seq // tq
    q_blk = 6
    kern = functools.partial(_mem_kernel, d=d, dh=dh)
    return pl.pallas_call(
        kern,
        grid=(batch, nq),
        in_specs=[pl.BlockSpec((tq, d), lambda b, i: (b * nq + i, q_blk)),
                  pl.BlockSpec((mem_len, 2 * d), lambda b, i: (b, 0))],
        out_specs=pl.BlockSpec((tq, d), lambda b, i: (b * nq + i, 0)),
        out_shape=jax.ShapeDtypeStruct((batch * seq, d), BF16),
        compiler_params=_params(2),
        name="memory_attention",
    )(proj, mem_kv)


def _merge_kernel(x_ref, yda_ref, yret_ref, ymem_ref, g0_ref, g1_ref, g2_ref, bg_ref,
                  wb_ref, wo_ref, lng_ref, lnb_ref, wrh_ref, wrl_ref, br_ref, triu_ref,
                  x1_ref, key_ref, cnt_ref, carry_ref, *, alpha, key_shift):
    @pl.when(pl.program_id(0) == 0)
    def _():
        carry_ref[...] = jnp.zeros_like(carry_ref)

    ys = (yda_ref, yret_ref, ymem_ref)
    gs = (g0_ref, g1_ref, g2_ref)
    merged = None
    for n in range(N_BRANCH):
        gate = gs[n][...].astype(F32) + bg_ref[n:n + 1, :]
        gate = 1.0 / (1.0 + jnp.exp(-gate))
        term = gate * jnp.dot(ys[n][...], wb_ref[n], preferred_element_type=F32)
        merged = term if merged is None else merged + term
    out = jnp.dot(merged.astype(BF16), wo_ref[...], preferred_element_type=F32)
    x1 = _layer_norm(alpha * x_ref[...] + out, lng_ref[...], lnb_ref[...])
    tm, n_blocks = x1.shape[0], x1.shape[1] // LANES
    pitch = _slab_pitch(n_blocks)
    _to_slabs(x1_ref, x1, tm, pitch)
    x_hi = x1.astype(BF16)
    x_lo = (x1 - x_hi.astype(F32)).astype(BF16)
    logits = (jnp.dot(x_hi, wrh_ref[...], preferred_element_type=F32)
              + jnp.dot(x_lo, wrh_ref[...], preferred_element_type=F32)
              + jnp.dot(x_hi, wrl_ref[...], preferred_element_type=F32) + br_ref[...])
    record, key, total = _route_tokens(logits.T, triu_ref[...], carry_ref[...], key_shift)
    carry_ref[...] = total
    cnt_ref[...] = total.astype(jnp.int32)
    key_ref[...] = key
    x1_ref[pl.ds(n_blocks, tm, stride=pitch), :] = record
    _zero_slab_padding(x1_ref, tm, pitch, n_blocks + 1)


def _merge(x2, y_da, y_ret, y_mem, proj, b_gate, wb, wo, ln_g, ln_b, wr, br, *, alpha, tm,
           key_shift):
    t, d = x2.shape
    tm = min(tm, t)
    triu = jnp.triu(jnp.ones((tm, tm), BF16))
    gate_blk = 7
    row = lambda i: (i, 0)
    const2 = lambda i: (0, 0)
    once = pl.Buffered(1)
    x_pitch = _slab_pitch(d // LANES)
    wr_hi = wr.astype(BF16)
    wr_lo = (wr - wr_hi.astype(F32)).astype(BF16)
    kern = functools.partial(_merge_kernel, alpha=alpha, key_shift=key_shift)
    return pl.pallas_call(
        kern,
        grid=(t // tm,),
        in_specs=[pl.BlockSpec((tm, d), row), pl.BlockSpec((tm, d), row),
                  pl.BlockSpec((tm, d), row), pl.BlockSpec((tm, d), row),
                  pl.BlockSpec((tm, d), lambda i: (i, gate_blk)),
                  pl.BlockSpec((tm, d), lambda i: (i, gate_blk + 1)),
                  pl.BlockSpec((tm, d), lambda i: (i, gate_blk + 2)),
                  pl.BlockSpec((N_BRANCH, d), const2),
                  pl.BlockSpec((N_BRANCH, d, d), lambda i: (0, 0, 0), pipeline_mode=once),
                  pl.BlockSpec((d, d), const2, pipeline_mode=once),
                  pl.BlockSpec((1, d), const2), pl.BlockSpec((1, d), const2),
                  pl.BlockSpec((d, LANES), const2), pl.BlockSpec((d, LANES), const2),
                  pl.BlockSpec((1, LANES), const2),
                  pl.BlockSpec((tm, tm), const2, pipeline_mode=once)],
        out_specs=[pl.BlockSpec((tm * x_pitch, LANES), row),
                   pl.BlockSpec((1, tm), lambda i: (0, i)),
                   pl.BlockSpec((CLASS_ROWS, 1), const2)],
        out_shape=[jax.ShapeDtypeStruct((t * x_pitch, LANES), F32),
                   jax.ShapeDtypeStruct((1, t), jnp.int32),
                   jax.ShapeDtypeStruct((CLASS_ROWS, 1), jnp.int32)],
        scratch_shapes=[pltpu.VMEM((CLASS_ROWS, 1), F32)],
        compiler_params=_params(1),
        name="merge_ln1_router",
    )(x2, y_da, y_ret, y_mem, proj, proj, proj, b_gate, wb, wo, ln_g, ln_b, wr_hi, wr_lo, br,
      triu)


PAIRS = ((0, 1), (0, 2), (0, 3), (1, 2), (1, 3), (2, 3))
N_CLASSES = N_GROUPS * len(PAIRS)
ROUTE_CLS, ROUTE_RANK, ROUTE_WA, ROUTE_WB = 0, 1, 2, 3
EXPERT_ROW0 = 8
CLASS_ROWS = 32
EXPERT_GATHER_AHEAD = 2
COMBINE_GATHER_AHEAD = 2
COMBINE_TILE = 512


def _route_tokens(lgt, triu, carry, key_shift):
    tm = lgt.shape[1]
    row = lambda r: lgt[r:r + 1, :]

    def first_max(vals):
        best = vals[0]
        for v in vals[1:]:
            best = jnp.maximum(best, v)
        idx = jnp.full_like(best, float(len(vals) - 1))
        for k in range(len(vals) - 2, -1, -1):
            idx = jnp.where(vals[k] == best, float(k), idx)
        return best, idx

    groups = [row(g) for g in range(N_GROUPS)]
    gmax, g_idx = first_max(groups)
    denom = jnp.exp(groups[0] - gmax)
    for g in groups[1:]:
        denom = denom + jnp.exp(g - gmax)
    p_top = 1.0 / denom
    sel = []
    for k in range(EXPERTS_PER_GROUP):
        v = row(EXPERT_ROW0 + (N_GROUPS - 1) * EXPERTS_PER_GROUP + k)
        for g in range(N_GROUPS - 2, -1, -1):
            v = jnp.where(g_idx == float(g), row(EXPERT_ROW0 + g * EXPERTS_PER_GROUP + k), v)
        sel.append(v)
    e1, i1 = first_max(sel)
    rest = [jnp.where(i1 == float(k), -jnp.inf, sel[k]) for k in range(EXPERTS_PER_GROUP)]
    e2, i2 = first_max(rest)
    r = jnp.exp(e2 - e1)
    w1 = p_top / (1.0 + r)
    w2 = p_top * r / (1.0 + r)

    a_loc = jnp.minimum(i1, i2)
    b_loc = jnp.maximum(i1, i2)
    pair = jnp.where(a_loc == 0.0, b_loc - 1.0, jnp.where(a_loc == 1.0, b_loc + 1.0, 5.0))
    cls = g_idx * float(len(PAIRS)) + pair
    top_is_a = i1 < i2
    w_a = jnp.where(top_is_a, w1, w2)
    w_b = jnp.where(top_is_a, w2, w1)

    cls_row = lax.broadcasted_iota(jnp.int32, (CLASS_ROWS, tm), 0).astype(F32)
    onehot = jnp.where(cls_row == cls, 1.0, 0.0)
    prefix = carry + jnp.dot(onehot.astype(BF16), triu, preferred_element_type=F32)
    rank = jnp.sum(onehot * prefix, axis=0, keepdims=True) - 1.0
    total = carry + jnp.sum(onehot, axis=1, keepdims=True)
    key = (cls * float(1 << key_shift) + rank).astype(jnp.int32)
    record = jnp.concatenate([cls, rank, w_a, w_b, jnp.zeros((LANES - 4, tm), F32)], axis=0)
    return record.T, key, total


def _class_starts(counts, tm):
    padded = (counts + (tm - 1)) // tm * tm
    return jnp.cumsum(padded) - padded


def _positions_kernel(starts_ref, key_ref, pos_ref, *, key_shift):
    key = key_ref[...]
    cls = lax.shift_right_logical(key, key_shift)
    pos = key & ((1 << key_shift) - 1)
    for c in range(N_CLASSES):
        pos = jnp.where(cls == c, pos + starts_ref[c], pos)
    pos_ref[...] = pos


def _positions(key, starts, *, key_shift):
    return pl.pallas_call(
        functools.partial(_positions_kernel, key_shift=key_shift),
        in_specs=[pl.BlockSpec(memory_space=pltpu.SMEM), pl.BlockSpec(memory_space=pltpu.VMEM)],
        out_specs=pl.BlockSpec(memory_space=pltpu.VMEM),
        out_shape=jax.ShapeDtypeStruct(key.shape, jnp.int32),
        name="moe_positions",
    )(starts, key)


def _plan_kernel(pos_ref, cnt_ref, src_ref, ea_ref, eb_ref, used_ref, *, n_tokens, n_tiles, tm):
    tm_shift = tm.bit_length() - 1
    row0 = jnp.int32(0)
    tile0 = jnp.int32(0)
    last_a = jnp.int32(0)
    last_b = jnp.int32(1)
    for c in range(N_CLASSES):
        e_a = c // len(PAIRS) * EXPERTS_PER_GROUP + PAIRS[c % len(PAIRS)][0]
        e_b = c // len(PAIRS) * EXPERTS_PER_GROUP + PAIRS[c % len(PAIRS)][1]
        cnt = cnt_ref[c]
        n_cls_tiles = lax.shift_right_logical(cnt + (tm - 1), tm_shift)

        def mark(i, _, e_a=e_a, e_b=e_b):
            ea_ref[i] = e_a
            eb_ref[i] = e_b
            used_ref[i] = 1
            return 0

        lax.fori_loop(tile0, tile0 + n_cls_tiles, mark, 0)
        has = n_cls_tiles > 0
        last_a = jnp.where(has, e_a, last_a)
        last_b = jnp.where(has, e_b, last_b)
        row_end = row0 + n_cls_tiles * tm

        def pad(r, _):
            src_ref[r] = 0
            return 0

        lax.fori_loop(row0 + cnt, row_end, pad, 0)
        row0 = row_end
        tile0 = tile0 + n_cls_tiles

    def unused(i, _):
        ea_ref[i] = last_a
        eb_ref[i] = last_b
        used_ref[i] = 0
        return 0

    lax.fori_loop(tile0, n_tiles, unused, 0)
    lax.fori_loop(row0, n_tiles * tm, pad, 0)

    def put(t, _):
        src_ref[pos_ref[t]] = t
        return 0

    lax.fori_loop(0, n_tokens, put, 0, unroll=8)


def _plan(pos, counts, *, tm):
    n_tokens = pos.shape[0]
    assert tm & (tm - 1) == 0
    n_tiles = n_tokens // tm + N_CLASSES - 1 + EXPERT_GATHER_AHEAD
    smem = pl.BlockSpec(memory_space=pltpu.SMEM)
    tiles = jax.ShapeDtypeStruct((n_tiles,), jnp.int32)
    src, e_a, e_b, used = pl.pallas_call(
        functools.partial(_plan_kernel, n_tokens=n_tokens, n_tiles=n_tiles, tm=tm),
        in_specs=[smem, smem], out_specs=[smem] * 4,
        out_shape=[jax.ShapeDtypeStruct((n_tiles * tm,), jnp.int32), tiles, tiles, tiles],
        name="moe_plan",
    )(pos, counts)
    return src, e_a, e_b, used, n_tiles


def _slab_pitch(n_blocks):
    return n_blocks + 4 if n_blocks % 8 == 0 else n_blocks


def _to_slabs(ref, value, n_rows, pitch, offset=0):
    for j in range(value.shape[1] // LANES):
        ref[pl.ds(offset + j, n_rows, stride=pitch), :] = value[:, j * LANES:(j + 1) * LANES]


def _zero_slab_padding(ref, n_rows, pitch, n_blocks):
    for j in range(n_blocks, pitch):
        ref[pl.ds(j, n_rows, stride=pitch), :] = jnp.zeros((n_rows, LANES), ref.dtype)


def _from_slabs(ref, n_rows, n_blocks, pitch, offset=0):
    return jnp.concatenate([ref[pl.ds(offset + j, n_rows, stride=pitch), :]
                            for j in range(n_blocks)], axis=1)


def _gather_copy(tab_ref, buf_ref, sem, row, r, rec, pitch):
    return pltpu.make_async_copy(tab_ref.at[pl.ds(pl.multiple_of(row * pitch, 4), rec), :],
                                 buf_ref.at[pl.ds(pl.multiple_of(r * pitch, 4), rec), :], sem)


def _gather_start(idx_ref, tab_ref, buf_ref, sem, tile, tm, rec, pitch):
    def issue(r, _):
        _gather_copy(tab_ref, buf_ref, sem, idx_ref[tile * tm + r], r, rec, pitch).start()
        return 0

    lax.fori_loop(0, tm, issue, 0, unroll=8)


def _gather_wait(tab_ref, buf_ref, sem, tm, rec):
    pltpu.make_async_copy(tab_ref.at[pl.ds(0, tm * rec), :], buf_ref.at[pl.ds(0, tm * rec), :],
                          sem).wait()


def _experts_kernel(src_ref, ea_ref, eb_ref, used_ref, x_ref, w1a_ref, w3a_ref, w2a_ref,
                    w1b_ref, w3b_ref, w2b_ref, o_ref, buf_ref, sem_ref, *, tm, n_tiles, n_blocks):
    i = pl.program_id(0)
    ahead = EXPERT_GATHER_AHEAD
    slot = lax.rem(i, ahead + 1)
    fill_slot = lax.rem(i + ahead, ahead + 1)
    used = lambda tile: used_ref[tile] != 0
    in_flight = (i < ahead) | used(jnp.maximum(i - ahead, 0))
    pitch = _slab_pitch(n_blocks)
    rec = n_blocks + 1

    @pl.when(i == 0)
    def _():
        for j in range(ahead):
            _gather_start(src_ref, x_ref, buf_ref.at[j], sem_ref.at[j], j, tm, rec, pitch)

    @pl.when(used(i))
    def _():
        _gather_wait(x_ref, buf_ref.at[slot], sem_ref.at[slot], tm, rec)
        x = _from_slabs(buf_ref.at[slot], tm, n_blocks, pitch).astype(BF16)
        record = buf_ref[slot, pl.ds(n_blocks, tm, stride=pitch), :]
        ups = [(jnp.dot(x, w1_ref[...], preferred_element_type=F32),
                jnp.dot(x, w3_ref[...], preferred_element_type=F32))
               for w1_ref, w3_ref in ((w1a_ref, w3a_ref), (w1b_ref, w3b_ref))]
        for r in range(tm):
            _gather_copy(x_ref, buf_ref.at[fill_slot], sem_ref.at[fill_slot],
                         src_ref[(i + ahead) * tm + r], r, rec, pitch).start()
        y = None
        for (a, b), w2_ref, lane in zip(ups, (w2a_ref, w2b_ref), (ROUTE_WA, ROUTE_WB)):
            hid = a * (1.0 / (1.0 + jnp.exp(-a))) * b
            term = record[:, lane:lane + 1] * jnp.dot(hid.astype(BF16), w2_ref[...],
                                                      preferred_element_type=F32)
            y = term if y is None else y + term
        _to_slabs(o_ref, y, tm, pitch)
        _zero_slab_padding(o_ref, tm, pitch, n_blocks)

    @pl.when(jnp.logical_not(used(i)))
    def _():
        @pl.when(in_flight)
        def _():
            _gather_wait(x_ref, buf_ref.at[slot], sem_ref.at[slot], tm, rec)

        o_ref[...] = jnp.zeros_like(o_ref)


def _experts(x1s, src, e_a, e_b, used, w1, w3, w2, *, tm, n_tiles):
    d, f = w1.shape[1], w1.shape[2]
    n_blocks = d // LANES
    pitch = _slab_pitch(n_blocks)
    up_a = pl.BlockSpec((None, d, f), lambda i, src, ea, eb, used: (ea[i], 0, 0))
    up_b = pl.BlockSpec((None, d, f), lambda i, src, ea, eb, used: (eb[i], 0, 0))
    down_a = pl.BlockSpec((None, f, d), lambda i, src, ea, eb, used: (ea[i], 0, 0))
    down_b = pl.BlockSpec((None, f, d), lambda i, src, ea, eb, used: (eb[i], 0, 0))
    return pl.pallas_call(
        functools.partial(_experts_kernel, tm=tm, n_tiles=n_tiles, n_blocks=n_blocks),
        grid_spec=pltpu.PrefetchScalarGridSpec(
            num_scalar_prefetch=4, grid=(n_tiles,),
            in_specs=[pl.BlockSpec(memory_space=pl.ANY),
                      up_a, up_a, down_a, up_b, up_b, down_b],
            out_specs=pl.BlockSpec((tm * pitch, LANES), lambda i, src, ea, eb, used: (i, 0)),
            scratch_shapes=[pltpu.VMEM((EXPERT_GATHER_AHEAD + 1, tm * pitch, LANES), F32),
                            pltpu.SemaphoreType.DMA((EXPERT_GATHER_AHEAD + 1,))]),
        out_shape=jax.ShapeDtypeStruct((n_tiles * tm * pitch, LANES), F32),
        compiler_params=_params(1),
        name="experts",
    )(src, e_a, e_b, used, x1s, w1, w3, w2, w1, w3, w2)


def _combine_kernel(pos_ref, x_ref, lng_ref, lnb_ref, ys_ref, o_ref, buf_ref, sem_ref,
                    *, tm, n_tiles, n_blocks, alpha, ahead):
    i = pl.program_id(0)
    pitch = _slab_pitch(n_blocks)
    rec = n_blocks
    n_buf = ahead + 1
    slot = lax.rem(i, n_buf)

    @pl.when(i == 0)
    def _():
        for j in range(max(ahead, 1)):
            _gather_start(pos_ref, ys_ref, buf_ref.at[j], sem_ref.at[j], j, tm, rec, pitch)

    _gather_wait(ys_ref, buf_ref.at[slot], sem_ref.at[slot], tm, rec)
    moe = _from_slabs(buf_ref.at[slot], tm, n_blocks, pitch)
    fill = lax.rem(i + ahead, n_buf)
    nxt = jnp.minimum(i + ahead, n_tiles - 1)
    for r in range(tm if ahead else 0):
        _gather_copy(ys_ref, buf_ref.at[fill], sem_ref.at[fill], pos_ref[nxt * tm + r], r, rec,
                     pitch).start()
    x1 = _from_slabs(x_ref, tm, n_blocks, pitch)
    o_ref[...] = _layer_norm(alpha * x1 + moe, lng_ref[...], lnb_ref[...])

    @pl.when(i == n_tiles - 1)
    def _():
        for j in range(1, ahead + 1):
            extra = lax.rem(i + j, n_buf)
            _gather_wait(ys_ref, buf_ref.at[extra], sem_ref.at[extra], tm, rec)


def _combine(x1s, ys, pos, ln_g, ln_b, *, alpha, tm):
    t = pos.shape[0]
    d = ln_g.shape[1]
    n_blocks = d // LANES
    pitch = _slab_pitch(n_blocks)
    n_tiles = t // tm
    ahead = min(COMBINE_GATHER_AHEAD, n_tiles - 1)
    return pl.pallas_call(
        functools.partial(_combine_kernel, tm=tm, n_tiles=n_tiles, n_blocks=n_blocks, alpha=alpha,
                          ahead=ahead),
        grid_spec=pltpu.PrefetchScalarGridSpec(
            num_scalar_prefetch=1, grid=(n_tiles,),
            in_specs=[pl.BlockSpec((tm * pitch, LANES), lambda i, pos: (i, 0)),
                      pl.BlockSpec((1, d), lambda i, pos: (0, 0)),
                      pl.BlockSpec((1, d), lambda i, pos: (0, 0)),
                      pl.BlockSpec(memory_space=pl.ANY)],
            out_specs=pl.BlockSpec((tm, d), lambda i, pos: (i, 0)),
            scratch_shapes=[pltpu.VMEM((ahead + 1, tm * pitch, LANES), F32),
                            pltpu.SemaphoreType.DMA((ahead + 1,))]),
        out_shape=jax.ShapeDtypeStruct((t, d), F32),
        compiler_params=_params(1),
        name="moe_combine_ln2",
    )(pos, x1s, ln_g, ln_b, ys)


def _moe(x1s, key, counts, w1, w3, w2, ln_g, ln_b, *, alpha, tm, key_shift):
    t = key.shape[1]
    tm = min(tm, t)
    counts = counts.reshape(CLASS_ROWS)[:N_CLASSES]
    pos = _positions(key.reshape(8, t // 8), _class_starts(counts, tm), key_shift=key_shift)
    pos = pos.reshape(t)
    src, e_a, e_b, used, n_tiles = _plan(pos, counts, tm=tm)
    ys = _experts(x1s, src, e_a, e_b, used, w1, w3, w2, tm=tm, n_tiles=n_tiles)
    return _combine(x1s, ys, pos, ln_g, ln_b, alpha=alpha, tm=min(COMBINE_TILE, t))


def kernel(x, mem, w_in, b_gate, w_mem_kv, da_lambda, da_norm_g, ret_gn_g, ret_gn_b, w_branch, w_o,
           ln1_g, ln1_b, w_rg, b_rg, w_re, b_re, w1, w3, w2, ln2_g, ln2_b):
    batch, seq, d = x.shape
    mem_len = mem.shape[1]
    depth = w_in.shape[0]
    alpha = (2.0 * depth) ** 0.25
    t = batch * seq
    x2 = x.reshape(t, d)
    for l in range(depth):
        lambda_init = 0.8 - 0.6 * math.exp(-0.3 * l)
        proj = _matmul(x2, w_in[l], tm=2048, tn=1024, name="in_proj")
        mem_kv = _matmul(mem.reshape(batch * mem_len, d), w_mem_kv[l],
                         tm=1024, tn=1024, name="mem_kv_proj")
        y_da = _diff_attention(proj, da_lambda[l], da_norm_g[l], batch=batch, seq=seq, d=d,
                               lambda_init=lambda_init, tile=256)
        y_ret = _retention(proj, ret_gn_g[l], ret_gn_b[l], batch=batch, seq=seq, d=d)
        y_mem = _memory_attention(proj, mem_kv, batch=batch, seq=seq, mem_len=mem_len, d=d, tq=512)
        gap = EXPERT_ROW0 - N_GROUPS
        tail = LANES - EXPERT_ROW0 - N_EXPERTS
        wr = jnp.concatenate([w_rg[l], jnp.zeros((d, gap), F32), w_re[l],
                              jnp.zeros((d, tail), F32)], axis=1)
        br = jnp.concatenate([b_rg[l], jnp.zeros((gap,), F32), b_re[l],
                              jnp.zeros((tail,), F32)]).reshape(1, LANES)
        key_shift = max(t - 1, 1).bit_length()
        assert N_CLASSES << key_shift < 1 << 24
        x1s, key, counts = _merge(x2, y_da, y_ret, y_mem, proj, b_gate[l].reshape(N_BRANCH, d),
                                  w_branch[l].astype(BF16), w_o[l].astype(BF16),
                                  ln1_g[l].reshape(1, d), ln1_b[l].reshape(1, d), wr, br,
                                  alpha=alpha, tm=512, key_shift=key_shift)
        x2 = _moe(x1s, key, counts, w1[l].astype(BF16), w3[l].astype(BF16), w2[l].astype(BF16),
                  ln2_g[l].reshape(1, d), ln2_b[l].reshape(1, d), alpha=alpha, tm=256,
                  key_shift=key_shift)
    return x2.reshape(batch, seq, d)
```

```python
import functools
import math

import jax
import jax.numpy as jnp
import numpy as np
from jax import lax
from jax.experimental import pallas as pl
from jax.experimental.pallas import tpu as pltpu

BF16 = jnp.bfloat16
F32 = jnp.float32

DA_HEADS = 8
RET_HEADS = 4
RET_CHUNK = 128
MEM_HEADS = 4
N_BRANCH = 3
N_GROUPS = 4
EXPERTS_PER_GROUP = 4
N_EXPERTS = N_GROUPS * EXPERTS_PER_GROUP
EPS = 1e-5
LANES = 128
VMEM_LIMIT = 56 * 1024 * 1024

_NT = (((1,), (1,)), ((), ()))


def _params(n_axes):
    return pltpu.CompilerParams(dimension_semantics=("arbitrary",) * n_axes,
                                vmem_limit_bytes=VMEM_LIMIT)


def _layer_norm(z, g, b):
    mu = jnp.mean(z, axis=-1, keepdims=True)
    zc = z - mu
    var = jnp.mean(zc * zc, axis=-1, keepdims=True)
    return zc * lax.rsqrt(var + EPS) * g + b


def _matmul_kernel(x_ref, w_ref, o_ref):
    o_ref[...] = jnp.dot(x_ref[...].astype(BF16), w_ref[...].astype(BF16),
                         preferred_element_type=F32).astype(o_ref.dtype)


def _matmul(x, w, *, tm, tn, name):
    m, k = x.shape
    n = w.shape[1]
    tm, tn = min(tm, m), min(tn, n)
    return pl.pallas_call(
        _matmul_kernel,
        grid=(m // tm, n // tn),
        in_specs=[pl.BlockSpec((tm, k), lambda i, j: (i, 0)),
                  pl.BlockSpec((k, tn), lambda i, j: (0, j))],
        out_specs=pl.BlockSpec((tm, tn), lambda i, j: (i, j)),
        out_shape=jax.ShapeDtypeStruct((m, n), BF16),
        compiler_params=_params(2),
        name=name,
    )(x, w)


DA_ONES_ROWS = 16
DA_MASKED = -1e30
DA_CHUNK_ROWS = 32
LOG2_E = 1.4426950408889634


def _da_schedule(n_blk):
    full = [(qi, p) for qi in range(n_blk) for p in range(qi // 2)]
    order = full + [(qi, qi // 2) for qi in range(n_blk)]
    order.append(order[-1])
    table = np.asarray(order, np.int32)
    return len(full), table[:, 0], table[:, 1]


def _da_kernel(sq_ref, sp_ref, slope_ref, lam_ref, g_ref, q_ref, k_ref, v_ref, o_ref,
               ka0_ref, ka1_ref, qa0_ref, qa1_ref, vt_ref, s_ref, acc0_ref, acc1_ref, m0_ref, m1_ref,
               *, tile, dk, n_blk, n_full, lambda_init):
    h = pl.program_id(1)
    a_log2 = slope_ref[h] * LOG2_E
    w = 2 * dk
    ka_refs = (ka0_ref, ka1_ref)
    qa_refs = (qa0_ref, qa1_ref)
    acc_refs = (acc0_ref, acc1_ref)
    m_refs = (m0_ref, m1_ref)
    lane = lax.broadcasted_iota(jnp.int32, (tile, w), 1)
    pos = lax.broadcasted_iota(jnp.int32, (tile, w), 0).astype(F32)
    pieces = []
    rest = jnp.full((tile, w), a_log2, F32)
    for _ in range(3):
        piece = rest.astype(BF16).astype(F32)
        pieces.append(piece)
        rest = rest - piece
    in_comp, k_bias, q_bias = [], [], []
    for c in range(2):
        free = dk * (1 - c)
        in_comp.append((lane >= c * dk) & (lane < (c + 1) * dk))
        kb = jnp.where((lane >= free) & (lane < free + 3), pos, 0.0)
        qb = jnp.where((lane >= free + 3) & (lane < free + 6), -pos, 0.0)
        for i in range(3):
            kb = jnp.where(lane == free + 3 + i, pieces[i], kb)
            qb = jnp.where(lane == free + i, pieces[i], qb)
        k_bias.append(kb)
        q_bias.append(qb)

    def prepare(pp, _):
        for jj in range(2):
            blk = 2 * pp + jj
            rows = pl.ds(pl.multiple_of(blk * tile, tile), tile)
            kf = k_ref[rows, :].astype(F32)
            qf = q_ref[rows, :].astype(F32) * (dk ** -0.5 * LOG2_E)
            for c in range(2):
                ka_refs[c][rows, :] = jnp.where(in_comp[c], kf, k_bias[c]).astype(BF16)
                qa_refs[c][rows, :] = jnp.where(in_comp[c], qf, q_bias[c]).astype(BF16)
                acc_refs[c][blk] = jnp.zeros((w + DA_ONES_ROWS, tile), F32)
                m_refs[c][blk] = jnp.full((1, tile), -jnp.inf, F32)
            vt_ref[pp, 0:w, jj * tile:(jj + 1) * tile] = v_ref[rows, :].astype(F32).T.astype(BF16)
        vt_ref[pp, w:w + DA_ONES_ROWS, :] = jnp.ones((DA_ONES_ROWS, 2 * tile), BF16)
        return 0

    lax.fori_loop(0, n_blk // 2, prepare, 0)

    lv = lam_ref[...]
    lam = (jnp.exp(jnp.sum(lv[0:1] * lv[1:2], axis=1, keepdims=True))
           - jnp.exp(jnp.sum(lv[2:3] * lv[3:4], axis=1, keepdims=True)) + lambda_init)
    out_gain = jnp.broadcast_to(g_ref[...] * (1.0 - lambda_init), (w, tile))
    chunk = DA_CHUNK_ROWS
    n_chunks = tile // chunk
    rel0 = (lax.broadcasted_iota(jnp.int32, (chunk, tile), 0)
            - lax.broadcasted_iota(jnp.int32, (chunk, tile), 1))

    def issue_scores(t, slot, n_kv):
        qi, p = sq_ref[t], sp_ref[t]
        q_rows = pl.ds(pl.multiple_of(qi * tile, tile), tile)
        for jj in range(n_kv):
            k_rows = pl.ds(pl.multiple_of((2 * p + jj) * tile, tile), tile)
            for c in range(2):
                s_ref[slot, 2 * jj + c] = lax.dot_general(
                    ka_refs[c][k_rows, :], qa_refs[c][q_rows, :], _NT, preferred_element_type=F32)

    def consume(t, slot, diag):
        n_kv = 1 if diag == 0 else 2
        qi, p = sq_ref[t], sp_ref[t]
        offs = [a_log2 * ((2 * p + jj - qi) * tile).astype(F32) for jj in range(n_kv)]
        accs = []
        for c in range(2):
            tiles = [s_ref.at[slot, 2 * jj + c] for jj in range(n_kv)]
            m_old = m_refs[c][qi]
            m_new = m_old
            for jj in range(n_kv):
                col_max = None
                for r in range(n_chunks):
                    rows = slice(r * chunk, (r + 1) * chunk)
                    x = tiles[jj][rows, :]
                    if jj == diag:
                        x = jnp.where(rel0 <= -r * chunk, x, DA_MASKED)
                        tiles[jj][rows, :] = x
                    col_max = x if col_max is None else jnp.maximum(col_max, x)
                m_new = jnp.maximum(m_new, jnp.max(col_max, axis=0, keepdims=True) + offs[jj])
            pts = []
            for jj in range(n_kv):
                shift = m_new - offs[jj]
                for r in range(n_chunks):
                    x = tiles[jj][r * chunk:(r + 1) * chunk, :]
                    pts.append(jnp.exp2(x - shift).astype(BF16))
            vt = vt_ref[p] if n_kv == 2 else vt_ref[p, :, 0:tile]
            pv = jnp.dot(vt, jnp.concatenate(pts, axis=0), preferred_element_type=F32)
            acc = jnp.exp2(m_old - m_new) * acc_refs[c][qi] + pv
            if diag is None:
                acc_refs[c][qi] = acc
                m_refs[c][qi] = m_new
            else:
                accs.append(acc)
        if diag is not None:
            a0, a1 = accs
            ot = a0[0:w] / a0[w:w + 1] - lam * (a1[0:w] / a1[w:w + 1])
            ot = ot * lax.rsqrt(jnp.mean(ot * ot, axis=0, keepdims=True) + EPS) * out_gain
            q_rows = pl.ds(pl.multiple_of(qi * tile, tile), tile)
            o_ref[q_rows, :] = ot.T.astype(o_ref.dtype)

    def two_entries(i, base, last):
        t = base + 2 * i
        issue_scores(t + 1, 1, 2)
        consume(t, 0, 0 if last else None)
        issue_scores(t + 2, 0, 1 if last else 2)
        consume(t + 1, 1, 1 if last else None)
        return 0

    issue_scores(0, 0, 2 if n_full else 1)
    lax.fori_loop(0, n_full // 2, lambda i, _: two_entries(i, 0, False), 0)
    lax.fori_loop(0, n_blk // 2, lambda i, _: two_entries(i, n_full, True), 0)


def _diff_attention(proj, lam_vecs, norm_g, *, batch, seq, d, lambda_init, tile):
    dk = d // (2 * DA_HEADS)
    w = 2 * dk
    tile = min(tile, seq)
    assert tile <= 256 and seq % (2 * tile) == 0
    n_blk = seq // tile
    hpb = d // w
    slopes = 2.0 ** (-8.0 * (jnp.arange(DA_HEADS, dtype=F32) + 1.0) / DA_HEADS)
    n_full, sched_q, sched_p = _da_schedule(n_blk)
    kern = functools.partial(_da_kernel, tile=tile, dk=dk, n_blk=n_blk, n_full=n_full,
                             lambda_init=lambda_init)
    smem = pl.BlockSpec(memory_space=pltpu.SMEM)
    return pl.pallas_call(
        kern,
        grid=(batch, DA_HEADS),
        in_specs=[smem, smem, smem,
                  pl.BlockSpec((4, dk), lambda b, h: (0, 0)),
                  pl.BlockSpec((w, 1), lambda b, h: (0, 0)),
                  pl.BlockSpec((seq, w), lambda b, h: (b, h)),
                  pl.BlockSpec((seq, w), lambda b, h: (b, hpb + h)),
                  pl.BlockSpec((seq, w), lambda b, h: (b, 2 * hpb + h))],
        out_specs=pl.BlockSpec((seq, w), lambda b, h: (b, h)),
        out_shape=jax.ShapeDtypeStruct((batch * seq, d), BF16),
        scratch_shapes=[pltpu.VMEM((seq, w), BF16), pltpu.VMEM((seq, w), BF16),
                        pltpu.VMEM((seq, w), BF16), pltpu.VMEM((seq, w), BF16),
                        pltpu.VMEM((n_blk // 2, w + DA_ONES_ROWS, 2 * tile), BF16),
                        pltpu.VMEM((2, 4, tile, tile), F32),
                        pltpu.VMEM((n_blk, w + DA_ONES_ROWS, tile), F32),
                        pltpu.VMEM((n_blk, w + DA_ONES_ROWS, tile), F32),
                        pltpu.VMEM((n_blk, 1, tile), F32),
                        pltpu.VMEM((n_blk, 1, tile), F32)],
        compiler_params=_params(2),
        name="diff_attention",
    )(jnp.asarray(sched_q), jnp.asarray(sched_p), slopes, lam_vecs, norm_g.reshape(w, 1),
      proj, proj, proj)


def _ret_kernel(cd_ref, intra_ref, qd_ref, kd_ref, gng_ref, gnb_ref,
                q_ref, k_ref, v_ref, g_ref, o_ref, state_ref, *, n_chunks, dk, dv):
    c_len = RET_CHUNK
    heads = range(RET_HEADS)
    state_ref[...] = jnp.zeros_like(state_ref)

    def chunk(c, _):
        rows = pl.ds(pl.multiple_of(c * c_len, c_len), c_len)
        q = [q_ref[rows, h * dk:(h + 1) * dk] for h in heads]
        k = [k_ref[rows, h * dk:(h + 1) * dk] for h in heads]
        v = [v_ref[rows, h * dv:(h + 1) * dv] for h in heads]
        state = [state_ref[h] for h in heads]
        s = [lax.dot_general(q[h], k[h], _NT, preferred_element_type=F32) for h in heads]
        cross = [jnp.dot(q[h], state[h].astype(BF16), preferred_element_type=F32) for h in heads]
        kv = [jnp.dot((k[h].astype(F32) * kd_ref[h]).T.astype(BF16), v[h],
                      preferred_element_type=F32) for h in heads]
        o = [jnp.dot((s[h] * intra_ref[h]).astype(BF16), v[h], preferred_element_type=F32)
             + cross[h] * qd_ref[h] for h in heads]
        for h in heads:
            state_ref[h] = state[h] * cd_ref[h] + kv[h]
            cols = slice(h * dv, (h + 1) * dv)
            mu = jnp.mean(o[h], axis=1, keepdims=True)
            oc = o[h] - mu
            var = jnp.mean(oc * oc, axis=1, keepdims=True)
            y = oc * lax.rsqrt(var + EPS) * gng_ref[:, cols] + gnb_ref[:, cols]
            gate = g_ref[rows, cols].astype(F32)
            y = gate * (1.0 / (1.0 + jnp.exp(-gate))) * y
            o_ref[rows, cols] = y.astype(o_ref.dtype)
        return 0

    lax.fori_loop(0, n_chunks, chunk, 0, unroll=2)


def _retention(proj, gn_g, gn_b, *, batch, seq, d):
    dv = d // RET_HEADS
    dk = dv // 2
    c_len = RET_CHUNK
    n_chunks = seq // c_len
    log_g = jnp.log(1.0 - 2.0 ** (-5.0 - jnp.arange(RET_HEADS, dtype=F32)))
    idx = jnp.arange(c_len, dtype=F32)
    rel = idx[:, None] - idx[None, :]
    intra = jnp.where(rel[None] >= 0, jnp.exp(rel[None] * log_g[:, None, None]), 0.0) * dk ** -0.5
    q_decay = jnp.exp((idx[None, :] + 1.0) * log_g[:, None])[:, :, None]
    k_decay = (jnp.exp((c_len - 1.0 - idx[None, :]) * log_g[:, None]) * dk ** -0.5)[:, :, None]
    chunk_decay = jnp.exp(c_len * log_g)
    hk = RET_HEADS * dk
    q_blk = 3 * d // hk
    k_blk = q_blk + 1
    v_blk = (3 * d + 2 * hk) // d
    g_blk = v_blk + 1
    kern = functools.partial(_ret_kernel, n_chunks=n_chunks, dk=dk, dv=dv)
    const3 = lambda b: (0, 0, 0)
    return pl.pallas_call(
        kern,
        grid=(batch,),
        in_specs=[pl.BlockSpec(memory_space=pltpu.SMEM),
                  pl.BlockSpec((RET_HEADS, c_len, c_len), const3),
                  pl.BlockSpec((RET_HEADS, c_len, 1), const3),
                  pl.BlockSpec((RET_HEADS, c_len, 1), const3),
                  pl.BlockSpec((1, d), lambda b: (0, 0)),
                  pl.BlockSpec((1, d), lambda b: (0, 0)),
                  pl.BlockSpec((seq, hk), lambda b: (b, q_blk)),
                  pl.BlockSpec((seq, hk), lambda b: (b, k_blk)),
                  pl.BlockSpec((seq, d), lambda b: (b, v_blk)),
                  pl.BlockSpec((seq, d), lambda b: (b, g_blk))],
        out_specs=pl.BlockSpec((seq, d), lambda b: (b, 0)),
        out_shape=jax.ShapeDtypeStruct((batch * seq, d), BF16),
        scratch_shapes=[pltpu.VMEM((RET_HEADS, dk, dv), F32)],
        compiler_params=_params(1),
        name="retention",
    )(chunk_decay, intra, q_decay, k_decay, gn_g.reshape(1, d), gn_b.reshape(1, d),
      proj, proj, proj, proj)


def _mem_kernel(q_ref, kv_ref, o_ref, *, d, dh):
    for h in range(MEM_HEADS):
        q = q_ref[:, h * dh:(h + 1) * dh]
        k = kv_ref[:, h * dh:(h + 1) * dh]
        v = kv_ref[:, d + h * dh:d + (h + 1) * dh]
        s = lax.dot_general(q, k, _NT, preferred_element_type=F32) * dh ** -0.5
        p = jnp.exp(s - jnp.max(s, axis=1, keepdims=True))
        l = jnp.sum(p, axis=1, keepdims=True)
        o = jnp.dot(p.astype(BF16), v, preferred_element_type=F32) / l
        o_ref[:, h * dh:(h + 1) * dh] = o.astype(o_ref.dtype)


def _memory_attention(proj, mem_kv, *, batch, seq, mem_len, d, tq):
    dh = d // MEM_HEADS
    tq = min(tq, seq)
    nq = seq // tq
    q_blk = 6
    kern = functools.partial(_mem_kernel, d=d, dh=dh)
    return pl.pallas_call(
        kern,
        grid=(batch, nq),
        in_specs=[pl.BlockSpec((tq, d), lambda b, i: (b * nq + i, q_blk)),
                  pl.BlockSpec((mem_len, 2 * d), lambda b, i: (b, 0))],
        out_specs=pl.BlockSpec((tq, d), lambda b, i: (b * nq + i, 0)),
        out_shape=jax.ShapeDtypeStruct((batch * seq, d), BF16),
        compiler_params=_params(2),
        name="memory_attention",
    )(proj, mem_kv)


def _merge_kernel(x_ref, yda_ref, yret_ref, ymem_ref, g0_ref, g1_ref, g2_ref, bg_ref,
                  wb_ref, wo_ref, lng_ref, lnb_ref, wrh_ref, wrl_ref, br_ref, triu_ref,
                  x1_ref, key_ref, cnt_ref, carry_ref, *, alpha, key_shift):
    @pl.when(pl.program_id(0) == 0)
    def _():
        carry_ref[...] = jnp.zeros_like(carry_ref)

    ys = (yda_ref, yret_ref, ymem_ref)
    gs = (g0_ref, g1_ref, g2_ref)
    merged = None
    for n in range(N_BRANCH):
        gate = gs[n][...].astype(F32) + bg_ref[n:n + 1, :]
        gate = 1.0 / (1.0 + jnp.exp(-gate))
        term = gate * jnp.dot(ys[n][...], wb_ref[n], preferred_element_type=F32)
        merged = term if merged is None else merged + term
    out = jnp.dot(merged.astype(BF16), wo_ref[...], preferred_element_type=F32)
    x1 = _layer_norm(alpha * x_ref[...] + out, lng_ref[...], lnb_ref[...])
    tm, n_blocks = x1.shape[0], x1.shape[1] // LANES
    pitch = _slab_pitch(n_blocks)
    _to_slabs(x1_ref, x1, tm, pitch)
    x_hi = x1.astype(BF16)
    x_lo = (x1 - x_hi.astype(F32)).astype(BF16)
    logits = (jnp.dot(x_hi, wrh_ref[...], preferred_element_type=F32)
              + jnp.dot(x_lo, wrh_ref[...], preferred_element_type=F32)
              + jnp.dot(x_hi, wrl_ref[...], preferred_element_type=F32) + br_ref[...])
    record, key, total = _route_tokens(logits.T, triu_ref[...], carry_ref[...], key_shift)
    carry_ref[...] = total
    cnt_ref[...] = total.astype(jnp.int32)
    key_ref[...] = key
    x1_ref[pl.ds(n_blocks, tm, stride=pitch), :] = record
    _zero_slab_padding(x1_ref, tm, pitch, n_blocks + 1)


def _merge(x2, y_da, y_ret, y_mem, proj, b_gate, wb, wo, ln_g, ln_b, wr, br, *, alpha, tm,
           key_shift):
    t, d = x2.shape
    tm = min(tm, t)
    triu = jnp.triu(jnp.ones((tm, tm), BF16))
    gate_blk = 7
    row = lambda i: (i, 0)
    const2 = lambda i: (0, 0)
    once = pl.Buffered(1)
    x_pitch = _slab_pitch(d // LANES)
    wr_hi = wr.astype(BF16)
    wr_lo = (wr - wr_hi.astype(F32)).astype(BF16)
    kern = functools.partial(_merge_kernel, alpha=alpha, key_shift=key_shift)
    return pl.pallas_call(
        kern,
        grid=(t // tm,),
        in_specs=[pl.BlockSpec((tm, d), row), pl.BlockSpec((tm, d), row),
                  pl.BlockSpec((tm, d), row), pl.BlockSpec((tm, d), row),
                  pl.BlockSpec((tm, d), lambda i: (i, gate_blk)),
                  pl.BlockSpec((tm, d), lambda i: (i, gate_blk + 1)),
                  pl.BlockSpec((tm, d), lambda i: (i, gate_blk + 2)),
                  pl.BlockSpec((N_BRANCH, d), const2),
                  pl.BlockSpec((N_BRANCH, d, d), lambda i: (0, 0, 0), pipeline_mode=once),
                  pl.BlockSpec((d, d), const2, pipeline_mode=once),
                  pl.BlockSpec((1, d), const2), pl.BlockSpec((1, d), const2),
                  pl.BlockSpec((d, LANES), const2), pl.BlockSpec((d, LANES), const2),
                  pl.BlockSpec((1, LANES), const2),
                  pl.BlockSpec((tm, tm), const2, pipeline_mode=once)],
        out_specs=[pl.BlockSpec((tm * x_pitch, LANES), row),
                   pl.BlockSpec((1, tm), lambda i: (0, i)),
                   pl.BlockSpec((CLASS_ROWS, 1), const2)],
        out_shape=[jax.ShapeDtypeStruct((t * x_pitch, LANES), F32),
                   jax.ShapeDtypeStruct((1, t), jnp.int32),
                   jax.ShapeDtypeStruct((CLASS_ROWS, 1), jnp.int32)],
        scratch_shapes=[pltpu.VMEM((CLASS_ROWS, 1), F32)],
        compiler_params=_params(1),
        name="merge_ln1_router",
    )(x2, y_da, y_ret, y_mem, proj, proj, proj, b_gate, wb, wo, ln_g, ln_b, wr_hi, wr_lo, br,
      triu)


PAIRS = ((0, 1), (0, 2), (0, 3), (1, 2), (1, 3), (2, 3))
N_CLASSES = N_GROUPS * len(PAIRS)
ROUTE_CLS, ROUTE_RANK, ROUTE_WA, ROUTE_WB = 0, 1, 2, 3
EXPERT_ROW0 = 8
CLASS_ROWS = 32
EXPERT_GATHER_AHEAD = 2
COMBINE_GATHER_AHEAD = 2
COMBINE_TILE = 512


def _route_tokens(lgt, triu, carry, key_shift):
    tm = lgt.shape[1]
    row = lambda r: lgt[r:r + 1, :]

    def first_max(vals):
        best = vals[0]
        for v in vals[1:]:
            best = jnp.maximum(best, v)
        idx = jnp.full_like(best, float(len(vals) - 1))
        for k in range(len(vals) - 2, -1, -1):
            idx = jnp.where(vals[k] == best, float(k), idx)
        return best, idx

    groups = [row(g) for g in range(N_GROUPS)]
    gmax, g_idx = first_max(groups)
    denom = jnp.exp(groups[0] - gmax)
    for g in groups[1:]:
        denom = denom + jnp.exp(g - gmax)
    p_top = 1.0 / denom
    sel = []
    for k in range(EXPERTS_PER_GROUP):
        v = row(EXPERT_ROW0 + (N_GROUPS - 1) * EXPERTS_PER_GROUP + k)
        for g in range(N_GROUPS - 2, -1, -1):
            v = jnp.where(g_idx == float(g), row(EXPERT_ROW0 + g * EXPERTS_PER_GROUP + k), v)
        sel.append(v)
    e1, i1 = first_max(sel)
    rest = [jnp.where(i1 == float(k), -jnp.inf, sel[k]) for k in range(EXPERTS_PER_GROUP)]
    e2, i2 = first_max(rest)
    r = jnp.exp(e2 - e1)
    w1 = p_top / (1.0 + r)
    w2 = p_top * r / (1.0 + r)

    a_loc = jnp.minimum(i1, i2)
    b_loc = jnp.maximum(i1, i2)
    pair = jnp.where(a_loc == 0.0, b_loc - 1.0, jnp.where(a_loc == 1.0, b_loc + 1.0, 5.0))
    cls = g_idx * float(len(PAIRS)) + pair
    top_is_a = i1 < i2
    w_a = jnp.where(top_is_a, w1, w2)
    w_b = jnp.where(top_is_a, w2, w1)

    cls_row = lax.broadcasted_iota(jnp.int32, (CLASS_ROWS, tm), 0).astype(F32)
    onehot = jnp.where(cls_row == cls, 1.0, 0.0)
    prefix = carry + jnp.dot(onehot.astype(BF16), triu, preferred_element_type=F32)
    rank = jnp.sum(onehot * prefix, axis=0, keepdims=True) - 1.0
    total = carry + jnp.sum(onehot, axis=1, keepdims=True)
    key = (cls * float(1 << key_shift) + rank).astype(jnp.int32)
    record = jnp.concatenate([cls, rank, w_a, w_b, jnp.zeros((LANES - 4, tm), F32)], axis=0)
    return record.T, key, total


def _class_starts(counts, tm):
    padded = (counts + (tm - 1)) // tm * tm
    return jnp.cumsum(padded) - padded


def _positions_kernel(starts_ref, key_ref, pos_ref, *, key_shift):
    key = key_ref[...]
    cls = lax.shift_right_logical(key, key_shift)
    pos = key & ((1 << key_shift) - 1)
    for c in range(N_CLASSES):
        pos = jnp.where(cls == c, pos + starts_ref[c], pos)
    pos_ref[...] = pos


def _positions(key, starts, *, key_shift):
    return pl.pallas_call(
        functools.partial(_positions_kernel, key_shift=key_shift),
        in_specs=[pl.BlockSpec(memory_space=pltpu.SMEM), pl.BlockSpec(memory_space=pltpu.VMEM)],
        out_specs=pl.BlockSpec(memory_space=pltpu.VMEM),
        out_shape=jax.ShapeDtypeStruct(key.shape, jnp.int32),
        name="moe_positions",
    )(starts, key)


def _plan_kernel(pos_ref, cnt_ref, src_ref, ea_ref, eb_ref, used_ref, *, n_tokens, n_tiles, tm):
    tm_shift = tm.bit_length() - 1
    row0 = jnp.int32(0)
    tile0 = jnp.int32(0)
    last_a = jnp.int32(0)
    last_b = jnp.int32(1)
    for c in range(N_CLASSES):
        e_a = c // len(PAIRS) * EXPERTS_PER_GROUP + PAIRS[c % len(PAIRS)][0]
        e_b = c // len(PAIRS) * EXPERTS_PER_GROUP + PAIRS[c % len(PAIRS)][1]
        cnt = cnt_ref[c]
        n_cls_tiles = lax.shift_right_logical(cnt + (tm - 1), tm_shift)

        def mark(i, _, e_a=e_a, e_b=e_b):
            ea_ref[i] = e_a
            eb_ref[i] = e_b
            used_ref[i] = 1
            return 0

        lax.fori_loop(tile0, tile0 + n_cls_tiles, mark, 0)
        has = n_cls_tiles > 0
        last_a = jnp.where(has, e_a, last_a)
        last_b = jnp.where(has, e_b, last_b)
        row_end = row0 + n_cls_tiles * tm

        def pad(r, _):
            src_ref[r] = 0
            return 0

        lax.fori_loop(row0 + cnt, row_end, pad, 0)
        row0 = row_end
        tile0 = tile0 + n_cls_tiles

    def unused(i, _):
        ea_ref[i] = last_a
        eb_ref[i] = last_b
        used_ref[i] = 0
        return 0

    lax.fori_loop(tile0, n_tiles, unused, 0)
    lax.fori_loop(row0, n_tiles * tm, pad, 0)

    def put(t, _):
        src_ref[pos_ref[t]] = t
        return 0

    lax.fori_loop(0, n_tokens, put, 0, unroll=8)


def _plan(pos, counts, *, tm):
    n_tokens = pos.shape[0]
    assert tm & (tm - 1) == 0
    n_tiles = n_tokens // tm + N_CLASSES - 1 + EXPERT_GATHER_AHEAD
    smem = pl.BlockSpec(memory_space=pltpu.SMEM)
    tiles = jax.ShapeDtypeStruct((n_tiles,), jnp.int32)
    src, e_a, e_b, used = pl.pallas_call(
        functools.partial(_plan_kernel, n_tokens=n_tokens, n_tiles=n_tiles, tm=tm),
        in_specs=[smem, smem], out_specs=[smem] * 4,
        out_shape=[jax.ShapeDtypeStruct((n_tiles * tm,), jnp.int32), tiles, tiles, tiles],
        name="moe_plan",
    )(pos, counts)
    return src, e_a, e_b, used, n_tiles


def _slab_pitch(n_blocks):
    return n_blocks + 4 if n_blocks % 8 == 0 else n_blocks


def _to_slabs(ref, value, n_rows, pitch, offset=0):
    for j in range(value.shape[1] // LANES):
        ref[pl.ds(offset + j, n_rows, stride=pitch), :] = value[:, j * LANES:(j + 1) * LANES]


def _zero_slab_padding(ref, n_rows, pitch, n_blocks):
    for j in range(n_blocks, pitch):
        ref[pl.ds(j, n_rows, stride=pitch), :] = jnp.zeros((n_rows, LANES), ref.dtype)


def _from_slabs(ref, n_rows, n_blocks, pitch, offset=0):
    return jnp.concatenate([ref[pl.ds(offset + j, n_rows, stride=pitch), :]
                            for j in range(n_blocks)], axis=1)


def _gather_copy(tab_ref, buf_ref, sem, row, r, rec, pitch):
    return pltpu.make_async_copy(tab_ref.at[pl.ds(pl.multiple_of(row * pitch, 4), rec), :],
                                 buf_ref.at[pl.ds(pl.multiple_of(r * pitch, 4), rec), :], sem)


def _gather_start(idx_ref, tab_ref, buf_ref, sem, tile, tm, rec, pitch):
    def issue(r, _):
        _gather_copy(tab_ref, buf_ref, sem, idx_ref[tile * tm + r], r, rec, pitch).start()
        return 0

    lax.fori_loop(0, tm, issue, 0, unroll=8)


def _gather_wait(tab_ref, buf_ref, sem, tm, rec):
    pltpu.make_async_copy(tab_ref.at[pl.ds(0, tm * rec), :], buf_ref.at[pl.ds(0, tm * rec), :],
                          sem).wait()


def _experts_kernel(src_ref, ea_ref, eb_ref, used_ref, x_ref, w1a_ref, w3a_ref, w2a_ref,
                    w1b_ref, w3b_ref, w2b_ref, o_ref, buf_ref, sem_ref, *, tm, n_tiles, n_blocks):
    i = pl.program_id(0)
    ahead = EXPERT_GATHER_AHEAD
    slot = lax.rem(i, ahead + 1)
    fill_slot = lax.rem(i + ahead, ahead + 1)
    used = lambda tile: used_ref[tile] != 0
    in_flight = (i < ahead) | used(jnp.maximum(i - ahead, 0))
    pitch = _slab_pitch(n_blocks)
    rec = n_blocks + 1

    @pl.when(i == 0)
    def _():
        for j in range(ahead):
            _gather_start(src_ref, x_ref, buf_ref.at[j], sem_ref.at[j], j, tm, rec, pitch)

    @pl.when(used(i))
    def _():
        _gather_wait(x_ref, buf_ref.at[slot], sem_ref.at[slot], tm, rec)
        x = _from_slabs(buf_ref.at[slot], tm, n_blocks, pitch).astype(BF16)
        record = buf_ref[slot, pl.ds(n_blocks, tm, stride=pitch), :]
        ups = [(jnp.dot(x, w1_ref[...], preferred_element_type=F32),
                jnp.dot(x, w3_ref[...], preferred_element_type=F32))
               for w1_ref, w3_ref in ((w1a_ref, w3a_ref), (w1b_ref, w3b_ref))]
        for r in range(tm):
            _gather_copy(x_ref, buf_ref.at[fill_slot], sem_ref.at[fill_slot],
                         src_ref[(i + ahead) * tm + r], r, rec, pitch).start()
        y = None
        for (a, b), w2_ref, lane in zip(ups, (w2a_ref, w2b_ref), (ROUTE_WA, ROUTE_WB)):
            hid = a * (1.0 / (1.0 + jnp.exp(-a))) * b
            term = record[:, lane:lane + 1] * jnp.dot(hid.astype(BF16), w2_ref[...],
                                                      preferred_element_type=F32)
            y = term if y is None else y + term
        _to_slabs(o_ref, y, tm, pitch)
        _zero_slab_padding(o_ref, tm, pitch, n_blocks)

    @pl.when(jnp.logical_not(used(i)))
    def _():
        @pl.when(in_flight)
        def _():
            _gather_wait(x_ref, buf_ref.at[slot], sem_ref.at[slot], tm, rec)

        o_ref[...] = jnp.zeros_like(o_ref)


def _experts(x1s, src, e_a, e_b, used, w1, w3, w2, *, tm, n_tiles):
    d, f = w1.shape[1], w1.shape[2]
    n_blocks = d // LANES
    pitch = _slab_pitch(n_blocks)
    up_a = pl.BlockSpec((None, d, f), lambda i, src, ea, eb, used: (ea[i], 0, 0))
    up_b = pl.BlockSpec((None, d, f), lambda i, src, ea, eb, used: (eb[i], 0, 0))
    down_a = pl.BlockSpec((None, f, d), lambda i, src, ea, eb, used: (ea[i], 0, 0))
    down_b = pl.BlockSpec((None, f, d), lambda i, src, ea, eb, used: (eb[i], 0, 0))
    return pl.pallas_call(
        functools.partial(_experts_kernel, tm=tm, n_tiles=n_tiles, n_blocks=n_blocks),
        grid_spec=pltpu.PrefetchScalarGridSpec(
            num_scalar_prefetch=4, grid=(n_tiles,),
            in_specs=[pl.BlockSpec(memory_space=pl.ANY),
                      up_a, up_a, down_a, up_b, up_b, down_b],
            out_specs=pl.BlockSpec((tm * pitch, LANES), lambda i, src, ea, eb, used: (i, 0)),
            scratch_shapes=[pltpu.VMEM((EXPERT_GATHER_AHEAD + 1, tm * pitch, LANES), F32),
                            pltpu.SemaphoreType.DMA((EXPERT_GATHER_AHEAD + 1,))]),
        out_shape=jax.ShapeDtypeStruct((n_tiles * tm * pitch, LANES), F32),
        compiler_params=_params(1),
        name="experts",
    )(src, e_a, e_b, used, x1s, w1, w3, w2, w1, w3, w2)


def _combine_kernel(pos_ref, x_ref, lng_ref, lnb_ref, ys_ref, o_ref, buf_ref, sem_ref,
                    *, tm, n_tiles, n_blocks, alpha, ahead):
    i = pl.program_id(0)
    pitch = _slab_pitch(n_blocks)
    rec = n_blocks
    n_buf = ahead + 1
    slot = lax.rem(i, n_buf)

    @pl.when(i == 0)
    def _():
        for j in range(max(ahead, 1)):
            _gather_start(pos_ref, ys_ref, buf_ref.at[j], sem_ref.at[j], j, tm, rec, pitch)

    _gather_wait(ys_ref, buf_ref.at[slot], sem_ref.at[slot], tm, rec)
    moe = _from_slabs(buf_ref.at[slot], tm, n_blocks, pitch)
    fill = lax.rem(i + ahead, n_buf)
    nxt = jnp.minimum(i + ahead, n_tiles - 1)
    for r in range(tm if ahead else 0):
        _gather_copy(ys_ref, buf_ref.at[fill], sem_ref.at[fill], pos_ref[nxt * tm + r], r, rec,
                     pitch).start()
    x1 = _from_slabs(x_ref, tm, n_blocks, pitch)
    o_ref[...] = _layer_norm(alpha * x1 + moe, lng_ref[...], lnb_ref[...])

    @pl.when(i == n_tiles - 1)
    def _():
        for j in range(1, ahead + 1):
            extra = lax.rem(i + j, n_buf)
            _gather_wait(ys_ref, buf_ref.at[extra], sem_ref.at[extra], tm, rec)


def _combine(x1s, ys, pos, ln_g, ln_b, *, alpha, tm):
    t = pos.shape[0]
    d = ln_g.shape[1]
    n_blocks = d // LANES
    pitch = _slab_pitch(n_blocks)
    n_tiles = t // tm
    ahead = min(COMBINE_GATHER_AHEAD, n_tiles - 1)
    return pl.pallas_call(
        functools.partial(_combine_kernel, tm=tm, n_tiles=n_tiles, n_blocks=n_blocks, alpha=alpha,
                          ahead=ahead),
        grid_spec=pltpu.PrefetchScalarGridSpec(
            num_scalar_prefetch=1, grid=(n_tiles,),
            in_specs=[pl.BlockSpec((tm * pitch, LANES), lambda i, pos: (i, 0)),
                      pl.BlockSpec((1, d), lambda i, pos: (0, 0)),
                      pl.BlockSpec((1, d), lambda i, pos: (0, 0)),
                      pl.BlockSpec(memory_space=pl.ANY)],
            out_specs=pl.BlockSpec((tm, d), lambda i, pos: (i, 0)),
            scratch_shapes=[pltpu.VMEM((ahead + 1, tm * pitch, LANES), F32),
                            pltpu.SemaphoreType.DMA((ahead + 1,))]),
        out_shape=jax.ShapeDtypeStruct((t, d), F32),
        compiler_params=_params(1),
        name="moe_combine_ln2",
    )(pos, x1s, ln_g, ln_b, ys)


def _moe(x1s, key, counts, w1, w3, w2, ln_g, ln_b, *, alpha, tm, key_shift):
    t = key.shape[1]
    tm = min(tm, t)
    counts = counts.reshape(CLASS_ROWS)[:N_CLASSES]
    pos = _positions(key.reshape(8, t // 8), _class_starts(counts, tm), key_shift=key_shift)
    pos = pos.reshape(t)
    src, e_a, e_b, used, n_tiles = _plan(pos, counts, tm=tm)
    ys = _experts(x1s, src, e_a, e_b, used, w1, w3, w2, tm=tm, n_tiles=n_tiles)
    return _combine(x1s, ys, pos, ln_g, ln_b, alpha=alpha, tm=min(COMBINE_TILE, t))


def kernel(x, mem, w_in, b_gate, w_mem_kv, da_lambda, da_norm_g, ret_gn_g, ret_gn_b, w_branch, w_o,
           ln1_g, ln1_b, w_rg, b_rg, w_re, b_re, w1, w3, w2, ln2_g, ln2_b):
    batch, seq, d = x.shape
    mem_len = mem.shape[1]
    depth = w_in.shape[0]
    alpha = (2.0 * depth) ** 0.25
    t = batch * seq
    x2 = x.reshape(t, d)
    for l in range(depth):
        lambda_init = 0.8 - 0.6 * math.exp(-0.3 * l)
        proj = _matmul(x2, w_in[l], tm=2048, tn=1024, name="in_proj")
        mem_kv = _matmul(mem.reshape(batch * mem_len, d), w_mem_kv[l],
                         tm=1024, tn=1024, name="mem_kv_proj")
        y_da = _diff_attention(proj, da_lambda[l], da_norm_g[l], batch=batch, seq=seq, d=d,
                               lambda_init=lambda_init, tile=256)
        y_ret = _retention(proj, ret_gn_g[l], ret_gn_b[l], batch=batch, seq=seq, d=d)
        y_mem = _memory_attention(proj, mem_kv, batch=batch, seq=seq, mem_len=mem_len, d=d, tq=1024)
        gap = EXPERT_ROW0 - N_GROUPS
        tail = LANES - EXPERT_ROW0 - N_EXPERTS
        wr = jnp.concatenate([w_rg[l], jnp.zeros((d, gap), F32), w_re[l],
                              jnp.zeros((d, tail), F32)], axis=1)
        br = jnp.concatenate([b_rg[l], jnp.zeros((gap,), F32), b_re[l],
                              jnp.zeros((tail,), F32)]).reshape(1, LANES)
        key_shift = max(t - 1, 1).bit_length()
        assert N_CLASSES << key_shift < 1 << 24
        x1s, key, counts = _merge(x2, y_da, y_ret, y_mem, proj, b_gate[l].reshape(N_BRANCH, d),
                                  w_branch[l].astype(BF16), w_o[l].astype(BF16),
                                  ln1_g[l].reshape(1, d), ln1_b[l].reshape(1, d), wr, br,
                                  alpha=alpha, tm=512, key_shift=key_shift)
        x2 = _moe(x1s, key, counts, w1[l].astype(BF16), w3[l].astype(BF16), w2[l].astype(BF16),
                  ln2_g[l].reshape(1, d), ln2_b[l].reshape(1, d), alpha=alpha, tm=256,
                  key_shift=key_shift)
    return x2.reshape(batch, seq, d)
```

```python
import functools
import math

import jax
import jax.numpy as jnp
import numpy as np
from jax import lax
from jax.experimental import pallas as pl
from jax.experimental.pallas import tpu as pltpu

BF16 = jnp.bfloat16
F32 = jnp.float32

DA_HEADS = 8
RET_HEADS = 4
RET_CHUNK = 128
MEM_HEADS = 4
N_BRANCH = 3
N_GROUPS = 4
EXPERTS_PER_GROUP = 4
N_EXPERTS = N_GROUPS * EXPERTS_PER_GROUP
EPS = 1e-5
LANES = 128
VMEM_LIMIT = 56 * 1024 * 1024

_NT = (((1,), (1,)), ((), ()))


def _params(n_axes):
    return pltpu.CompilerParams(dimension_semantics=("arbitrary",) * n_axes,
                                vmem_limit_bytes=VMEM_LIMIT)


def _layer_norm(z, g, b):
    mu = jnp.mean(z, axis=-1, keepdims=True)
    zc = z - mu
    var = jnp.mean(zc * zc, axis=-1, keepdims=True)
    return zc * lax.rsqrt(var + EPS) * g + b


def _matmul_kernel(x_ref, w_ref, o_ref):
    o_ref[...] = jnp.dot(x_ref[...].astype(BF16), w_ref[...].astype(BF16),
                         preferred_element_type=F32).astype(o_ref.dtype)


def _matmul(x, w, *, tm, tn, name):
    m, k = x.shape
    n = w.shape[1]
    tm, tn = min(tm, m), min(tn, n)
    return pl.pallas_call(
        _matmul_kernel,
        grid=(m // tm, n // tn),
        in_specs=[pl.BlockSpec((tm, k), lambda i, j: (i, 0)),
                  pl.BlockSpec((k, tn), lambda i, j: (0, j))],
        out_specs=pl.BlockSpec((tm, tn), lambda i, j: (i, j)),
        out_shape=jax.ShapeDtypeStruct((m, n), BF16),
        compiler_params=_params(2),
        name=name,
    )(x, w)


DA_ONES_ROWS = 16
DA_MASKED = -1e30
DA_CHUNK_ROWS = 32
LOG2_E = 1.4426950408889634


def _da_schedule(n_blk):
    full = [(qi, p) for qi in range(n_blk) for p in range(qi // 2)]
    order = full + [(qi, qi // 2) for qi in range(n_blk)]
    order.append(order[-1])
    table = np.asarray(order, np.int32)
    return len(full), table[:, 0], table[:, 1]


def _da_kernel(sq_ref, sp_ref, slope_ref, lam_ref, g_ref, q_ref, k_ref, v_ref, o_ref,
               ka0_ref, ka1_ref, qa0_ref, qa1_ref, vt_ref, s_ref, acc0_ref, acc1_ref, m0_ref, m1_ref,
               *, tile, dk, n_blk, n_full, lambda_init):
    h = pl.program_id(1)
    a_log2 = slope_ref[h] * LOG2_E
    w = 2 * dk
    ka_refs = (ka0_ref, ka1_ref)
    qa_refs = (qa0_ref, qa1_ref)
    acc_refs = (acc0_ref, acc1_ref)
    m_refs = (m0_ref, m1_ref)
    lane = lax.broadcasted_iota(jnp.int32, (tile, w), 1)
    pos = lax.broadcasted_iota(jnp.int32, (tile, w), 0).astype(F32)
    pieces = []
    rest = jnp.full((tile, w), a_log2, F32)
    for _ in range(3):
        piece = rest.astype(BF16).astype(F32)
        pieces.append(piece)
        rest = rest - piece
    in_comp, k_bias, q_bias = [], [], []
    for c in range(2):
        free = dk * (1 - c)
        in_comp.append((lane >= c * dk) & (lane < (c + 1) * dk))
        kb = jnp.where((lane >= free) & (lane < free + 3), pos, 0.0)
        qb = jnp.where((lane >= free + 3) & (lane < free + 6), -pos, 0.0)
        for i in range(3):
            kb = jnp.where(lane == free + 3 + i, pieces[i], kb)
            qb = jnp.where(lane == free + i, pieces[i], qb)
        k_bias.append(kb)
        q_bias.append(qb)

    def prepare(pp, _):
        for jj in range(2):
            blk = 2 * pp + jj
            rows = pl.ds(pl.multiple_of(blk * tile, tile), tile)
            kf = k_ref[rows, :].astype(F32)
            qf = q_ref[rows, :].astype(F32) * (dk ** -0.5 * LOG2_E)
            for c in range(2):
                ka_refs[c][rows, :] = jnp.where(in_comp[c], kf, k_bias[c]).astype(BF16)
                qa_refs[c][rows, :] = jnp.where(in_comp[c], qf, q_bias[c]).astype(BF16)
                acc_refs[c][blk] = jnp.zeros((w + DA_ONES_ROWS, tile), F32)
                m_refs[c][blk] = jnp.full((1, tile), -jnp.inf, F32)
            vt_ref[pp, 0:w, jj * tile:(jj + 1) * tile] = v_ref[rows, :].astype(F32).T.astype(BF16)
        vt_ref[pp, w:w + DA_ONES_ROWS, :] = jnp.ones((DA_ONES_ROWS, 2 * tile), BF16)
        return 0

    lax.fori_loop(0, n_blk // 2, prepare, 0)

    lv = lam_ref[...]
    lam = (jnp.exp(jnp.sum(lv[0:1] * lv[1:2], axis=1, keepdims=True))
           - jnp.exp(jnp.sum(lv[2:3] * lv[3:4], axis=1, keepdims=True)) + lambda_init)
    out_gain = jnp.broadcast_to(g_ref[...] * (1.0 - lambda_init), (w, tile))
    chunk = DA_CHUNK_ROWS
    n_chunks = tile // chunk
    rel0 = (lax.broadcasted_iota(jnp.int32, (chunk, tile), 0)
            - lax.broadcasted_iota(jnp.int32, (chunk, tile), 1))

    def issue_scores(t, slot, n_kv):
        qi, p = sq_ref[t], sp_ref[t]
        q_rows = pl.ds(pl.multiple_of(qi * tile, tile), tile)
        for jj in range(n_kv):
            k_rows = pl.ds(pl.multiple_of((2 * p + jj) * tile, tile), tile)
            for c in range(2):
                s_ref[slot, 2 * jj + c] = lax.dot_general(
                    ka_refs[c][k_rows, :], qa_refs[c][q_rows, :], _NT, preferred_element_type=F32)

    def consume(t, slot, diag):
        n_kv = 1 if diag == 0 else 2
        qi, p = sq_ref[t], sp_ref[t]
        offs = [a_log2 * ((2 * p + jj - qi) * tile).astype(F32) for jj in range(n_kv)]
        accs = []
        for c in range(2):
            tiles = [s_ref.at[slot, 2 * jj + c] for jj in range(n_kv)]
            m_old = m_refs[c][qi]
            m_new = m_old
            for jj in range(n_kv):
                col_max = None
                for r in range(n_chunks):
                    rows = slice(r * chunk, (r + 1) * chunk)
                    x = tiles[jj][rows, :]
                    if jj == diag:
                        x = jnp.where(rel0 <= -r * chunk, x, DA_MASKED)
                        tiles[jj][rows, :] = x
                    col_max = x if col_max is None else jnp.maximum(col_max, x)
                m_new = jnp.maximum(m_new, jnp.max(col_max, axis=0, keepdims=True) + offs[jj])
            pts = []
            for jj in range(n_kv):
                shift = m_new - offs[jj]
                for r in range(n_chunks):
                    x = tiles[jj][r * chunk:(r + 1) * chunk, :]
                    pts.append(jnp.exp2(x - shift).astype(BF16))
            vt = vt_ref[p] if n_kv == 2 else vt_ref[p, :, 0:tile]
            pv = jnp.dot(vt, jnp.concatenate(pts, axis=0), preferred_element_type=F32)
            acc = jnp.exp2(m_old - m_new) * acc_refs[c][qi] + pv
            if diag is None:
                acc_refs[c][qi] = acc
                m_refs[c][qi] = m_new
            else:
                accs.append(acc)
        if diag is not None:
            a0, a1 = accs
            ot = a0[0:w] / a0[w:w + 1] - lam * (a1[0:w] / a1[w:w + 1])
            ot = ot * lax.rsqrt(jnp.mean(ot * ot, axis=0, keepdims=True) + EPS) * out_gain
            q_rows = pl.ds(pl.multiple_of(qi * tile, tile), tile)
            o_ref[q_rows, :] = ot.T.astype(o_ref.dtype)

    def two_entries(i, base, last):
        t = base + 2 * i
        issue_scores(t + 1, 1, 2)
        consume(t, 0, 0 if last else None)
        issue_scores(t + 2, 0, 1 if last else 2)
        consume(t + 1, 1, 1 if last else None)
        return 0

    issue_scores(0, 0, 2 if n_full else 1)
    lax.fori_loop(0, n_full // 2, lambda i, _: two_entries(i, 0, False), 0)
    lax.fori_loop(0, n_blk // 2, lambda i, _: two_entries(i, n_full, True), 0)


def _diff_attention(proj, lam_vecs, norm_g, *, batch, seq, d, lambda_init, tile):
    dk = d // (2 * DA_HEADS)
    w = 2 * dk
    tile = min(tile, seq)
    assert tile <= 256 and seq % (2 * tile) == 0
    n_blk = seq // tile
    hpb = d // w
    slopes = 2.0 ** (-8.0 * (jnp.arange(DA_HEADS, dtype=F32) + 1.0) / DA_HEADS)
    n_full, sched_q, sched_p = _da_schedule(n_blk)
    kern = functools.partial(_da_kernel, tile=tile, dk=dk, n_blk=n_blk, n_full=n_full,
                             lambda_init=lambda_init)
    smem = pl.BlockSpec(memory_space=pltpu.SMEM)
    return pl.pallas_call(
        kern,
        grid=(batch, DA_HEADS),
        in_specs=[smem, smem, smem,
                  pl.BlockSpec((4, dk), lambda b, h: (0, 0)),
                  pl.BlockSpec((w, 1), lambda b, h: (0, 0)),
                  pl.BlockSpec((seq, w), lambda b, h: (b, h)),
                  pl.BlockSpec((seq, w), lambda b, h: (b, hpb + h)),
                  pl.BlockSpec((seq, w), lambda b, h: (b, 2 * hpb + h))],
        out_specs=pl.BlockSpec((seq, w), lambda b, h: (b, h)),
        out_shape=jax.ShapeDtypeStruct((batch * seq, d), BF16),
        scratch_shapes=[pltpu.VMEM((seq, w), BF16), pltpu.VMEM((seq, w), BF16),
                        pltpu.VMEM((seq, w), BF16), pltpu.VMEM((seq, w), BF16),
                        pltpu.VMEM((n_blk // 2, w + DA_ONES_ROWS, 2 * tile), BF16),
                        pltpu.VMEM((2, 4, tile, tile), F32),
                        pltpu.VMEM((n_blk, w + DA_ONES_ROWS, tile), F32),
                        pltpu.VMEM((n_blk, w + DA_ONES_ROWS, tile), F32),
                        pltpu.VMEM((n_blk, 1, tile), F32),
                        pltpu.VMEM((n_blk, 1, tile), F32)],
        compiler_params=_params(2),
        name="diff_attention",
    )(jnp.asarray(sched_q), jnp.asarray(sched_p), slopes, lam_vecs, norm_g.reshape(w, 1),
      proj, proj, proj)


def _ret_kernel(cd_ref, intra_ref, qd_ref, kd_ref, gng_ref, gnb_ref,
                q_ref, k_ref, v_ref, g_ref, o_ref, state_ref, *, n_chunks, dk, dv):
    c_len = RET_CHUNK
    heads = range(RET_HEADS)
    state_ref[...] = jnp.zeros_like(state_ref)

    def chunk(c, _):
        rows = pl.ds(pl.multiple_of(c * c_len, c_len), c_len)
        q = [q_ref[rows, h * dk:(h + 1) * dk] for h in heads]
        k = [k_ref[rows, h * dk:(h + 1) * dk] for h in heads]
        v = [v_ref[rows, h * dv:(h + 1) * dv] for h in heads]
        state = [state_ref[h] for h in heads]
        s = [lax.dot_general(q[h], k[h], _NT, preferred_element_type=F32) for h in heads]
        cross = [jnp.dot(q[h], state[h].astype(BF16), preferred_element_type=F32) for h in heads]
        kv = [jnp.dot((k[h].astype(F32) * kd_ref[h]).T.astype(BF16), v[h],
                      preferred_element_type=F32) for h in heads]
        o = [jnp.dot((s[h] * intra_ref[h]).astype(BF16), v[h], preferred_element_type=F32)
             + cross[h] * qd_ref[h] for h in heads]
        for h in heads:
            state_ref[h] = state[h] * cd_ref[h] + kv[h]
            cols = slice(h * dv, (h + 1) * dv)
            mu = jnp.mean(o[h], axis=1, keepdims=True)
            oc = o[h] - mu
            var = jnp.mean(oc * oc, axis=1, keepdims=True)
            y = oc * lax.rsqrt(var + EPS) * gng_ref[:, cols] + gnb_ref[:, cols]
            gate = g_ref[rows, cols].astype(F32)
            y = gate * (1.0 / (1.0 + jnp.exp(-gate))) * y
            o_ref[rows, cols] = y.astype(o_ref.dtype)
        return 0

    lax.fori_loop(0, n_chunks, chunk, 0, unroll=2)


def _retention(proj, gn_g, gn_b, *, batch, seq, d):
    dv = d // RET_HEADS
    dk = dv // 2
    c_len = RET_CHUNK
    n_chunks = seq // c_len
    log_g = jnp.log(1.0 - 2.0 ** (-5.0 - jnp.arange(RET_HEADS, dtype=F32)))
    idx = jnp.arange(c_len, dtype=F32)
    rel = idx[:, None] - idx[None, :]
    intra = jnp.where(rel[None] >= 0, jnp.exp(rel[None] * log_g[:, None, None]), 0.0) * dk ** -0.5
    q_decay = jnp.exp((idx[None, :] + 1.0) * log_g[:, None])[:, :, None]
    k_decay = (jnp.exp((c_len - 1.0 - idx[None, :]) * log_g[:, None]) * dk ** -0.5)[:, :, None]
    chunk_decay = jnp.exp(c_len * log_g)
    hk = RET_HEADS * dk
    q_blk = 3 * d // hk
    k_blk = q_blk + 1
    v_blk = (3 * d + 2 * hk) // d
    g_blk = v_blk + 1
    kern = functools.partial(_ret_kernel, n_chunks=n_chunks, dk=dk, dv=dv)
    const3 = lambda b: (0, 0, 0)
    return pl.pallas_call(
        kern,
        grid=(batch,),
        in_specs=[pl.BlockSpec(memory_space=pltpu.SMEM),
                  pl.BlockSpec((RET_HEADS, c_len, c_len), const3),
                  pl.BlockSpec((RET_HEADS, c_len, 1), const3),
                  pl.BlockSpec((RET_HEADS, c_len, 1), const3),
                  pl.BlockSpec((1, d), lambda b: (0, 0)),
                  pl.BlockSpec((1, d), lambda b: (0, 0)),
                  pl.BlockSpec((seq, hk), lambda b: (b, q_blk)),
                  pl.BlockSpec((seq, hk), lambda b: (b, k_blk)),
                  pl.BlockSpec((seq, d), lambda b: (b, v_blk)),
                  pl.BlockSpec((seq, d), lambda b: (b, g_blk))],
        out_specs=pl.BlockSpec((seq, d), lambda b: (b, 0)),
        out_shape=jax.ShapeDtypeStruct((batch * seq, d), BF16),
        scratch_shapes=[pltpu.VMEM((RET_HEADS, dk, dv), F32)],
        compiler_params=_params(1),
        name="retention",
    )(chunk_decay, intra, q_decay, k_decay, gn_g.reshape(1, d), gn_b.reshape(1, d),
      proj, proj, proj, proj)


def _mem_kernel(q_ref, kv_ref, o_ref, *, d, dh):
    for h in range(MEM_HEADS):
        q = q_ref[:, h * dh:(h + 1) * dh]
        k = kv_ref[:, h * dh:(h + 1) * dh]
        v = kv_ref[:, d + h * dh:d + (h + 1) * dh]
        s = lax.dot_general(q, k, _NT, preferred_element_type=F32) * dh ** -0.5
        p = jnp.exp(s - jnp.max(s, axis=1, keepdims=True))
        l = jnp.sum(p, axis=1, keepdims=True)
        o = jnp.dot(p.astype(BF16), v, preferred_element_type=F32) / l
        o_ref[:, h * dh:(h + 1) * dh] = o.astype(o_ref.dtype)


def _memory_attention(proj, mem_kv, *, batch, seq, mem_len, d, tq):
    dh = d // MEM_HEADS
    tq = min(tq, seq)
    nq = seq // tq
    q_blk = 6
    kern = functools.partial(_mem_kernel, d=d, dh=dh)
    return pl.pallas_call(
        kern,
        grid=(batch, nq),
        in_specs=[pl.BlockSpec((tq, d), lambda b, i: (b * nq + i, q_blk)),
                  pl.BlockSpec((mem_len, 2 * d), lambda b, i: (b, 0))],
        out_specs=pl.BlockSpec((tq, d), lambda b, i: (b * nq + i, 0)),
        out_shape=jax.ShapeDtypeStruct((batch * seq, d), BF16),
        compiler_params=_params(2),
        name="memory_attention",
    )(proj, mem_kv)


def _merge_kernel(x_ref, yda_ref, yret_ref, ymem_ref, g0_ref, g1_ref, g2_ref, bg_ref,
                  wb_ref, wo_ref, lng_ref, lnb_ref, wrh_ref, wrl_ref, br_ref, triu_ref,
                  x1_ref, key_ref, cnt_ref, carry_ref, *, alpha, key_shift):
    @pl.when(pl.program_id(0) == 0)
    def _():
        carry_ref[...] = jnp.zeros_like(carry_ref)

    ys = (yda_ref, yret_ref, ymem_ref)
    gs = (g0_ref, g1_ref, g2_ref)
    merged = None
    for n in range(N_BRANCH):
        gate = gs[n][...].astype(F32) + bg_ref[n:n + 1, :]
        gate = 1.0 / (1.0 + jnp.exp(-gate))
        term = gate * jnp.dot(ys[n][...], wb_ref[n], preferred_element_type=F32)
        merged = term if merged is None else merged + term
    out = jnp.dot(merged.astype(BF16), wo_ref[...], preferred_element_type=F32)
    x1 = _layer_norm(alpha * x_ref[...] + out, lng_ref[...], lnb_ref[...])
    tm, n_blocks = x1.shape[0], x1.shape[1] // LANES
    pitch = _slab_pitch(n_blocks)
    _to_slabs(x1_ref, x1, tm, pitch)
    x_hi = x1.astype(BF16)
    x_lo = (x1 - x_hi.astype(F32)).astype(BF16)
    logits = (jnp.dot(x_hi, wrh_ref[...], preferred_element_type=F32)
              + jnp.dot(x_lo, wrh_ref[...], preferred_element_type=F32)
              + jnp.dot(x_hi, wrl_ref[...], preferred_element_type=F32) + br_ref[...])
    record, key, total = _route_tokens(logits.T, triu_ref[...], carry_ref[...], key_shift)
    carry_ref[...] = total
    cnt_ref[...] = total.astype(jnp.int32)
    key_ref[...] = key
    x1_ref[pl.ds(n_blocks, tm, stride=pitch), :] = record
    _zero_slab_padding(x1_ref, tm, pitch, n_blocks + 1)


def _merge(x2, y_da, y_ret, y_mem, proj, b_gate, wb, wo, ln_g, ln_b, wr, br, *, alpha, tm,
           key_shift):
    t, d = x2.shape
    tm = min(tm, t)
    triu = jnp.triu(jnp.ones((tm, tm), BF16))
    gate_blk = 7
    row = lambda i: (i, 0)
    const2 = lambda i: (0, 0)
    once = pl.Buffered(1)
    x_pitch = _slab_pitch(d // LANES)
    wr_hi = wr.astype(BF16)
    wr_lo = (wr - wr_hi.astype(F32)).astype(BF16)
    kern = functools.partial(_merge_kernel, alpha=alpha, key_shift=key_shift)
    return pl.pallas_call(
        kern,
        grid=(t // tm,),
        in_specs=[pl.BlockSpec((tm, d), row), pl.BlockSpec((tm, d), row),
                  pl.BlockSpec((tm, d), row), pl.BlockSpec((tm, d), row),
                  pl.BlockSpec((tm, d), lambda i: (i, gate_blk)),
                  pl.BlockSpec((tm, d), lambda i: (i, gate_blk + 1)),
                  pl.BlockSpec((tm, d), lambda i: (i, gate_blk + 2)),
                  pl.BlockSpec((N_BRANCH, d), const2),
                  pl.BlockSpec((N_BRANCH, d, d), lambda i: (0, 0, 0), pipeline_mode=once),
                  pl.BlockSpec((d, d), const2, pipeline_mode=once),
                  pl.BlockSpec((1, d), const2), pl.BlockSpec((1, d), const2),
                  pl.BlockSpec((d, LANES), const2), pl.BlockSpec((d, LANES), const2),
                  pl.BlockSpec((1, LANES), const2),
                  pl.BlockSpec((tm, tm), const2, pipeline_mode=once)],
        out_specs=[pl.BlockSpec((tm * x_pitch, LANES), row),
                   pl.BlockSpec((1, tm), lambda i: (0, i)),
                   pl.BlockSpec((CLASS_ROWS, 1), const2)],
        out_shape=[jax.ShapeDtypeStruct((t * x_pitch, LANES), F32),
                   jax.ShapeDtypeStruct((1, t), jnp.int32),
                   jax.ShapeDtypeStruct((CLASS_ROWS, 1), jnp.int32)],
        scratch_shapes=[pltpu.VMEM((CLASS_ROWS, 1), F32)],
        compiler_params=_params(1),
        name="merge_ln1_router",
    )(x2, y_da, y_ret, y_mem, proj, proj, proj, b_gate, wb, wo, ln_g, ln_b, wr_hi, wr_lo, br,
      triu)


PAIRS = ((0, 1), (0, 2), (0, 3), (1, 2), (1, 3), (2, 3))
N_CLASSES = N_GROUPS * len(PAIRS)
ROUTE_CLS, ROUTE_RANK, ROUTE_WA, ROUTE_WB = 0, 1, 2, 3
EXPERT_ROW0 = 8
CLASS_ROWS = 32
EXPERT_GATHER_AHEAD = 2
COMBINE_GATHER_AHEAD = 2
COMBINE_TILE = 512


def _route_tokens(lgt, triu, carry, key_shift):
    tm = lgt.shape[1]
    row = lambda r: lgt[r:r + 1, :]

    def first_max(vals):
        best = vals[0]
        for v in vals[1:]:
            best = jnp.maximum(best, v)
        idx = jnp.full_like(best, float(len(vals) - 1))
        for k in range(len(vals) - 2, -1, -1):
            idx = jnp.where(vals[k] == best, float(k), idx)
        return best, idx

    groups = [row(g) for g in range(N_GROUPS)]
    gmax, g_idx = first_max(groups)
    denom = jnp.exp(groups[0] - gmax)
    for g in groups[1:]:
        denom = denom + jnp.exp(g - gmax)
    p_top = 1.0 / denom
    sel = []
    for k in range(EXPERTS_PER_GROUP):
        v = row(EXPERT_ROW0 + (N_GROUPS - 1) * EXPERTS_PER_GROUP + k)
        for g in range(N_GROUPS - 2, -1, -1):
            v = jnp.where(g_idx == float(g), row(EXPERT_ROW0 + g * EXPERTS_PER_GROUP + k), v)
        sel.append(v)
    e1, i1 = first_max(sel)
    rest = [jnp.where(i1 == float(k), -jnp.inf, sel[k]) for k in range(EXPERTS_PER_GROUP)]
    e2, i2 = first_max(rest)
    r = jnp.exp(e2 - e1)
    w1 = p_top / (1.0 + r)
    w2 = p_top * r / (1.0 + r)

    a_loc = jnp.minimum(i1, i2)
    b_loc = jnp.maximum(i1, i2)
    pair = jnp.where(a_loc == 0.0, b_loc - 1.0, jnp.where(a_loc == 1.0, b_loc + 1.0, 5.0))
    cls = g_idx * float(len(PAIRS)) + pair
    top_is_a = i1 < i2
    w_a = jnp.where(top_is_a, w1, w2)
    w_b = jnp.where(top_is_a, w2, w1)

    cls_row = lax.broadcasted_iota(jnp.int32, (CLASS_ROWS, tm), 0).astype(F32)
    onehot = jnp.where(cls_row == cls, 1.0, 0.0)
    prefix = carry + jnp.dot(onehot.astype(BF16), triu, preferred_element_type=F32)
    rank = jnp.sum(onehot * prefix, axis=0, keepdims=True) - 1.0
    total = carry + jnp.sum(onehot, axis=1, keepdims=True)
    key = (cls * float(1 << key_shift) + rank).astype(jnp.int32)
    record = jnp.concatenate([cls, rank, w_a, w_b, jnp.zeros((LANES - 4, tm), F32)], axis=0)
    return record.T, key, total


def _class_starts(counts, tm):
    padded = (counts + (tm - 1)) // tm * tm
    return jnp.cumsum(padded) - padded


def _positions_kernel(starts_ref, key_ref, pos_ref, *, key_shift):
    key = key_ref[...]
    cls = lax.shift_right_logical(key, key_shift)
    pos = key & ((1 << key_shift) - 1)
    for c in range(N_CLASSES):
        pos = jnp.where(cls == c, pos + starts_ref[c], pos)
    pos_ref[...] = pos


def _positions(key, starts, *, key_shift):
    return pl.pallas_call(
        functools.partial(_positions_kernel, key_shift=key_shift),
        in_specs=[pl.BlockSpec(memory_space=pltpu.SMEM), pl.BlockSpec(memory_space=pltpu.VMEM)],
        out_specs=pl.BlockSpec(memory_space=pltpu.VMEM),
        out_shape=jax.ShapeDtypeStruct(key.shape, jnp.int32),
        name="moe_positions",
    )(starts, key)


def _plan_kernel(pos_ref, cnt_ref, src_ref, ea_ref, eb_ref, used_ref, *, n_tokens, n_tiles, tm):
    tm_shift = tm.bit_length() - 1
    row0 = jnp.int32(0)
    tile0 = jnp.int32(0)
    last_a = jnp.int32(0)
    last_b = jnp.int32(1)
    for c in range(N_CLASSES):
        e_a = c // len(PAIRS) * EXPERTS_PER_GROUP + PAIRS[c % len(PAIRS)][0]
        e_b = c // len(PAIRS) * EXPERTS_PER_GROUP + PAIRS[c % len(PAIRS)][1]
        cnt = cnt_ref[c]
        n_cls_tiles = lax.shift_right_logical(cnt + (tm - 1), tm_shift)

        def mark(i, _, e_a=e_a, e_b=e_b):
            ea_ref[i] = e_a
            eb_ref[i] = e_b
            used_ref[i] = 1
            return 0

        lax.fori_loop(tile0, tile0 + n_cls_tiles, mark, 0)
        has = n_cls_tiles > 0
        last_a = jnp.where(has, e_a, last_a)
        last_b = jnp.where(has, e_b, last_b)
        row_end = row0 + n_cls_tiles * tm

        def pad(r, _):
            src_ref[r] = 0
            return 0

        lax.fori_loop(row0 + cnt, row_end, pad, 0)
        row0 = row_end
        tile0 = tile0 + n_cls_tiles

    def unused(i, _):
        ea_ref[i] = last_a
        eb_ref[i] = last_b
        used_ref[i] = 0
        return 0

    lax.fori_loop(tile0, n_tiles, unused, 0)
    lax.fori_loop(row0, n_tiles * tm, pad, 0)

    def put(t, _):
        src_ref[pos_ref[t]] = t
        return 0

    lax.fori_loop(0, n_tokens, put, 0, unroll=8)


def _plan(pos, counts, *, tm):
    n_tokens = pos.shape[0]
    assert tm & (tm - 1) == 0
    n_tiles = n_tokens // tm + N_CLASSES - 1 + EXPERT_GATHER_AHEAD
    smem = pl.BlockSpec(memory_space=pltpu.SMEM)
    tiles = jax.ShapeDtypeStruct((n_tiles,), jnp.int32)
    src, e_a, e_b, used = pl.pallas_call(
        functools.partial(_plan_kernel, n_tokens=n_tokens, n_tiles=n_tiles, tm=tm),
        in_specs=[smem, smem], out_specs=[smem] * 4,
        out_shape=[jax.ShapeDtypeStruct((n_tiles * tm,), jnp.int32), tiles, tiles, tiles],
        name="moe_plan",
    )(pos, counts)
    return src, e_a, e_b, used, n_tiles


def _slab_pitch(n_blocks):
    return n_blocks + 4 if n_blocks % 8 == 0 else n_blocks


def _to_slabs(ref, value, n_rows, pitch, offset=0):
    for j in range(value.shape[1] // LANES):
        ref[pl.ds(offset + j, n_rows, stride=pitch), :] = value[:, j * LANES:(j + 1) * LANES]


def _zero_slab_padding(ref, n_rows, pitch, n_blocks):
    for j in range(n_blocks, pitch):
        ref[pl.ds(j, n_rows, stride=pitch), :] = jnp.zeros((n_rows, LANES), ref.dtype)


def _from_slabs(ref, n_rows, n_blocks, pitch, offset=0):
    return jnp.concatenate([ref[pl.ds(offset + j, n_rows, stride=pitch), :]
                            for j in range(n_blocks)], axis=1)


def _gather_copy(tab_ref, buf_ref, sem, row, r, rec, pitch):
    return pltpu.make_async_copy(tab_ref.at[pl.ds(pl.multiple_of(row * pitch, 4), rec), :],
                                 buf_ref.at[pl.ds(pl.multiple_of(r * pitch, 4), rec), :], sem)


def _gather_start(idx_ref, tab_ref, buf_ref, sem, tile, tm, rec, pitch):
    def issue(r, _):
        _gather_copy(tab_ref, buf_ref, sem, idx_ref[tile * tm + r], r, rec, pitch).start()
        return 0

    lax.fori_loop(0, tm, issue, 0, unroll=8)


def _gather_wait(tab_ref, buf_ref, sem, tm, rec):
    pltpu.make_async_copy(tab_ref.at[pl.ds(0, tm * rec), :], buf_ref.at[pl.ds(0, tm * rec), :],
                          sem).wait()


def _experts_kernel(src_ref, ea_ref, eb_ref, used_ref, x_ref, w1a_ref, w3a_ref, w2a_ref,
                    w1b_ref, w3b_ref, w2b_ref, o_ref, buf_ref, sem_ref, *, tm, n_tiles, n_blocks):
    i = pl.program_id(0)
    ahead = EXPERT_GATHER_AHEAD
    slot = lax.rem(i, ahead + 1)
    fill_slot = lax.rem(i + ahead, ahead + 1)
    used = lambda tile: used_ref[tile] != 0
    in_flight = (i < ahead) | used(jnp.maximum(i - ahead, 0))
    pitch = _slab_pitch(n_blocks)
    rec = n_blocks + 1

    @pl.when(i == 0)
    def _():
        for j in range(ahead):
            _gather_start(src_ref, x_ref, buf_ref.at[j], sem_ref.at[j], j, tm, rec, pitch)

    @pl.when(used(i))
    def _():
        _gather_wait(x_ref, buf_ref.at[slot], sem_ref.at[slot], tm, rec)
        x = _from_slabs(buf_ref.at[slot], tm, n_blocks, pitch).astype(BF16)
        record = buf_ref[slot, pl.ds(n_blocks, tm, stride=pitch), :]
        ups = [(jnp.dot(x, w1_ref[...], preferred_element_type=F32),
                jnp.dot(x, w3_ref[...], preferred_element_type=F32))
               for w1_ref, w3_ref in ((w1a_ref, w3a_ref), (w1b_ref, w3b_ref))]
        for r in range(tm):
            _gather_copy(x_ref, buf_ref.at[fill_slot], sem_ref.at[fill_slot],
                         src_ref[(i + ahead) * tm + r], r, rec, pitch).start(priority=r % 2)
        y = None
        for (a, b), w2_ref, lane in zip(ups, (w2a_ref, w2b_ref), (ROUTE_WA, ROUTE_WB)):
            hid = a * (1.0 / (1.0 + jnp.exp(-a))) * b
            term = record[:, lane:lane + 1] * jnp.dot(hid.astype(BF16), w2_ref[...],
                                                      preferred_element_type=F32)
            y = term if y is None else y + term
        _to_slabs(o_ref, y, tm, pitch)
        _zero_slab_padding(o_ref, tm, pitch, n_blocks)

    @pl.when(jnp.logical_not(used(i)))
    def _():
        @pl.when(in_flight)
        def _():
            _gather_wait(x_ref, buf_ref.at[slot], sem_ref.at[slot], tm, rec)

        o_ref[...] = jnp.zeros_like(o_ref)


def _experts(x1s, src, e_a, e_b, used, w1, w3, w2, *, tm, n_tiles):
    d, f = w1.shape[1], w1.shape[2]
    n_blocks = d // LANES
    pitch = _slab_pitch(n_blocks)
    up_a = pl.BlockSpec((None, d, f), lambda i, src, ea, eb, used: (ea[i], 0, 0))
    up_b = pl.BlockSpec((None, d, f), lambda i, src, ea, eb, used: (eb[i], 0, 0))
    down_a = pl.BlockSpec((None, f, d), lambda i, src, ea, eb, used: (ea[i], 0, 0))
    down_b = pl.BlockSpec((None, f, d), lambda i, src, ea, eb, used: (eb[i], 0, 0))
    return pl.pallas_call(
        functools.partial(_experts_kernel, tm=tm, n_tiles=n_tiles, n_blocks=n_blocks),
        grid_spec=pltpu.PrefetchScalarGridSpec(
            num_scalar_prefetch=4, grid=(n_tiles,),
            in_specs=[pl.BlockSpec(memory_space=pl.ANY),
                      up_a, up_a, down_a, up_b, up_b, down_b],
            out_specs=pl.BlockSpec((tm * pitch, LANES), lambda i, src, ea, eb, used: (i, 0)),
            scratch_shapes=[pltpu.VMEM((EXPERT_GATHER_AHEAD + 1, tm * pitch, LANES), F32),
                            pltpu.SemaphoreType.DMA((EXPERT_GATHER_AHEAD + 1,))]),
        out_shape=jax.ShapeDtypeStruct((n_tiles * tm * pitch, LANES), F32),
        compiler_params=_params(1),
        name="experts",
    )(src, e_a, e_b, used, x1s, w1, w3, w2, w1, w3, w2)


def _combine_kernel(pos_ref, x_ref, lng_ref, lnb_ref, ys_ref, o_ref, buf_ref, sem_ref,
                    *, tm, n_tiles, n_blocks, alpha, ahead):
    i = pl.program_id(0)
    pitch = _slab_pitch(n_blocks)
    rec = n_blocks
    n_buf = ahead + 1
    slot = lax.rem(i, n_buf)

    @pl.when(i == 0)
    def _():
        for j in range(max(ahead, 1)):
            _gather_start(pos_ref, ys_ref, buf_ref.at[j], sem_ref.at[j], j, tm, rec, pitch)

    _gather_wait(ys_ref, buf_ref.at[slot], sem_ref.at[slot], tm, rec)
    moe = _from_slabs(buf_ref.at[slot], tm, n_blocks, pitch)
    fill = lax.rem(i + ahead, n_buf)
    nxt = jnp.minimum(i + ahead, n_tiles - 1)
    for r in range(tm if ahead else 0):
        _gather_copy(ys_ref, buf_ref.at[fill], sem_ref.at[fill], pos_ref[nxt * tm + r], r, rec,
                     pitch).start(priority=r % 2)
    x1 = _from_slabs(x_ref, tm, n_blocks, pitch)
    o_ref[...] = _layer_norm(alpha * x1 + moe, lng_ref[...], lnb_ref[...])

    @pl.when(i == n_tiles - 1)
    def _():
        for j in range(1, ahead + 1):
            extra = lax.rem(i + j, n_buf)
            _gather_wait(ys_ref, buf_ref.at[extra], sem_ref.at[extra], tm, rec)


def _combine(x1s, ys, pos, ln_g, ln_b, *, alpha, tm):
    t = pos.shape[0]
    d = ln_g.shape[1]
    n_blocks = d // LANES
    pitch = _slab_pitch(n_blocks)
    n_tiles = t // tm
    ahead = min(COMBINE_GATHER_AHEAD, n_tiles - 1)
    return pl.pallas_call(
        functools.partial(_combine_kernel, tm=tm, n_tiles=n_tiles, n_blocks=n_blocks, alpha=alpha,
                          ahead=ahead),
        grid_spec=pltpu.PrefetchScalarGridSpec(
            num_scalar_prefetch=1, grid=(n_tiles,),
            in_specs=[pl.BlockSpec((tm * pitch, LANES), lambda i, pos: (i, 0)),
                      pl.BlockSpec((1, d), lambda i, pos: (0, 0)),
                      pl.BlockSpec((1, d), lambda i, pos: (0, 0)),
                      pl.BlockSpec(memory_space=pl.ANY)],
            out_specs=pl.BlockSpec((tm, d), lambda i, pos: (i, 0)),
            scratch_shapes=[pltpu.VMEM((ahead + 1, tm * pitch, LANES), F32),
                            pltpu.SemaphoreType.DMA((ahead + 1,))]),
        out_shape=jax.ShapeDtypeStruct((t, d), F32),
        compiler_params=_params(1),
        name="moe_combine_ln2",
    )(pos, x1s, ln_g, ln_b, ys)


def _moe(x1s, key, counts, w1, w3, w2, ln_g, ln_b, *, alpha, tm, key_shift):
    t = key.shape[1]
    tm = min(tm, t)
    counts = counts.reshape(CLASS_ROWS)[:N_CLASSES]
    pos = _positions(key.reshape(8, t // 8), _class_starts(counts, tm), key_shift=key_shift)
    pos = pos.reshape(t)
    src, e_a, e_b, used, n_tiles = _plan(pos, counts, tm=tm)
    ys = _experts(x1s, src, e_a, e_b, used, w1, w3, w2, tm=tm, n_tiles=n_tiles)
    return _combine(x1s, ys, pos, ln_g, ln_b, alpha=alpha, tm=min(COMBINE_TILE, t))


def kernel(x, mem, w_in, b_gate, w_mem_kv, da_lambda, da_norm_g, ret_gn_g, ret_gn_b, w_branch, w_o,
           ln1_g, ln1_b, w_rg, b_rg, w_re, b_re, w1, w3, w2, ln2_g, ln2_b):
    batch, seq, d = x.shape
    mem_len = mem.shape[1]
    depth = w_in.shape[0]
    alpha = (2.0 * depth) ** 0.25
    t = batch * seq
    x2 = x.reshape(t, d)
    for l in range(depth):
        lambda_init = 0.8 - 0.6 * math.exp(-0.3 * l)
        proj = _matmul(x2, w_in[l], tm=2048, tn=1024, name="in_proj")
        mem_kv = _matmul(mem.reshape(batch * mem_len, d), w_mem_kv[l],
                         tm=1024, tn=1024, name="mem_kv_proj")
        y_da = _diff_attention(proj, da_lambda[l], da_norm_g[l], batch=batch, seq=seq, d=d,
                               lambda_init=lambda_init, tile=256)
        y_ret = _retention(proj, ret_gn_g[l], ret_gn_b[l], batch=batch, seq=seq, d=d)
        y_mem = _memory_attention(proj, mem_kv, batch=batch, seq=seq, mem_len=mem_len, d=d, tq=1024)
        gap = EXPERT_ROW0 - N_GROUPS
        tail = LANES - EXPERT_ROW0 - N_EXPERTS
        wr = jnp.concatenate([w_rg[l], jnp.zeros((d, gap), F32), w_re[l],
                              jnp.zeros((d, tail), F32)], axis=1)
        br = jnp.concatenate([b_rg[l], jnp.zeros((gap,), F32), b_re[l],
                              jnp.zeros((tail,), F32)]).reshape(1, LANES)
        key_shift = max(t - 1, 1).bit_length()
        assert N_CLASSES << key_shift < 1 << 24
        x1s, key, counts = _merge(x2, y_da, y_ret, y_mem, proj, b_gate[l].reshape(N_BRANCH, d),
                                  w_branch[l].astype(BF16), w_o[l].astype(BF16),
                                  ln1_g[l].reshape(1, d), ln1_b[l].reshape(1, d), wr, br,
                                  alpha=alpha, tm=512, key_shift=key_shift)
        x2 = _moe(x1s, key, counts, w1[l].astype(BF16), w3[l].astype(BF16), w2[l].astype(BF16),
                  ln2_g[l].reshape(1, d), ln2_b[l].reshape(1, d), alpha=alpha, tm=256,
                  key_shift=key_shift)
    return x2.reshape(batch, seq, d)
```
